```python
import jax
import jax.numpy as jnp
from jax import lax
import numpy as np

D_MODEL = 1024
BATCH = 8
SEQ = 4096
DEPTH = 2

GRID_W = 64
CTX_LEN = 256
EPS = 1e-6
N_MOD = 6

HG_HEADS = 8
HG_DK = 128
HG_DV = 128
HG_WIDTH = HG_HEADS * HG_DK
HG_CHUNK = 32

CONV_WIDTH = D_MODEL
CONV_K = 3

N_HEADS = 16
N_KV_HEADS = 4
HEAD_DIM = 64
GQA_GROUP = N_HEADS // N_KV_HEADS
ATTN_WIDTH = N_HEADS * HEAD_DIM
KV_WIDTH = N_KV_HEADS * HEAD_DIM
Q_BLOCK = 128
ROPE_THETA = 10000.0
ATTN_SCALE = HEAD_DIM ** -0.5

N_BRANCHES = 3
IN_SPLITS = (HG_WIDTH,) * 5 + (CONV_WIDTH,) * 3 + (ATTN_WIDTH, KV_WIDTH, KV_WIDTH) + (D_MODEL,) * N_BRANCHES
IN_WIDTH = 5 * HG_WIDTH + 3 * CONV_WIDTH + ATTN_WIDTH + 2 * KV_WIDTH + N_BRANCHES * D_MODEL

N_EXPERTS = 16
EXPERT_FF = 1024
CAPACITY_FACTOR = 2

kernel_name = 'hybrid_diffusion_hgrn2_conv_gqa_ecmoe'


def rmsnorm(x, g):
    xf = x.astype(jnp.float32)
    y = xf * lax.rsqrt(jnp.mean(xf * xf, axis=-1, keepdims=True) + EPS)
    return (y * g).astype(x.dtype)


def modulation(cvec, w, b):
    mod = jax.nn.silu(cvec) @ w + b
    return jnp.split(mod[:, None, :], N_MOD, axis=-1)


def modulate(h, g, shift, scale):
    return rmsnorm(h, g) * (1 + scale) + shift


def project(h, w_in):
    p = h @ w_in
    return jnp.split(p, np.cumsum(IN_SPLITS)[:-1].tolist(), axis=-1)


def to_heads(t, d):
    return t.reshape(t.shape[0], t.shape[1], -1, d)


def axial_rope_tables(n):
    rows = n // GRID_W
    row = jnp.repeat(jnp.arange(rows), GRID_W).astype(jnp.float32)
    col = jnp.tile(jnp.arange(GRID_W), rows).astype(jnp.float32)
    inv = ROPE_THETA ** (-jnp.arange(0, HEAD_DIM // 2, 2, dtype=jnp.float32) / (HEAD_DIM // 2))
    ang = jnp.concatenate([row[:, None] * inv, col[:, None] * inv], axis=-1)
    return jnp.cos(ang), jnp.sin(ang)


def apply_rope(x, cos, sin):
    x1, x2 = x[..., 0::2], x[..., 1::2]
    c, s = cos[None, :, None, :], sin[None, :, None, :]
    return jnp.stack([x1 * c - x2 * s, x1 * s + x2 * c], axis=-1).reshape(x.shape).astype(x.dtype)


def hgrn_chunk_scan(q, log_f, k, v, s0):
    bsz, n, h, _ = q.shape
    dv = v.shape[-1]
    nc = n // HG_CHUNK

    def chunks(t):
        return t.astype(jnp.float32).reshape(bsz, nc, HG_CHUNK, h, t.shape[-1]).transpose(1, 0, 3, 2, 4)

    lower = jnp.tril(jnp.ones((HG_CHUNK, HG_CHUNK), dtype=bool))[:, :, None]

    def step(s, blk):
        qc, lfc, kc, vc = blk
        b = jnp.cumsum(lfc, axis=2)
        decay = jnp.exp(jnp.where(lower, b[:, :, :, None, :] - b[:, :, None, :, :], -jnp.inf))
        a = jnp.einsum('bhtd,bhsd,bhtsd->bhts', qc, kc, decay)
        o = jnp.einsum('bhts,bhse->bhte', a, vc) + jnp.einsum('bhtd,bhde->bhte', qc * jnp.exp(b), s)
        b_last = b[:, :, -1:, :]
        s_new = jnp.exp(b_last)[:, :, 0, :, None] * s + jnp.einsum('bhsd,bhse->bhde', kc * jnp.exp(b_last - b), vc)
        return s_new, o

    s_fin, o = lax.scan(step, s0, (chunks(q), chunks(log_f), chunks(k), chunks(v)))
    return o.transpose(1, 0, 3, 2, 4).reshape(bsz, n, h, dv), s_fin


def hgrn_direction(q, z, v, lb, s0, reverse):
    z = z.astype(jnp.float32)
    lb = lb.reshape(HG_HEADS, HG_DK)
    log_f = jnp.logaddexp(jnp.log(lb), jnp.log1p(-lb) + jax.nn.log_sigmoid(z))
    k = (1.0 - lb) * jax.nn.sigmoid(-z)
    if reverse:
        q, log_f, k, v = (jnp.flip(t, axis=1) for t in (q, log_f, k, v))
    o, s = hgrn_chunk_scan(q, log_f, k, v, s0)
    if reverse:
        o = jnp.flip(o, axis=1)
    return o, s


def hgrn_readout(o, g, norm_g):
    o = rmsnorm(o, norm_g.reshape(HG_HEADS, HG_DV))
    return o.reshape(o.shape[0], o.shape[1], HG_WIDTH).astype(g.dtype) * jax.nn.silu(g)


def depthwise_conv3(u, w):
    return lax.conv_general_dilated(u, w[:, None, :].astype(u.dtype), window_strides=(1,), padding='SAME',
                                    dimension_numbers=('NWC', 'WIO', 'NWC'), feature_group_count=CONV_WIDTH)


def gqa_softmax(qg, k, v):
    s = jnp.einsum('bqkgd,bskd->bkgqs', qg, k).astype(jnp.float32) * ATTN_SCALE
    p = jax.nn.softmax(s, axis=-1)
    return jnp.einsum('bkgqs,bskd->bqkgd', p.astype(v.dtype), v)


def attend_latent(q, k_lat, v_lat, k_ctx, v_ctx):
    bsz, n = q.shape[:2]
    k_all = jnp.concatenate([k_ctx, k_lat], axis=1)
    v_all = jnp.concatenate([v_ctx, v_lat], axis=1)
    qb = q.reshape(bsz, n // Q_BLOCK, Q_BLOCK, N_KV_HEADS, GQA_GROUP, HEAD_DIM).transpose(1, 0, 2, 3, 4, 5)
    o = lax.map(lambda qi: gqa_softmax(qi, k_all, v_all), qb)
    return o.transpose(1, 0, 2, 3, 4, 5).reshape(bsz, n, ATTN_WIDTH)


def attend_ctx(q, k, v):
    bsz, n = q.shape[:2]
    o = gqa_softmax(q.reshape(bsz, n, N_KV_HEADS, GQA_GROUP, HEAD_DIM), k, v)
    return o.reshape(bsz, n, ATTN_WIDTH)


def merge(pr, o_a, o_b, o_c, w_proj_a, w_proj_b, w_proj_c, w_out):
    ga, gb, gc = (jax.nn.sigmoid(t) for t in pr[11:14])
    y = ga * (o_a @ w_proj_a) + gb * (o_b @ w_proj_b) + gc * (o_c @ w_proj_c)
    return y @ w_out


def token_mixer(h_lat, h_ctx, w_in, lb_fwd, lb_bwd, hg_norm_g, conv_w, q_norm_g, k_norm_g,
                w_proj_a, w_proj_b, w_proj_c, w_out, cos, sin, ctx_out):
    pl = project(h_lat, w_in)
    pc = project(h_ctx, w_in)
    bsz = h_lat.shape[0]
    zero_state = jnp.zeros((bsz, HG_HEADS, HG_DK, HG_DV), jnp.float32)
    qc_h, ic_h = to_heads(pc[0], HG_DK), to_heads(pc[3], HG_DV)
    ql_h, il_h = to_heads(pl[0], HG_DK), to_heads(pl[3], HG_DV)
    oc_f, sc_f = hgrn_direction(qc_h, to_heads(pc[1], HG_DK), ic_h, lb_fwd, zero_state, False)
    oc_b, sc_b = hgrn_direction(qc_h, to_heads(pc[2], HG_DK), ic_h, lb_bwd, zero_state, True)
    ol_f, _ = hgrn_direction(ql_h, to_heads(pl[1], HG_DK), il_h, lb_fwd, sc_f, False)
    ol_b, _ = hgrn_direction(ql_h, to_heads(pl[2], HG_DK), il_h, lb_bwd, sc_b, True)
    a_lat = hgrn_readout(ol_f + ol_b, pl[4], hg_norm_g)
    b_lat = pl[5] * depthwise_conv3(pl[6] * pl[7], conv_w)
    q_l = apply_rope(rmsnorm(to_heads(pl[8], HEAD_DIM), q_norm_g), cos, sin)
    k_l = apply_rope(rmsnorm(to_heads(pl[9], HEAD_DIM), k_norm_g), cos, sin)
    v_l = to_heads(pl[10], HEAD_DIM)
    k_c = rmsnorm(to_heads(pc[9], HEAD_DIM), k_norm_g)
    v_c = to_heads(pc[10], HEAD_DIM)
    c_lat = attend_latent(q_l, k_l, v_l, k_c, v_c)
    y_lat = merge(pl, a_lat, b_lat, c_lat, w_proj_a, w_proj_b, w_proj_c, w_out)
    if not ctx_out:
        return y_lat, None
    a_ctx = hgrn_readout(oc_f + oc_b, pc[4], hg_norm_g)
    b_ctx = pc[5] * depthwise_conv3(pc[6] * pc[7], conv_w)
    q_c = rmsnorm(to_heads(pc[8], HEAD_DIM), q_norm_g)
    att_c = attend_ctx(q_c, k_c, v_c)
    y_ctx = merge(pc, a_ctx, b_ctx, att_c, w_proj_a, w_proj_b, w_proj_c, w_out)
    return y_lat, y_ctx


def expert_choice_ffn(h, router_w, w_gate, w_up, w_down):
    bsz, n, d = h.shape
    cap = CAPACITY_FACTOR * n // N_EXPERTS
    aff = jax.nn.softmax((h @ router_w).astype(jnp.float32), axis=-1)
    w_sel, idx = lax.top_k(aff.transpose(0, 2, 1), cap)
    xs = jax.vmap(lambda hb, ib: hb[ib])(h, idx)
    hid = jax.nn.silu(jnp.einsum('becd,edf->becf', xs, w_gate)) * jnp.einsum('becd,edf->becf', xs, w_up)
    ys = jnp.einsum('becf,efd->becd', hid, w_down) * w_sel[..., None].astype(h.dtype)
    return jax.vmap(lambda yb, ib: jnp.zeros((n, d), h.dtype).at[ib.reshape(-1)].add(yb.reshape(-1, d)))(ys, idx)


def setup_inputs(seed: int = 0) -> dict:
    key = jax.random.key(seed)
    ks = jax.random.split(key, 23)

    def nrm(k, shape, scale):
        return jax.random.normal(k, shape, jnp.float32) * scale

    return {
        'x': nrm(ks[0], (BATCH, SEQ, D_MODEL), 1.0),
        'c': nrm(ks[1], (BATCH, D_MODEL), 1.0),
        'ctx': nrm(ks[2], (BATCH, CTX_LEN, D_MODEL), 1.0),
        'c_ctx': nrm(ks[3], (D_MODEL,), 1.0),
        'ada_w': nrm(ks[4], (DEPTH, D_MODEL, N_MOD * D_MODEL), 0.5 * D_MODEL ** -0.5),
        'ada_b': nrm(ks[5], (DEPTH, N_MOD * D_MODEL), 0.02),
        'norm1_g': 1.0 + nrm(ks[6], (DEPTH, D_MODEL), 0.1),
        'norm2_g': 1.0 + nrm(ks[7], (DEPTH, D_MODEL), 0.1),
        'w_in': nrm(ks[8], (DEPTH, D_MODEL, IN_WIDTH), D_MODEL ** -0.5),
        'hg_lb_logits': nrm(ks[9], (DEPTH, 2, HG_WIDTH), 1.0),
        'hg_norm_g': 1.0 + nrm(ks[10], (DEPTH, HG_WIDTH), 0.1),
        'conv_w': nrm(ks[11], (DEPTH, CONV_K, CONV_WIDTH), CONV_K ** -0.5),
        'q_norm_g': 1.0 + nrm(ks[12], (DEPTH, HEAD_DIM), 0.1),
        'k_norm_g': 1.0 + nrm(ks[13], (DEPTH, HEAD_DIM), 0.1),
        'w_proj_a': nrm(ks[14], (DEPTH, HG_WIDTH, D_MODEL), HG_WIDTH ** -0.5),
        'w_proj_b': nrm(ks[15], (DEPTH, CONV_WIDTH, D_MODEL), CONV_WIDTH ** -0.5),
        'w_proj_c': nrm(ks[16], (DEPTH, ATTN_WIDTH, D_MODEL), ATTN_WIDTH ** -0.5),
        'w_out': nrm(ks[17], (DEPTH, D_MODEL, D_MODEL), D_MODEL ** -0.5),
        'router_w': nrm(ks[18], (DEPTH, D_MODEL, N_EXPERTS), D_MODEL ** -0.5),
        'w_gate': nrm(ks[19], (DEPTH, N_EXPERTS, D_MODEL, EXPERT_FF), D_MODEL ** -0.5),
        'w_up': nrm(ks[20], (DEPTH, N_EXPERTS, D_MODEL, EXPERT_FF), D_MODEL ** -0.5),
        'w_down': nrm(ks[21], (DEPTH, N_EXPERTS, EXPERT_FF, D_MODEL), EXPERT_FF ** -0.5),
        'final_norm_g': 1.0 + nrm(ks[22], (D_MODEL,), 0.1),
    }


def reference(x, c, ctx, c_ctx, ada_w, ada_b, norm1_g, norm2_g, w_in, hg_lb_logits, hg_norm_g, conv_w,
              q_norm_g, k_norm_g, w_proj_a, w_proj_b, w_proj_c, w_out, router_w, w_gate, w_up, w_down,
              final_norm_g):
    cos, sin = axial_rope_tables(x.shape[1])
    lb = jnp.cumsum(jax.nn.softmax(hg_lb_logits.astype(jnp.float32), axis=0), axis=0)
    lb = lb - lb[:1]
    h_ctx = ctx
    for l in range(DEPTH):
        ctx_out = l < DEPTH - 1
        sh_a, sc_a, g_a, sh_f, sc_f, g_f = modulation(c, ada_w[l], ada_b[l])
        csh_a, csc_a, cg_a, csh_f, csc_f, cg_f = modulation(c_ctx[None, :], ada_w[l], ada_b[l])
        y_lat, y_ctx = token_mixer(modulate(x, norm1_g[l], sh_a, sc_a), modulate(h_ctx, norm1_g[l], csh_a, csc_a),
                                   w_in[l], lb[l, 0], lb[l, 1], hg_norm_g[l], conv_w[l], q_norm_g[l], k_norm_g[l],
                                   w_proj_a[l], w_proj_b[l], w_proj_c[l], w_out[l], cos, sin, ctx_out)
        x = x + g_a * y_lat
        x = x + g_f * expert_choice_ffn(modulate(x, norm2_g[l], sh_f, sc_f), router_w[l], w_gate[l], w_up[l], w_down[l])
        if ctx_out:
            h_ctx = h_ctx + cg_a * y_ctx
            h_ctx = h_ctx + cg_f * expert_choice_ffn(modulate(h_ctx, norm2_g[l], csh_f, csc_f),
                                                     router_w[l], w_gate[l], w_up[l], w_down[l])
    return rmsnorm(x, final_norm_g)
```

```python
import functools

import jax
import jax.numpy as jnp
from jax import lax
from jax.experimental import pallas as pl
from jax.experimental.pallas import tpu as pltpu

F32 = jnp.float32
BF16 = jnp.bfloat16

EPS = 1e-6
N_MOD = 6
HG_DK = 128
GRID_W = 64
ROPE_THETA = 10000.0
CAPACITY_FACTOR = 2

LANES = 128
TOK_TILE = 256
HG_CHUNK = 64
KV_TILE = 256
HALO = 16
VMEM_LIMIT = 56 * 1024 * 1024

HIGHEST = lax.Precision.HIGHEST


def _dot(a, b, precision=None):
    return jnp.dot(a, b, preferred_element_type=F32, precision=precision)


def _dot_nt(a, b, precision=None):
    return lax.dot_general(a, b, (((1,), (1,)), ((), ())), preferred_element_type=F32, precision=precision)


def _dot_tn(a, b):
    return lax.dot_general(a, b, (((0,), (0,)), ((), ())), preferred_element_type=F32)


def _sigmoid(x):
    return 1.0 / (1.0 + jnp.exp(-x))


def _params(sem, vmem=VMEM_LIMIT):
    return pltpu.CompilerParams(dimension_semantics=sem, vmem_limit_bytes=vmem)


def _mod_kernel(c_ref, w_ref, b_ref, o_ref):
    c = c_ref[...]
    sc = c * _sigmoid(c)
    o_ref[0] = _dot(sc, w_ref[0], precision=HIGHEST) + b_ref[0]


def _modulation(cvec, ada_w, ada_b):
    depth, d, n = ada_w.shape
    r = cvec.shape[0]
    tn = d
    return pl.pallas_call(
        _mod_kernel,
        grid=(depth, n // tn),
        in_specs=[
            pl.BlockSpec((r, d), lambda l, j: (0, 0)),
            pl.BlockSpec((1, d, tn), lambda l, j: (l, 0, j)),
            pl.BlockSpec((1, 1, tn), lambda l, j: (l, 0, j)),
        ],
        out_specs=pl.BlockSpec((1, r, tn), lambda l, j: (l, 0, j)),
        out_shape=jax.ShapeDtypeStruct((depth, r, n), F32),
        compiler_params=_params(("parallel", "parallel")),
        name="adaln_modulation",
    )(cvec, ada_w, ada_b.reshape(depth, 1, n))


def _modulated_norm(x, g, shift, scale):
    ms = jnp.mean(x * x, axis=-1, keepdims=True)
    return (x * lax.rsqrt(ms + EPS) * g) * (1.0 + scale) + shift


def _headnorm_rope(p, gain, gmat, cos, sin, hd, post_scale):
    rows, width = p.shape
    lane = lax.broadcasted_iota(jnp.int32, (rows, LANES), 1)
    even = (lane % 2) == 0
    outs = []
    for cb in range(width // LANES):
        xb = p[:, cb * LANES:(cb + 1) * LANES]
        sq = xb * xb
        hi = sq.astype(BF16)
        lo = (sq - hi.astype(F32)).astype(BF16)
        ss = _dot(hi, gmat) + _dot(lo, gmat)
        y = xb * lax.rsqrt(ss * (1.0 / hd) + EPS) * gain
        y_next = pltpu.roll(y, LANES - 1, axis=1)
        y_prev = pltpu.roll(y, 1, axis=1)
        ysw = jnp.where(even, y_next, y_prev)
        outs.append((y * cos + ysw * sin) * post_scale)
    return jnp.concatenate(outs, axis=1) if len(outs) > 1 else outs[0]


def _inproj_kernel(x_ref, mod_ref, g_ref, w_ref, cos_ref, sin_ref, qg_ref, kg_ref, gmat_ref,
                   hq_ref, zf_ref, zb_ref, hi_ref, hog_ref, cb_ref, cu_ref, aq_ref, ak_ref, av_ref,
                   ga_ref, gb_ref, gc_ref, *, d, hgw, cw, aw, kvw, hd):
    x = x_ref[0]
    m = mod_ref[0]
    h = _modulated_norm(x, g_ref[...], m[:, 0:d], m[:, d:2 * d]).astype(BF16)

    def proj(lo, width):
        return _dot(h, w_ref[:, lo:lo + width])

    o = 0
    hq_ref[0] = proj(o, hgw).astype(BF16); o += hgw
    zf_ref[0] = proj(o, hgw); o += hgw
    zb_ref[0] = proj(o, hgw); o += hgw
    hi_ref[0] = proj(o, hgw).astype(BF16); o += hgw
    g = proj(o, hgw); o += hgw
    hog_ref[0] = (g * _sigmoid(g)).astype(BF16)
    cb_ref[0] = proj(o, cw).astype(BF16); o += cw
    cc = proj(o, cw); o += cw
    cx = proj(o, cw); o += cw
    cu_ref[0] = (cc * cx).astype(BF16)
    cos = cos_ref[...]
    sin = sin_ref[...]
    gmat = gmat_ref[...]
    q = proj(o, aw); o += aw
    aq_ref[0] = _headnorm_rope(q, qg_ref[...], gmat, cos, sin, hd, hd ** -0.5).astype(BF16)
    k = proj(o, kvw); o += kvw
    ak_ref[0] = _headnorm_rope(k, kg_ref[...], gmat, cos, sin, hd, 1.0).astype(BF16)
    av_ref[0] = proj(o, kvw).astype(BF16); o += kvw
    ga_ref[0] = _sigmoid(proj(o, d)).astype(BF16); o += d
    gb_ref[0] = _sigmoid(proj(o, d)).astype(BF16); o += d
    gc_ref[0] = _sigmoid(proj(o, d)).astype(BF16); o += d


def _in_projection(xc, mod_l, norm_g, w_bf, cos_t, sin_t, qg, kg, gmat, dims):
    b, t, d = xc.shape
    hgw, cw, aw, kvw, hd, nct = dims["hgw"], dims["cw"], dims["aw"], dims["kvw"], dims["hd"], dims["nct"]
    nt = t // TOK_TILE
    in_w = w_bf.shape[1]
    nb = mod_l.shape[0] - 1

    def tok(width):
        return pl.BlockSpec((1, TOK_TILE, width), lambda bi, i: (bi, i, 0))

    def const(shape):
        return pl.BlockSpec(shape, lambda bi, i: (0,) * len(shape))

    widths = [(hgw, BF16), (hgw, F32), (hgw, F32), (hgw, BF16), (hgw, BF16), (cw, BF16), (cw, BF16),
              (aw, BF16), (kvw, BF16), (kvw, BF16), (d, BF16), (d, BF16), (d, BF16)]
    return pl.pallas_call(
        functools.partial(_inproj_kernel, d=d, hgw=hgw, cw=cw, aw=aw, kvw=kvw, hd=hd),
        grid=(b, nt),
        in_specs=[
            tok(d),
            pl.BlockSpec((1, 1, N_MOD * d), lambda bi, i: (jnp.where(i < nct, nb, bi), 0, 0)),
            const((1, d)),
            pl.BlockSpec((d, in_w), lambda bi, i: (0, 0), pipeline_mode=pl.Buffered(1)),
            pl.BlockSpec((TOK_TILE, LANES), lambda bi, i: (i, 0)),
            pl.BlockSpec((TOK_TILE, LANES), lambda bi, i: (i, 0)),
            const((1, LANES)),
            const((1, LANES)),
            const((LANES, LANES)),
        ],
        out_specs=[tok(w) for w, _ in widths],
        out_shape=[jax.ShapeDtypeStruct((b, t, w), dt) for w, dt in widths],
        compiler_params=_params(("parallel", "parallel")),
        name="in_projection",
    )(xc, mod_l, norm_g, w_bf, cos_t, sin_t, qg, kg, gmat)


def _cumsum_rows(x, reverse):
    n = x.shape[0]
    row = lax.broadcasted_iota(jnp.int32, x.shape, 0)
    s = 1
    while s < n:
        if reverse:
            x = x + jnp.where(row < n - s, pltpu.roll(x, n - s, axis=0), 0.0)
        else:
            x = x + jnp.where(row >= s, pltpu.roll(x, s, axis=0), 0.0)
        s *= 2
    return x


def _hgrn_chunk(q, z, v, log_lb, log1m_lb, one_m_lb, st_ref, reverse):
    c = q.shape[0]
    sp = jnp.log1p(jnp.exp(-jnp.abs(z)))
    ls = jnp.minimum(z, 0.0) - sp
    kk = one_m_lb * jnp.exp(jnp.minimum(-z, 0.0) - sp)
    t = log1m_lb + ls
    mx = jnp.maximum(log_lb, t)
    mn = jnp.minimum(log_lb, t)
    lf = mx + jnp.log1p(jnp.exp(mn - mx))
    cs = _cumsum_rows(lf, reverse)
    r = c // 2 if reverse else c // 2 - 1
    c_mid = cs[r:r + 1, :]
    c_end = cs[0:1, :] if reverse else cs[c - 1:c, :]
    qt = q * jnp.exp(cs - c_mid)
    kt = kk * jnp.exp(c_mid - cs)
    a = _dot_nt(qt.astype(BF16), kt.astype(BF16))
    row = lax.broadcasted_iota(jnp.int32, (c, c), 0)
    col = lax.broadcasted_iota(jnp.int32, (c, c), 1)
    keep = (col >= row) if reverse else (col <= row)
    a = jnp.where(keep, a, 0.0)
    st = st_ref[...]
    qs = (qt * jnp.exp(c_mid)).astype(BF16)
    o = _dot(a.astype(BF16), v) + _dot_nt(qs, st.astype(BF16))
    kh = (kt * jnp.exp(c_end - c_mid)).astype(BF16)
    st_ref[...] = st * jnp.exp(c_end) + _dot_tn(v, kh)
    return o


def _hgrn_kernel(lbl_ref, hq_ref, zf_ref, zb_ref, hi_ref, hog_ref, ng_ref, a_ref,
                 of_ref, ob_ref, stf_ref, stb_ref, *, layer, n_chunks, n_ctx_chunks):
    lg = lbl_ref[...]
    e = jnp.exp(lg - jnp.max(lg, axis=0, keepdims=True))
    sm = e / jnp.sum(e, axis=0, keepdims=True)
    lb = jnp.zeros(lg.shape[1:], F32)
    for j in range(1, layer + 1):
        lb = lb + sm[j]
    consts = []
    for dirn in range(2):
        lbd = lb[dirn:dirn + 1, :]
        consts.append((jnp.log(lbd), jnp.log1p(-lbd), 1.0 - lbd))

    stf_ref[...] = jnp.zeros_like(stf_ref)
    stb_ref[...] = jnp.zeros_like(stb_ref)
    c = HG_CHUNK

    def step(s, carry):
        rf = pl.multiple_of(s * c, c)
        of_ref[pl.ds(rf, c), :] = _hgrn_chunk(
            hq_ref[0, pl.ds(rf, c), :].astype(F32), zf_ref[0, pl.ds(rf, c), :], hi_ref[0, pl.ds(rf, c), :],
            *consts[0], stf_ref, False)
        cbk = jnp.where(s < n_ctx_chunks, n_ctx_chunks - 1 - s, n_chunks - 1 - (s - n_ctx_chunks))
        rb = pl.multiple_of(cbk * c, c)
        ob_ref[pl.ds(rb, c), :] = _hgrn_chunk(
            hq_ref[0, pl.ds(rb, c), :].astype(F32), zb_ref[0, pl.ds(rb, c), :], hi_ref[0, pl.ds(rb, c), :],
            *consts[1], stb_ref, True)
        return carry

    lax.fori_loop(0, n_chunks, step, 0)

    def readout(i, carry):
        r0 = pl.multiple_of(i * TOK_TILE, TOK_TILE)
        o = of_ref[pl.ds(r0, TOK_TILE), :] + ob_ref[pl.ds(r0, TOK_TILE), :]
        ms = jnp.mean(o * o, axis=-1, keepdims=True)
        y = o * lax.rsqrt(ms + EPS) * ng_ref[...]
        a_ref[0, pl.ds(r0, TOK_TILE), :] = (y * hog_ref[0, pl.ds(r0, TOK_TILE), :].astype(F32)).astype(BF16)
        return carry

    lax.fori_loop(0, (n_chunks * c) // TOK_TILE, readout, 0)


def _hgrn(hq, zf, zb, hi, hog, lb_logits, norm_g, layer, ctx_len):
    b, t, hgw = hq.shape
    nh = hgw // HG_DK
    depth = lb_logits.shape[0]

    def seq():
        return pl.BlockSpec((1, t, HG_DK), lambda bi, h: (bi, 0, h))

    return pl.pallas_call(
        functools.partial(_hgrn_kernel, layer=layer, n_chunks=t // HG_CHUNK, n_ctx_chunks=ctx_len // HG_CHUNK),
        grid=(b, nh),
        in_specs=[
            pl.BlockSpec((depth, 2, HG_DK), lambda bi, h: (0, 0, h)),
            seq(), seq(), seq(), seq(), seq(),
            pl.BlockSpec((1, HG_DK), lambda bi, h: (0, h)),
        ],
        out_specs=seq(),
        out_shape=jax.ShapeDtypeStruct((b, t, hgw), BF16),
        scratch_shapes=[pltpu.VMEM((t, HG_DK), F32), pltpu.VMEM((t, HG_DK), F32),
                        pltpu.VMEM((HG_DK, HG_DK), F32), pltpu.VMEM((HG_DK, HG_DK), F32)],
        compiler_params=_params(("parallel", "parallel")),
        name="hgrn2_bidirectional",
    )(lb_logits, hq, zf, zb, hi, hog, norm_g)


def _attn_kernel(q_ref, k_ref, v_ref, o_ref, qs_ref, m_ref, l_ref, acc_ref, *,
                 n_groups, group, hd, nct, n_kv_ctx, n_kv_all):
    i = pl.program_id(1)
    tq = q_ref.shape[1]
    n_kv = jnp.where(i < nct, n_kv_ctx, n_kv_all)
    for g in range(n_groups):
        for h in range(group):
            c0 = (g * group + h) * hd
            qs_ref[h * tq:(h + 1) * tq, :] = q_ref[0, :, c0:c0 + hd]
        m_ref[...] = jnp.full_like(m_ref, -jnp.inf)
        l_ref[...] = jnp.zeros_like(l_ref)
        acc_ref[...] = jnp.zeros_like(acc_ref)

        def body(j, carry, g=g):
            r0 = pl.multiple_of(j * KV_TILE, KV_TILE)
            kt = k_ref[0, pl.ds(r0, KV_TILE), g * hd:(g + 1) * hd]
            vt = v_ref[0, pl.ds(r0, KV_TILE), g * hd:(g + 1) * hd]
            s = _dot_nt(qs_ref[...], kt)
            m_prev = m_ref[...]
            m_new = jnp.maximum(m_prev, jnp.max(s, axis=1, keepdims=True))
            alpha = jnp.exp(m_prev - m_new)
            p = jnp.exp(s - m_new)
            l_ref[...] = alpha * l_ref[...] + jnp.sum(p, axis=1, keepdims=True)
            acc_ref[...] = alpha * acc_ref[...] + _dot(p.astype(BF16), vt)
            m_ref[...] = m_new
            return carry

        lax.fori_loop(0, n_kv, body, 0)
        out = acc_ref[...] / l_ref[...]
        for h in range(group):
            c0 = (g * group + h) * hd
            o_ref[0, :, c0:c0 + hd] = out[h * tq:(h + 1) * tq, :].astype(BF16)


def _attention(aq, ak, av, hd, ctx_len):
    b, t, aw = aq.shape
    kvw = ak.shape[2]
    n_groups = kvw // hd
    group = aw // kvw
    nt = t // TOK_TILE
    rows = group * TOK_TILE
    return pl.pallas_call(
        functools.partial(_attn_kernel, n_groups=n_groups, group=group, hd=hd, nct=ctx_len // TOK_TILE,
                          n_kv_ctx=ctx_len // KV_TILE, n_kv_all=t // KV_TILE),
        grid=(b, nt),
        in_specs=[
            pl.BlockSpec((1, TOK_TILE, aw), lambda bi, i: (bi, i, 0)),
            pl.BlockSpec((1, t, kvw), lambda bi, i: (bi, 0, 0)),
            pl.BlockSpec((1, t, kvw), lambda bi, i: (bi, 0, 0)),
        ],
        out_specs=pl.BlockSpec((1, TOK_TILE, aw), lambda bi, i: (bi, i, 0)),
        out_shape=jax.ShapeDtypeStruct((b, t, aw), BF16),
        scratch_shapes=[pltpu.VMEM((rows, hd), BF16), pltpu.VMEM((rows, 1), F32), pltpu.VMEM((rows, 1), F32),
                        pltpu.VMEM((rows, hd), F32)],
        compiler_params=_params(("parallel", "parallel")),
        name="gqa_attention",
    )(aq, ak, av)


def _merge_kernel(x_ref, mod_ref, a_ref, cb_ref, cu_ref, cup_ref, cun_ref, att_ref, ga_ref, gb_ref, gc_ref,
                  cw_ref, wa_ref, wb_ref, wc_ref, wo_ref, g2_ref, rwt_ref,
                  x1_ref, h2_ref, aff_ref, *, d, nct, nt):
    i = pl.program_id(1)
    m = mod_ref[0]
    rows = x_ref.shape[1]
    u = cu_ref[0].astype(F32)
    row = lax.broadcasted_iota(jnp.int32, u.shape, 0)
    has_prev = jnp.logical_and(i != 0, i != nct)
    has_next = jnp.logical_and(i != nct - 1, i != nt - 1)
    prev_row = jnp.where(has_prev, cup_ref[0, HALO - 1:HALO, :].astype(F32), 0.0)
    next_row = jnp.where(has_next, cun_ref[0, 0:1, :].astype(F32), 0.0)
    u_prev = jnp.where(row == 0, prev_row, pltpu.roll(u, 1, axis=0))
    u_next = jnp.where(row == rows - 1, next_row, pltpu.roll(u, rows - 1, axis=0))
    cw = cw_ref[...]
    conv = cw[0:1, :] * u_prev + cw[1:2, :] * u + cw[2:3, :] * u_next
    bb = (cb_ref[0].astype(F32) * conv).astype(BF16)
    y = (ga_ref[0].astype(F32) * _dot(a_ref[0], wa_ref[...])
         + gb_ref[0].astype(F32) * _dot(bb, wb_ref[...])
         + gc_ref[0].astype(F32) * _dot(att_ref[0], wc_ref[...]))
    x1 = x_ref[0] + m[:, 2 * d:3 * d] * _dot(y.astype(BF16), wo_ref[...])
    x1_ref[0] = x1
    h2 = _modulated_norm(x1, g2_ref[...], m[:, 3 * d:4 * d], m[:, 4 * d:5 * d])
    h2_ref[0] = h2.astype(BF16)
    logits = _dot_nt(rwt_ref[...], h2, precision=HIGHEST)
    ex = jnp.exp(logits - jnp.max(logits, axis=0, keepdims=True))
    aff_ref[0] = ex / jnp.sum(ex, axis=0, keepdims=True)


def _merge(xc, mod_l, a, cb, cu, att, ga, gb, gc, conv_w, wa, wb, wc, wo, norm2_g, rwt, nct):
    b, t, d = xc.shape
    nt = t // TOK_TILE
    ne = rwt.shape[0]
    nb = mod_l.shape[0] - 1
    sub = TOK_TILE // HALO
    n8 = t // HALO

    def tok(width):
        return pl.BlockSpec((1, TOK_TILE, width), lambda bi, i: (bi, i, 0))

    def const(shape):
        return pl.BlockSpec(shape, lambda bi, i: (0,) * len(shape))

    cwid = cu.shape[2]
    return pl.pallas_call(
        functools.partial(_merge_kernel, d=d, nct=nct, nt=nt),
        grid=(b, nt),
        in_specs=[
            tok(d),
            pl.BlockSpec((1, 1, N_MOD * d), lambda bi, i: (jnp.where(i < nct, nb, bi), 0, 0)),
            tok(a.shape[2]), tok(cwid), tok(cwid),
            pl.BlockSpec((1, HALO, cwid), lambda bi, i: (bi, jnp.maximum(i * sub - 1, 0), 0)),
            pl.BlockSpec((1, HALO, cwid), lambda bi, i: (bi, jnp.minimum((i + 1) * sub, n8 - 1), 0)),
            tok(att.shape[2]), tok(d), tok(d), tok(d),
            const(conv_w.shape), const(wa.shape), const(wb.shape), const(wc.shape), const(wo.shape),
            const((1, d)), const(rwt.shape),
        ],
        out_specs=[tok(d), tok(d), pl.BlockSpec((1, ne, TOK_TILE), lambda bi, i: (bi, 0, i))],
        out_shape=[jax.ShapeDtypeStruct((b, t, d), F32), jax.ShapeDtypeStruct((b, t, d), BF16),
                   jax.ShapeDtypeStruct((b, ne, t), F32)],
        compiler_params=_params(("parallel", "parallel")),
        name="merge_residual_router",
    )(xc, mod_l, a, cb, cu, cu, cu, att, ga, gb, gc, conv_w, wa, wb, wc, wo, norm2_g, rwt)


def _prefix_count(mask, tri):
    e, n = mask.shape
    carry = jnp.zeros((e, 1), F32)
    outs = []
    for blk in range(n // LANES):
        xb = jnp.where(mask[:, blk * LANES:(blk + 1) * LANES], 1.0, 0.0).astype(BF16)
        pre = _dot(xb, tri) + carry
        carry = pre[:, LANES - 1:LANES]
        outs.append(pre)
    return jnp.concatenate(outs, axis=1) if len(outs) > 1 else outs[0]


def _topk_kernel(aff_ref, tri_ref, pos_ref, *, off, n, cap):
    a = aff_ref[0][:, off:off + n]
    bits = pltpu.bitcast(a, jnp.int32)
    thr = jnp.zeros((a.shape[0], 1), jnp.int32)
    for bit in range(30, -1, -1):
        cand = thr | (1 << bit)
        cnt = jnp.sum(jnp.where(bits >= cand, 1.0, 0.0), axis=1, keepdims=True)
        thr = jnp.where(cnt >= cap, cand, thr)
    gt = bits > thr
    eq = bits == thr
    need = cap - jnp.sum(jnp.where(gt, 1.0, 0.0), axis=1, keepdims=True)
    tri = tri_ref[...]
    eq_rank = _prefix_count(eq, tri)
    sel = jnp.logical_or(gt, jnp.logical_and(eq, eq_rank <= need))
    slot = _prefix_count(sel, tri) - 1.0
    pos_ref[0] = jnp.where(sel, slot, -1.0).astype(jnp.int32)


def _topk_positions(aff_t, tri, off, n, cap):
    b, ne, t = aff_t.shape
    return pl.pallas_call(
        functools.partial(_topk_kernel, off=off, n=n, cap=cap),
        grid=(b,),
        in_specs=[pl.BlockSpec((1, ne, t), lambda bi: (bi, 0, 0)), pl.BlockSpec((LANES, LANES), lambda bi: (0, 0))],
        out_specs=pl.BlockSpec((1, ne, n), lambda bi: (bi, 0, 0)),
        out_shape=jax.ShapeDtypeStruct((b, ne, n), jnp.int32),
        compiler_params=_params(("parallel",)),
        name="expert_choice_topk",
    )(aff_t, tri)


def _gather_kernel(h_ref, pos_ref, xs_ref, *, off, n, cap, kt):
    pos = pos_ref[0, 0]
    acc = jnp.zeros(xs_ref.shape[1:], F32)
    for c0 in range(0, n, kt):
        slot = lax.broadcasted_iota(jnp.int32, (cap, kt), 0)
        onehot = jnp.where(pos[:, c0:c0 + kt] == slot, 1.0, 0.0).astype(BF16)
        acc = acc + _dot(onehot, h_ref[0, off + c0:off + c0 + kt, :])
    xs_ref[0] = acc.astype(BF16)


def _gather_tokens(h2, pos, off, n, cap):
    b, t, d = h2.shape
    ne = pos.shape[1]
    kt = min(n, 512)
    return pl.pallas_call(
        functools.partial(_gather_kernel, off=off, n=n, cap=cap, kt=kt),
        grid=(b, ne),
        in_specs=[pl.BlockSpec((1, t, d), lambda bi, e: (bi, 0, 0)),
                  pl.BlockSpec((1, 1, 1, n), lambda bi, e: (bi, e, 0, 0))],
        out_specs=pl.BlockSpec((1, cap, d), lambda bi, e: (e, bi, 0)),
        out_shape=jax.ShapeDtypeStruct((ne, b * cap, d), BF16),
        compiler_params=_params(("parallel", "parallel")),
        name="expert_gather",
    )(h2, pos.reshape(b, ne, 1, n))


def _ffn_kernel(xs_ref, wg_ref, wu_ref, wd_ref, ys_ref):
    x = xs_ref[0]
    g = _dot(x, wg_ref[0])
    hid = (g * _sigmoid(g)) * _dot(x, wu_ref[0])
    ys_ref[0] = _dot(hid.astype(BF16), wd_ref[0]).astype(BF16)


def _expert_ffn(xs, wg, wu, wd):
    ne, m, d = xs.shape
    ff = wg.shape[2]
    tm = min(m, 512)
    return pl.pallas_call(
        _ffn_kernel,
        grid=(ne, m // tm),
        in_specs=[pl.BlockSpec((1, tm, d), lambda e, j: (e, j, 0)),
                  pl.BlockSpec((1, d, ff), lambda e, j: (e, 0, 0)),
                  pl.BlockSpec((1, d, ff), lambda e, j: (e, 0, 0)),
                  pl.BlockSpec((1, ff, d), lambda e, j: (e, 0, 0))],
        out_specs=pl.BlockSpec((1, tm, d), lambda e, j: (e, j, 0)),
        out_shape=jax.ShapeDtypeStruct((ne, m, d), BF16),
        compiler_params=_params(("parallel", "parallel")),
        name="expert_ffn",
    )(xs, wg, wu, wd)


def _scatter_kernel(x_ref, mod_ref, ys_ref, pos_ref, aff_ref, fg_ref, o_ref, *, d, ne, cap, final):
    m = mod_ref[0]
    pos = pos_ref[0]
    aff = aff_ref[0]
    rows = pos.shape[0]
    slot = lax.broadcasted_iota(jnp.int32, (rows, cap), 1)
    acc = jnp.zeros((rows, d), F32)
    for e in range(ne):
        onehot = jnp.where(pos[:, e:e + 1] == slot, 1.0, 0.0).astype(BF16)
        acc = acc + aff[:, e:e + 1] * _dot(onehot, ys_ref[e])
    x2 = x_ref[0] + m[:, 5 * d:6 * d] * acc
    if final:
        ms = jnp.mean(x2 * x2, axis=-1, keepdims=True)
        x2 = x2 * lax.rsqrt(ms + EPS) * fg_ref[...]
    o_ref[0] = x2


def _scatter_residual(x1, mod_l, ys, pos_n, aff_n, final_g, off, n, cap, mod_row_ctx, final):
    b, t, d = x1.shape
    ne = ys.shape[0]
    nb = mod_l.shape[0] - 1
    ot = off // TOK_TILE
    if final:
        out_shape = jax.ShapeDtypeStruct((b, n, d), F32)
        out_spec = pl.BlockSpec((1, TOK_TILE, d), lambda bi, i: (bi, i, 0))
        aliases = {}
    else:
        out_shape = jax.ShapeDtypeStruct((b, t, d), F32)
        out_spec = pl.BlockSpec((1, TOK_TILE, d), lambda bi, i: (bi, i + ot, 0))
        aliases = {0: 0}
    return pl.pallas_call(
        functools.partial(_scatter_kernel, d=d, ne=ne, cap=cap, final=final),
        grid=(b, n // TOK_TILE),
        in_specs=[
            pl.BlockSpec((1, TOK_TILE, d), lambda bi, i: (bi, i + ot, 0)),
            pl.BlockSpec((1, 1, N_MOD * d), lambda bi, i: (nb if mod_row_ctx else bi, 0, 0)),
            pl.BlockSpec((ne, cap, d), lambda bi, i: (0, bi, 0)),
            pl.BlockSpec((1, TOK_TILE, ne), lambda bi, i: (bi, i, 0)),
            pl.BlockSpec((1, TOK_TILE, ne), lambda bi, i: (bi, i + ot, 0)),
            pl.BlockSpec((1, d), lambda bi, i: (0, 0)),
        ],
        out_specs=out_spec,
        out_shape=out_shape,
        input_output_aliases=aliases,
        compiler_params=_params(("parallel", "parallel")),
        name="expert_scatter_residual",
    )(x1, mod_l, ys, pos_n, aff_n, final_g)


def _rope_tables(ctx_len, seq, hd):
    rows = seq // GRID_W
    row = jnp.repeat(jnp.arange(rows), GRID_W).astype(F32)
    col = jnp.tile(jnp.arange(GRID_W), rows).astype(F32)
    inv = ROPE_THETA ** (-jnp.arange(0, hd // 2, 2, dtype=F32) / (hd // 2))
    ang = jnp.concatenate([row[:, None] * inv, col[:, None] * inv], axis=-1)
    cos = jnp.repeat(jnp.cos(ang), 2, axis=-1)
    sin = jnp.repeat(jnp.sin(ang), 2, axis=-1) * jnp.tile(jnp.array([-1.0, 1.0], F32), hd // 2)
    cos = jnp.concatenate([jnp.ones((ctx_len, hd), F32), cos], axis=0)
    sin = jnp.concatenate([jnp.zeros((ctx_len, hd), F32), sin], axis=0)
    rep = LANES // hd
    return jnp.tile(cos, (1, rep)), jnp.tile(sin, (1, rep))


def kernel(x, c, ctx, c_ctx, ada_w, ada_b, norm1_g, norm2_g, w_in, hg_lb_logits, hg_norm_g, conv_w, q_norm_g, k_norm_g, w_proj_a, w_proj_b, w_proj_c, w_out, router_w, w_gate, w_up, w_down, final_norm_g):
    b, s, d = x.shape
    ctx_len = ctx.shape[1]
    t = ctx_len + s
    depth = w_in.shape[0]
    hgw = hg_norm_g.shape[1]
    cw = conv_w.shape[2]
    aw = w_proj_c.shape[1]
    hd = q_norm_g.shape[1]
    kvw = (w_in.shape[2] - 5 * hgw - 3 * cw - aw - 3 * d) // 2
    ne = router_w.shape[2]
    nct = ctx_len // TOK_TILE
    assert ctx_len % TOK_TILE == 0 and s % TOK_TILE == 0 and LANES % hd == 0 and cw == d
    dims = dict(hgw=hgw, cw=cw, aw=aw, kvw=kvw, hd=hd, nct=nct)

    xc = jnp.concatenate([ctx, x], axis=1)
    n_rows = -(-(b + 1) // 8) * 8
    cvec = jnp.concatenate([c, c_ctx[None, :], jnp.zeros((n_rows - b - 1, d), F32)], axis=0)
    mod = _modulation(cvec, ada_w, ada_b)
    mod = mod[:, :b + 1].reshape(depth, b + 1, 1, N_MOD * d)

    cos_t, sin_t = _rope_tables(ctx_len, s, hd)
    lane = jnp.arange(LANES)
    gmat = (lane[:, None] // hd == lane[None, :] // hd).astype(BF16)
    tri = (lane[:, None] <= lane[None, :]).astype(BF16)
    rep = LANES // hd
    lb_logits = hg_lb_logits.astype(F32)

    out = None
    for l in range(depth):
        last = l == depth - 1
        mod_l = mod[l]
        (hq, zf, zb, hi, hog, cb, cu, aq, ak, av, ga, gb, gc) = _in_projection(
            xc, mod_l, norm1_g[l][None, :], w_in[l].astype(BF16), cos_t, sin_t,
            jnp.tile(q_norm_g[l], rep)[None, :], jnp.tile(k_norm_g[l], rep)[None, :], gmat, dims)
        a = _hgrn(hq, zf, zb, hi, hog, lb_logits, hg_norm_g[l][None, :], l, ctx_len)
        att = _attention(aq, ak, av, hd, ctx_len)
        x1, h2, aff_t = _merge(xc, mod_l, a, cb, cu, att, ga, gb, gc, conv_w[l],
                               w_proj_a[l].astype(BF16), w_proj_b[l].astype(BF16), w_proj_c[l].astype(BF16),
                               w_out[l].astype(BF16), norm2_g[l][None, :], router_w[l].T, nct)
        aff_n = jnp.swapaxes(aff_t, 1, 2)
        wg, wu, wd = w_gate[l].astype(BF16), w_up[l].astype(BF16), w_down[l].astype(BF16)

        def moe(stream, off, n, mod_row_ctx, final):
            cap = CAPACITY_FACTOR * n // ne
            pos = _topk_positions(aff_t, tri, off, n, cap)
            xs = _gather_tokens(h2, pos, off, n, cap)
            ys = _expert_ffn(xs, wg, wu, wd)
            return _scatter_residual(stream, mod_l, ys, jnp.swapaxes(pos, 1, 2), aff_n,
                                     final_norm_g[None, :], off, n, cap, mod_row_ctx, final)

        if last:
            out = moe(x1, ctx_len, s, False, True)
        else:
            xc = moe(x1, ctx_len, s, False, False)
            xc = moe(xc, 0, ctx_len, True, False)
    return out
```

```python
import functools

import jax
import jax.numpy as jnp
from jax import lax
from jax.experimental import pallas as pl
from jax.experimental.pallas import tpu as pltpu

F32 = jnp.float32
BF16 = jnp.bfloat16

EPS = 1e-6
N_MOD = 6
HG_DK = 128
GRID_W = 64
ROPE_THETA = 10000.0
CAPACITY_FACTOR = 2

LANES = 128
TOK_TILE = 256
HG_CHUNK = 64
KV_TILE = 256
HALO = 16
VMEM_LIMIT = 56 * 1024 * 1024

HIGHEST = lax.Precision.HIGHEST


def _dot(a, b, precision=None):
    return jnp.dot(a, b, preferred_element_type=F32, precision=precision)


def _dot_nt(a, b, precision=None):
    return lax.dot_general(a, b, (((1,), (1,)), ((), ())), preferred_element_type=F32, precision=precision)


def _dot_tn(a, b):
    return lax.dot_general(a, b, (((0,), (0,)), ((), ())), preferred_element_type=F32)


def _sigmoid(x):
    return 1.0 / (1.0 + jnp.exp(-x))


def _params(sem, vmem=VMEM_LIMIT):
    return pltpu.CompilerParams(dimension_semantics=sem, vmem_limit_bytes=vmem)


def _mod_kernel(c_ref, w_ref, b_ref, o_ref):
    c = c_ref[...]
    sc = c * _sigmoid(c)
    o_ref[0] = _dot(sc, w_ref[0], precision=HIGHEST) + b_ref[0]


def _modulation(cvec, ada_w, ada_b):
    depth, d, n = ada_w.shape
    r = cvec.shape[0]
    tn = d
    return pl.pallas_call(
        _mod_kernel,
        grid=(depth, n // tn),
        in_specs=[
            pl.BlockSpec((r, d), lambda l, j: (0, 0)),
            pl.BlockSpec((1, d, tn), lambda l, j: (l, 0, j)),
            pl.BlockSpec((1, 1, tn), lambda l, j: (l, 0, j)),
        ],
        out_specs=pl.BlockSpec((1, r, tn), lambda l, j: (l, 0, j)),
        out_shape=jax.ShapeDtypeStruct((depth, r, n), F32),
        compiler_params=_params(("parallel", "parallel")),
        name="adaln_modulation",
    )(cvec, ada_w, ada_b.reshape(depth, 1, n))


def _modulated_norm(x, g, shift, scale):
    ms = jnp.mean(x * x, axis=-1, keepdims=True)
    return (x * lax.rsqrt(ms + EPS) * g) * (1.0 + scale) + shift


def _headnorm_rope(p, gain, gmat, cos, sin, hd, post_scale):
    rows, width = p.shape
    lane = lax.broadcasted_iota(jnp.int32, (rows, LANES), 1)
    even = (lane % 2) == 0
    outs = []
    for cb in range(width // LANES):
        xb = p[:, cb * LANES:(cb + 1) * LANES]
        sq = xb * xb
        hi = sq.astype(BF16)
        lo = (sq - hi.astype(F32)).astype(BF16)
        ss = _dot(hi, gmat) + _dot(lo, gmat)
        y = xb * lax.rsqrt(ss * (1.0 / hd) + EPS) * gain
        y_next = pltpu.roll(y, LANES - 1, axis=1)
        y_prev = pltpu.roll(y, 1, axis=1)
        ysw = jnp.where(even, y_next, y_prev)
        outs.append((y * cos + ysw * sin) * post_scale)
    return jnp.concatenate(outs, axis=1) if len(outs) > 1 else outs[0]


def _inproj_kernel(x_ref, mod_ref, g_ref, w_ref, cos_ref, sin_ref, qg_ref, kg_ref, gmat_ref,
                   hq_ref, zf_ref, zb_ref, hi_ref, hog_ref, cb_ref, cu_ref, aq_ref, ak_ref, av_ref,
                   ga_ref, gb_ref, gc_ref, *, d, hgw, cw, aw, kvw, hd):
    x = x_ref[0]
    m = mod_ref[0]
    h = _modulated_norm(x, g_ref[...], m[:, 0:d], m[:, d:2 * d]).astype(BF16)

    def proj(lo, width):
        return _dot(h, w_ref[:, lo:lo + width])

    o = 0
    hq_ref[0] = proj(o, hgw).astype(BF16); o += hgw
    zf_ref[0] = proj(o, hgw); o += hgw
    zb_ref[0] = proj(o, hgw); o += hgw
    hi_ref[0] = proj(o, hgw).astype(BF16); o += hgw
    g = proj(o, hgw); o += hgw
    hog_ref[0] = (g * _sigmoid(g)).astype(BF16)
    cb_ref[0] = proj(o, cw).astype(BF16); o += cw
    cc = proj(o, cw); o += cw
    cx = proj(o, cw); o += cw
    cu_ref[0] = (cc * cx).astype(BF16)
    cos = cos_ref[...]
    sin = sin_ref[...]
    gmat = gmat_ref[...]
    q = proj(o, aw); o += aw
    aq_ref[0] = _headnorm_rope(q, qg_ref[...], gmat, cos, sin, hd, hd ** -0.5).astype(BF16)
    k = proj(o, kvw); o += kvw
    kn = _headnorm_rope(k, kg_ref[...], gmat, cos, sin, hd, 1.0).astype(BF16)
    vv = proj(o, kvw).astype(BF16); o += kvw
    for g in range(kvw // hd):
        ak_ref[0, g] = kn[:, g * hd:(g + 1) * hd]
        av_ref[0, g] = vv[:, g * hd:(g + 1) * hd]
    ga_ref[0] = _sigmoid(proj(o, d)).astype(BF16); o += d
    gb_ref[0] = _sigmoid(proj(o, d)).astype(BF16); o += d
    gc_ref[0] = _sigmoid(proj(o, d)).astype(BF16); o += d


def _in_projection(xc, mod_l, norm_g, w_bf, cos_t, sin_t, qg, kg, gmat, dims):
    b, t, d = xc.shape
    hgw, cw, aw, kvw, hd, nct = dims["hgw"], dims["cw"], dims["aw"], dims["kvw"], dims["hd"], dims["nct"]
    nt = t // TOK_TILE
    in_w = w_bf.shape[1]
    nb = mod_l.shape[0] - 1

    def tok(width):
        return pl.BlockSpec((1, TOK_TILE, width), lambda bi, i: (bi, i, 0))

    def const(shape):
        return pl.BlockSpec(shape, lambda bi, i: (0,) * len(shape))

    widths = [(hgw, BF16), (hgw, F32), (hgw, F32), (hgw, BF16), (hgw, BF16), (cw, BF16), (cw, BF16),
              (aw, BF16), (kvw, BF16), (kvw, BF16), (d, BF16), (d, BF16), (d, BF16)]
    ng = kvw // hd
    out_specs = [tok(w) for w, _ in widths]
    out_shape = [jax.ShapeDtypeStruct((b, t, w), dt) for w, dt in widths]
    for idx in (8, 9):
        out_specs[idx] = pl.BlockSpec((1, ng, TOK_TILE, hd), lambda bi, i: (bi, 0, i, 0))
        out_shape[idx] = jax.ShapeDtypeStruct((b, ng, t, hd), BF16)
    return pl.pallas_call(
        functools.partial(_inproj_kernel, d=d, hgw=hgw, cw=cw, aw=aw, kvw=kvw, hd=hd),
        grid=(b, nt),
        in_specs=[
            tok(d),
            pl.BlockSpec((1, 1, N_MOD * d), lambda bi, i: (jnp.where(i < nct, nb, bi), 0, 0)),
            const((1, d)),
            pl.BlockSpec((d, in_w), lambda bi, i: (0, 0), pipeline_mode=pl.Buffered(1)),
            pl.BlockSpec((TOK_TILE, LANES), lambda bi, i: (i, 0)),
            pl.BlockSpec((TOK_TILE, LANES), lambda bi, i: (i, 0)),
            const((1, LANES)),
            const((1, LANES)),
            const((LANES, LANES)),
        ],
        out_specs=out_specs,
        out_shape=out_shape,
        compiler_params=_params(("parallel", "parallel")),
        name="in_projection",
    )(xc, mod_l, norm_g, w_bf, cos_t, sin_t, qg, kg, gmat)


def _cumsum_rows(x, reverse):
    n = x.shape[0]
    row = lax.broadcasted_iota(jnp.int32, x.shape, 0)
    s = 1
    while s < n:
        if reverse:
            x = x + jnp.where(row < n - s, pltpu.roll(x, n - s, axis=0), 0.0)
        else:
            x = x + jnp.where(row >= s, pltpu.roll(x, s, axis=0), 0.0)
        s *= 2
    return x


def _hgrn_chunk(q, z, v, log_lb, log1m_lb, one_m_lb, st_ref, reverse):
    c = q.shape[0]
    sp = jnp.log1p(jnp.exp(-jnp.abs(z)))
    ls = jnp.minimum(z, 0.0) - sp
    kk = one_m_lb * jnp.exp(jnp.minimum(-z, 0.0) - sp)
    t = log1m_lb + ls
    mx = jnp.maximum(log_lb, t)
    mn = jnp.minimum(log_lb, t)
    lf = mx + jnp.log1p(jnp.exp(mn - mx))
    cs = _cumsum_rows(lf, reverse)
    r = c // 2 if reverse else c // 2 - 1
    c_mid = cs[r:r + 1, :]
    c_end = cs[0:1, :] if reverse else cs[c - 1:c, :]
    qt = q * jnp.exp(cs - c_mid)
    kt = kk * jnp.exp(c_mid - cs)
    a = _dot_nt(qt.astype(BF16), kt.astype(BF16))
    row = lax.broadcasted_iota(jnp.int32, (c, c), 0)
    col = lax.broadcasted_iota(jnp.int32, (c, c), 1)
    keep = (col >= row) if reverse else (col <= row)
    a = jnp.where(keep, a, 0.0)
    st = st_ref[...]
    qs = (qt * jnp.exp(c_mid)).astype(BF16)
    o = _dot(a.astype(BF16), v) + _dot_nt(qs, st.astype(BF16))
    kh = (kt * jnp.exp(c_end - c_mid)).astype(BF16)
    st_ref[...] = st * jnp.exp(c_end) + _dot_tn(v, kh)
    return o


def _hgrn_kernel(lbl_ref, hq_ref, zf_ref, zb_ref, hi_ref, hog_ref, ng_ref, a_ref,
                 of_ref, ob_ref, stf_ref, stb_ref, *, layer, n_chunks, n_ctx_chunks):
    lg = lbl_ref[...]
    e = jnp.exp(lg - jnp.max(lg, axis=0, keepdims=True))
    sm = e / jnp.sum(e, axis=0, keepdims=True)
    lb = jnp.zeros(lg.shape[1:], F32)
    for j in range(1, layer + 1):
        lb = lb + sm[j]
    consts = []
    for dirn in range(2):
        lbd = lb[dirn:dirn + 1, :]
        consts.append((jnp.log(lbd), jnp.log1p(-lbd), 1.0 - lbd))

    stf_ref[...] = jnp.zeros_like(stf_ref)
    stb_ref[...] = jnp.zeros_like(stb_ref)
    c = HG_CHUNK

    def step(s, carry):
        rf = pl.multiple_of(s * c, c)
        of_ref[pl.ds(rf, c), :] = _hgrn_chunk(
            hq_ref[0, pl.ds(rf, c), :].astype(F32), zf_ref[0, pl.ds(rf, c), :], hi_ref[0, pl.ds(rf, c), :],
            *consts[0], stf_ref, False)
        cbk = jnp.where(s < n_ctx_chunks, n_ctx_chunks - 1 - s, n_chunks - 1 - (s - n_ctx_chunks))
        rb = pl.multiple_of(cbk * c, c)
        ob_ref[pl.ds(rb, c), :] = _hgrn_chunk(
            hq_ref[0, pl.ds(rb, c), :].astype(F32), zb_ref[0, pl.ds(rb, c), :], hi_ref[0, pl.ds(rb, c), :],
            *consts[1], stb_ref, True)
        return carry

    lax.fori_loop(0, n_chunks, step, 0)

    def readout(i, carry):
        r0 = pl.multiple_of(i * TOK_TILE, TOK_TILE)
        o = of_ref[pl.ds(r0, TOK_TILE), :] + ob_ref[pl.ds(r0, TOK_TILE), :]
        ms = jnp.mean(o * o, axis=-1, keepdims=True)
        y = o * lax.rsqrt(ms + EPS) * ng_ref[...]
        a_ref[0, pl.ds(r0, TOK_TILE), :] = (y * hog_ref[0, pl.ds(r0, TOK_TILE), :].astype(F32)).astype(BF16)
        return carry

    lax.fori_loop(0, (n_chunks * c) // TOK_TILE, readout, 0)


def _hgrn(hq, zf, zb, hi, hog, lb_logits, norm_g, layer, ctx_len):
    b, t, hgw = hq.shape
    nh = hgw // HG_DK
    depth = lb_logits.shape[0]

    def seq():
        return pl.BlockSpec((1, t, HG_DK), lambda bi, h: (bi, 0, h))

    return pl.pallas_call(
        functools.partial(_hgrn_kernel, layer=layer, n_chunks=t // HG_CHUNK, n_ctx_chunks=ctx_len // HG_CHUNK),
        grid=(b, nh),
        in_specs=[
            pl.BlockSpec((depth, 2, HG_DK), lambda bi, h: (0, 0, h)),
            seq(), seq(), seq(), seq(), seq(),
            pl.BlockSpec((1, HG_DK), lambda bi, h: (0, h)),
        ],
        out_specs=seq(),
        out_shape=jax.ShapeDtypeStruct((b, t, hgw), BF16),
        scratch_shapes=[pltpu.VMEM((t, HG_DK), F32), pltpu.VMEM((t, HG_DK), F32),
                        pltpu.VMEM((HG_DK, HG_DK), F32), pltpu.VMEM((HG_DK, HG_DK), F32)],
        compiler_params=_params(("parallel", "parallel")),
        name="hgrn2_bidirectional",
    )(lb_logits, hq, zf, zb, hi, hog, norm_g)


def _attn_kernel(q_ref, k_ref, v_ref, o_ref, *, group, hd, nct, ctx_len, t_all):
    i = pl.program_id(2)

    def run(n_keys):
        k = k_ref[0, 0, 0:n_keys, :]
        v = v_ref[0, 0, 0:n_keys, :]
        outs = []
        for h in range(group):
            s = _dot_nt(q_ref[0, :, h * hd:(h + 1) * hd], k)
            p = jnp.exp(s - jnp.max(s, axis=1, keepdims=True))
            l = jnp.sum(p, axis=1, keepdims=True)
            outs.append((_dot(p.astype(BF16), v) / l).astype(BF16))
        o_ref[0] = jnp.concatenate(outs, axis=1)

    @pl.when(i < nct)
    def _():
        run(ctx_len)

    @pl.when(i >= nct)
    def _():
        run(t_all)


def _attention(aq, ak, av, ctx_len):
    b, t, aw = aq.shape
    n_groups, hd = ak.shape[1], ak.shape[3]
    group = aw // (n_groups * hd)
    nt = t // TOK_TILE
    return pl.pallas_call(
        functools.partial(_attn_kernel, group=group, hd=hd, nct=ctx_len // TOK_TILE, ctx_len=ctx_len, t_all=t),
        grid=(b, n_groups, nt),
        in_specs=[
            pl.BlockSpec((1, TOK_TILE, group * hd), lambda bi, g, i: (bi, i, g)),
            pl.BlockSpec((1, 1, t, hd), lambda bi, g, i: (bi, g, 0, 0)),
            pl.BlockSpec((1, 1, t, hd), lambda bi, g, i: (bi, g, 0, 0)),
        ],
        out_specs=pl.BlockSpec((1, TOK_TILE, group * hd), lambda bi, g, i: (bi, i, g)),
        out_shape=jax.ShapeDtypeStruct((b, t, aw), BF16),
        compiler_params=_params(("parallel", "parallel", "parallel")),
        name="gqa_attention",
    )(aq, ak, av)


def _merge_kernel(x_ref, mod_ref, a_ref, cb_ref, cu_ref, cup_ref, cun_ref, att_ref, ga_ref, gb_ref, gc_ref,
                  cw_ref, wa_ref, wb_ref, wc_ref, wo_ref, g2_ref, rwt_ref,
                  x1_ref, h2_ref, aff_ref, *, d, nct, nt):
    i = pl.program_id(1)
    m = mod_ref[0]
    rows = x_ref.shape[1]
    u = cu_ref[0].astype(F32)
    row = lax.broadcasted_iota(jnp.int32, u.shape, 0)
    has_prev = jnp.logical_and(i != 0, i != nct)
    has_next = jnp.logical_and(i != nct - 1, i != nt - 1)
    prev_row = jnp.where(has_prev, cup_ref[0, HALO - 1:HALO, :].astype(F32), 0.0)
    next_row = jnp.where(has_next, cun_ref[0, 0:1, :].astype(F32), 0.0)
    u_prev = jnp.where(row == 0, prev_row, pltpu.roll(u, 1, axis=0))
    u_next = jnp.where(row == rows - 1, next_row, pltpu.roll(u, rows - 1, axis=0))
    cw = cw_ref[...]
    conv = cw[0:1, :] * u_prev + cw[1:2, :] * u + cw[2:3, :] * u_next
    bb = (cb_ref[0].astype(F32) * conv).astype(BF16)
    y = (ga_ref[0].astype(F32) * _dot(a_ref[0], wa_ref[...])
         + gb_ref[0].astype(F32) * _dot(bb, wb_ref[...])
         + gc_ref[0].astype(F32) * _dot(att_ref[0], wc_ref[...]))
    x1 = x_ref[0] + m[:, 2 * d:3 * d] * _dot(y.astype(BF16), wo_ref[...])
    x1_ref[0] = x1
    h2 = _modulated_norm(x1, g2_ref[...], m[:, 3 * d:4 * d], m[:, 4 * d:5 * d])
    h2_ref[0] = h2.astype(BF16)
    logits = _dot_nt(rwt_ref[...], h2, precision=HIGHEST)
    ex = jnp.exp(logits - jnp.max(logits, axis=0, keepdims=True))
    aff_ref[0] = ex / jnp.sum(ex, axis=0, keepdims=True)


def _merge(xc, mod_l, a, cb, cu, att, ga, gb, gc, conv_w, wa, wb, wc, wo, norm2_g, rwt, nct):
    b, t, d = xc.shape
    nt = t // TOK_TILE
    ne = rwt.shape[0]
    nb = mod_l.shape[0] - 1
    sub = TOK_TILE // HALO
    n8 = t // HALO

    def tok(width):
        return pl.BlockSpec((1, TOK_TILE, width), lambda bi, i: (bi, i, 0))

    def const(shape):
        return pl.BlockSpec(shape, lambda bi, i: (0,) * len(shape))

    cwid = cu.shape[2]
    return pl.pallas_call(
        functools.partial(_merge_kernel, d=d, nct=nct, nt=nt),
        grid=(b, nt),
        in_specs=[
            tok(d),
            pl.BlockSpec((1, 1, N_MOD * d), lambda bi, i: (jnp.where(i < nct, nb, bi), 0, 0)),
            tok(a.shape[2]), tok(cwid), tok(cwid),
            pl.BlockSpec((1, HALO, cwid), lambda bi, i: (bi, jnp.maximum(i * sub - 1, 0), 0)),
            pl.BlockSpec((1, HALO, cwid), lambda bi, i: (bi, jnp.minimum((i + 1) * sub, n8 - 1), 0)),
            tok(att.shape[2]), tok(d), tok(d), tok(d),
            const(conv_w.shape), const(wa.shape), const(wb.shape), const(wc.shape), const(wo.shape),
            const((1, d)), const(rwt.shape),
        ],
        out_specs=[tok(d), tok(d), pl.BlockSpec((1, ne, TOK_TILE), lambda bi, i: (bi, 0, i))],
        out_shape=[jax.ShapeDtypeStruct((b, t, d), F32), jax.ShapeDtypeStruct((b, t, d), BF16),
                   jax.ShapeDtypeStruct((b, ne, t), F32)],
        compiler_params=_params(("parallel", "parallel")),
        name="merge_residual_router",
    )(xc, mod_l, a, cb, cu, cu, cu, att, ga, gb, gc, conv_w, wa, wb, wc, wo, norm2_g, rwt)


def _prefix_count(mask, tri):
    e, n = mask.shape
    carry = jnp.zeros((e, 1), F32)
    outs = []
    for blk in range(n // LANES):
        xb = jnp.where(mask[:, blk * LANES:(blk + 1) * LANES], 1.0, 0.0).astype(BF16)
        pre = _dot(xb, tri) + carry
        carry = pre[:, LANES - 1:LANES]
        outs.append(pre)
    return jnp.concatenate(outs, axis=1) if len(outs) > 1 else outs[0]


def _topk_kernel(aff_ref, tri_ref, pos_ref, *, off, n, cap):
    a = aff_ref[0][:, off:off + n]
    bits = pltpu.bitcast(a, jnp.int32)
    thr = jnp.zeros((a.shape[0], 1), jnp.int32)
    for bit in range(30, -1, -1):
        cand = thr | (1 << bit)
        cnt = jnp.sum(jnp.where(bits >= cand, 1.0, 0.0), axis=1, keepdims=True)
        thr = jnp.where(cnt >= cap, cand, thr)
    gt = bits > thr
    eq = bits == thr
    need = cap - jnp.sum(jnp.where(gt, 1.0, 0.0), axis=1, keepdims=True)
    tri = tri_ref[...]
    eq_rank = _prefix_count(eq, tri)
    sel = jnp.logical_or(gt, jnp.logical_and(eq, eq_rank <= need))
    slot = _prefix_count(sel, tri) - 1.0
    pos_ref[0] = jnp.where(sel, slot, -1.0).astype(jnp.int32)


def _topk_positions(aff_t, tri, off, n, cap):
    b, ne, t = aff_t.shape
    return pl.pallas_call(
        functools.partial(_topk_kernel, off=off, n=n, cap=cap),
        grid=(b,),
        in_specs=[pl.BlockSpec((1, ne, t), lambda bi: (bi, 0, 0)), pl.BlockSpec((LANES, LANES), lambda bi: (0, 0))],
        out_specs=pl.BlockSpec((1, ne, n), lambda bi: (bi, 0, 0)),
        out_shape=jax.ShapeDtypeStruct((b, ne, n), jnp.int32),
        compiler_params=_params(("parallel",)),
        name="expert_choice_topk",
    )(aff_t, tri)


def _gather_kernel(h_ref, pos_ref, xs_ref, *, off, n, cap, kt):
    pos = pos_ref[0, 0]
    acc = jnp.zeros(xs_ref.shape[1:], F32)
    for c0 in range(0, n, kt):
        slot = lax.broadcasted_iota(jnp.int32, (cap, kt), 0)
        onehot = jnp.where(pos[:, c0:c0 + kt] == slot, 1.0, 0.0).astype(BF16)
        acc = acc + _dot(onehot, h_ref[0, off + c0:off + c0 + kt, :])
    xs_ref[0] = acc.astype(BF16)


def _gather_tokens(h2, pos, off, n, cap):
    b, t, d = h2.shape
    ne = pos.shape[1]
    kt = min(n, 512)
    return pl.pallas_call(
        functools.partial(_gather_kernel, off=off, n=n, cap=cap, kt=kt),
        grid=(b, ne),
        in_specs=[pl.BlockSpec((1, t, d), lambda bi, e: (bi, 0, 0)),
                  pl.BlockSpec((1, 1, 1, n), lambda bi, e: (bi, e, 0, 0))],
        out_specs=pl.BlockSpec((1, cap, d), lambda bi, e: (e, bi, 0)),
        out_shape=jax.ShapeDtypeStruct((ne, b * cap, d), BF16),
        compiler_params=_params(("parallel", "parallel")),
        name="expert_gather",
    )(h2, pos.reshape(b, ne, 1, n))


def _ffn_kernel(xs_ref, wg_ref, wu_ref, wd_ref, ys_ref):
    x = xs_ref[0]
    g = _dot(x, wg_ref[0])
    hid = (g * _sigmoid(g)) * _dot(x, wu_ref[0])
    ys_ref[0] = _dot(hid.astype(BF16), wd_ref[0]).astype(BF16)


def _expert_ffn(xs, wg, wu, wd):
    ne, m, d = xs.shape
    ff = wg.shape[2]
    tm = min(m, 512)
    return pl.pallas_call(
        _ffn_kernel,
        grid=(ne, m // tm),
        in_specs=[pl.BlockSpec((1, tm, d), lambda e, j: (e, j, 0)),
                  pl.BlockSpec((1, d, ff), lambda e, j: (e, 0, 0)),
                  pl.BlockSpec((1, d, ff), lambda e, j: (e, 0, 0)),
                  pl.BlockSpec((1, ff, d), lambda e, j: (e, 0, 0))],
        out_specs=pl.BlockSpec((1, tm, d), lambda e, j: (e, j, 0)),
        out_shape=jax.ShapeDtypeStruct((ne, m, d), BF16),
        compiler_params=_params(("parallel", "parallel")),
        name="expert_ffn",
    )(xs, wg, wu, wd)


def _scatter_kernel(x_ref, mod_ref, ys_ref, pos_ref, aff_ref, fg_ref, o_ref, *, d, ne, cap, final):
    m = mod_ref[0]
    pos = pos_ref[0]
    aff = aff_ref[0]
    rows = pos.shape[0]
    slot = lax.broadcasted_iota(jnp.int32, (rows, cap), 1)
    acc = jnp.zeros((rows, d), F32)
    for e in range(ne):
        onehot = jnp.where(pos[:, e:e + 1] == slot, 1.0, 0.0).astype(BF16)
        acc = acc + aff[:, e:e + 1] * _dot(onehot, ys_ref[e])
    x2 = x_ref[0] + m[:, 5 * d:6 * d] * acc
    if final:
        ms = jnp.mean(x2 * x2, axis=-1, keepdims=True)
        x2 = x2 * lax.rsqrt(ms + EPS) * fg_ref[...]
    o_ref[0] = x2


def _scatter_residual(x1, mod_l, ys, pos_n, aff_n, final_g, off, n, cap, mod_row_ctx, final):
    b, t, d = x1.shape
    ne = ys.shape[0]
    nb = mod_l.shape[0] - 1
    ot = off // TOK_TILE
    if final:
        out_shape = jax.ShapeDtypeStruct((b, n, d), F32)
        out_spec = pl.BlockSpec((1, TOK_TILE, d), lambda bi, i: (bi, i, 0))
        aliases = {}
    else:
        out_shape = jax.ShapeDtypeStruct((b, t, d), F32)
        out_spec = pl.BlockSpec((1, TOK_TILE, d), lambda bi, i: (bi, i + ot, 0))
        aliases = {0: 0}
    return pl.pallas_call(
        functools.partial(_scatter_kernel, d=d, ne=ne, cap=cap, final=final),
        grid=(b, n // TOK_TILE),
        in_specs=[
            pl.BlockSpec((1, TOK_TILE, d), lambda bi, i: (bi, i + ot, 0)),
            pl.BlockSpec((1, 1, N_MOD * d), lambda bi, i: (nb if mod_row_ctx else bi, 0, 0)),
            pl.BlockSpec((ne, cap, d), lambda bi, i: (0, bi, 0)),
            pl.BlockSpec((1, TOK_TILE, ne), lambda bi, i: (bi, i, 0)),
            pl.BlockSpec((1, TOK_TILE, ne), lambda bi, i: (bi, i + ot, 0)),
            pl.BlockSpec((1, d), lambda bi, i: (0, 0)),
        ],
        out_specs=out_spec,
        out_shape=out_shape,
        input_output_aliases=aliases,
        compiler_params=_params(("parallel", "parallel")),
        name="expert_scatter_residual",
    )(x1, mod_l, ys, pos_n, aff_n, final_g)


def _rope_tables(ctx_len, seq, hd):
    rows = seq // GRID_W
    row = jnp.repeat(jnp.arange(rows), GRID_W).astype(F32)
    col = jnp.tile(jnp.arange(GRID_W), rows).astype(F32)
    inv = ROPE_THETA ** (-jnp.arange(0, hd // 2, 2, dtype=F32) / (hd // 2))
    ang = jnp.concatenate([row[:, None] * inv, col[:, None] * inv], axis=-1)
    cos = jnp.repeat(jnp.cos(ang), 2, axis=-1)
    sin = jnp.repeat(jnp.sin(ang), 2, axis=-1) * jnp.tile(jnp.array([-1.0, 1.0], F32), hd // 2)
    cos = jnp.concatenate([jnp.ones((ctx_len, hd), F32), cos], axis=0)
    sin = jnp.concatenate([jnp.zeros((ctx_len, hd), F32), sin], axis=0)
    rep = LANES // hd
    return jnp.tile(cos, (1, rep)), jnp.tile(sin, (1, rep))


def kernel(x, c, ctx, c_ctx, ada_w, ada_b, norm1_g, norm2_g, w_in, hg_lb_logits, hg_norm_g, conv_w, q_norm_g, k_norm_g, w_proj_a, w_proj_b, w_proj_c, w_out, router_w, w_gate, w_up, w_down, final_norm_g):
    b, s, d = x.shape
    ctx_len = ctx.shape[1]
    t = ctx_len + s
    depth = w_in.shape[0]
    hgw = hg_norm_g.shape[1]
    cw = conv_w.shape[2]
    aw = w_proj_c.shape[1]
    hd = q_norm_g.shape[1]
    kvw = (w_in.shape[2] - 5 * hgw - 3 * cw - aw - 3 * d) // 2
    ne = router_w.shape[2]
    nct = ctx_len // TOK_TILE
    assert ctx_len % TOK_TILE == 0 and s % TOK_TILE == 0 and LANES % hd == 0 and cw == d
    dims = dict(hgw=hgw, cw=cw, aw=aw, kvw=kvw, hd=hd, nct=nct)

    xc = jnp.concatenate([ctx, x], axis=1)
    n_rows = -(-(b + 1) // 8) * 8
    cvec = jnp.concatenate([c, c_ctx[None, :], jnp.zeros((n_rows - b - 1, d), F32)], axis=0)
    mod = _modulation(cvec, ada_w, ada_b)
    mod = mod[:, :b + 1].reshape(depth, b + 1, 1, N_MOD * d)

    cos_t, sin_t = _rope_tables(ctx_len, s, hd)
    lane = jnp.arange(LANES)
    gmat = (lane[:, None] // hd == lane[None, :] // hd).astype(BF16)
    tri = (lane[:, None] <= lane[None, :]).astype(BF16)
    rep = LANES // hd
    lb_logits = hg_lb_logits.astype(F32)

    out = None
    for l in range(depth):
        last = l == depth - 1
        mod_l = mod[l]
        (hq, zf, zb, hi, hog, cb, cu, aq, ak, av, ga, gb, gc) = _in_projection(
            xc, mod_l, norm1_g[l][None, :], w_in[l].astype(BF16), cos_t, sin_t,
            jnp.tile(q_norm_g[l], rep)[None, :], jnp.tile(k_norm_g[l], rep)[None, :], gmat, dims)
        a = _hgrn(hq, zf, zb, hi, hog, lb_logits, hg_norm_g[l][None, :], l, ctx_len)
        att = _attention(aq, ak, av, ctx_len)
        x1, h2, aff_t = _merge(xc, mod_l, a, cb, cu, att, ga, gb, gc, conv_w[l],
                               w_proj_a[l].astype(BF16), w_proj_b[l].astype(BF16), w_proj_c[l].astype(BF16),
                               w_out[l].astype(BF16), norm2_g[l][None, :], router_w[l].T, nct)
        aff_n = jnp.swapaxes(aff_t, 1, 2)
        wg, wu, wd = w_gate[l].astype(BF16), w_up[l].astype(BF16), w_down[l].astype(BF16)

        def moe(stream, off, n, mod_row_ctx, final):
            cap = CAPACITY_FACTOR * n // ne
            pos = _topk_positions(aff_t, tri, off, n, cap)
            xs = _gather_tokens(h2, pos, off, n, cap)
            ys = _expert_ffn(xs, wg, wu, wd)
            return _scatter_residual(stream, mod_l, ys, jnp.swapaxes(pos, 1, 2), aff_n,
                                     final_norm_g[None, :], off, n, cap, mod_row_ctx, final)

        if last:
            out = moe(x1, ctx_len, s, False, True)
        else:
            xc = moe(x1, ctx_len, s, False, False)
            xc = moe(xc, 0, ctx_len, True, False)
    return out
```

```python
import functools

import jax
import jax.numpy as jnp
from jax import lax
from jax.experimental import pallas as pl
from jax.experimental.pallas import tpu as pltpu

F32 = jnp.float32
BF16 = jnp.bfloat16

EPS = 1e-6
N_MOD = 6
HG_DK = 128
GRID_W = 64
ROPE_THETA = 10000.0
CAPACITY_FACTOR = 2

LANES = 128
TOK_TILE = 256
HG_CHUNK = 64
HG_SUB = 32
HG_EXP_CLAMP = 80.0
HG_HEADS_PER_STEP = 2
KV_TILE = 256
HALO = 16
VMEM_LIMIT = 56 * 1024 * 1024

HIGHEST = lax.Precision.HIGHEST


def _dot(a, b, precision=None):
    return jnp.dot(a, b, preferred_element_type=F32, precision=precision)


def _dot_nt(a, b, precision=None):
    return lax.dot_general(a, b, (((1,), (1,)), ((), ())), preferred_element_type=F32, precision=precision)


def _dot_tn(a, b):
    return lax.dot_general(a, b, (((0,), (0,)), ((), ())), preferred_element_type=F32)


def _sigmoid(x):
    return 1.0 / (1.0 + jnp.exp(-x))


def _params(sem, vmem=VMEM_LIMIT):
    return pltpu.CompilerParams(dimension_semantics=sem, vmem_limit_bytes=vmem)


def _mod_kernel(c_ref, w_ref, b_ref, o_ref):
    c = c_ref[...]
    sc = c * _sigmoid(c)
    o_ref[0] = _dot(sc, w_ref[0], precision=HIGHEST) + b_ref[0]


def _modulation(cvec, ada_w, ada_b):
    depth, d, n = ada_w.shape
    r = cvec.shape[0]
    tn = d
    return pl.pallas_call(
        _mod_kernel,
        grid=(depth, n // tn),
        in_specs=[
            pl.BlockSpec((r, d), lambda l, j: (0, 0)),
            pl.BlockSpec((1, d, tn), lambda l, j: (l, 0, j)),
            pl.BlockSpec((1, 1, tn), lambda l, j: (l, 0, j)),
        ],
        out_specs=pl.BlockSpec((1, r, tn), lambda l, j: (l, 0, j)),
        out_shape=jax.ShapeDtypeStruct((depth, r, n), F32),
        compiler_params=_params(("parallel", "parallel")),
        name="adaln_modulation",
    )(cvec, ada_w, ada_b.reshape(depth, 1, n))


def _modulated_norm(x, g, shift, scale):
    ms = jnp.mean(x * x, axis=-1, keepdims=True)
    return (x * lax.rsqrt(ms + EPS) * g) * (1.0 + scale) + shift


def _headnorm_rope(p, gain, gmat, cos, sin, hd, post_scale):
    rows, width = p.shape
    lane = lax.broadcasted_iota(jnp.int32, (rows, LANES), 1)
    even = (lane % 2) == 0
    outs = []
    for cb in range(width // LANES):
        xb = p[:, cb * LANES:(cb + 1) * LANES]
        sq = xb * xb
        hi = sq.astype(BF16)
        lo = (sq - hi.astype(F32)).astype(BF16)
        ss = _dot(hi, gmat) + _dot(lo, gmat)
        y = xb * lax.rsqrt(ss * (1.0 / hd) + EPS) * gain
        y_next = pltpu.roll(y, LANES - 1, axis=1)
        y_prev = pltpu.roll(y, 1, axis=1)
        ysw = jnp.where(even, y_next, y_prev)
        outs.append((y * cos + ysw * sin) * post_scale)
    return jnp.concatenate(outs, axis=1) if len(outs) > 1 else outs[0]


def _lower_bounds(lg, layer):
    e = jnp.exp(lg - jnp.max(lg, axis=0, keepdims=True))
    sm = e / jnp.sum(e, axis=0, keepdims=True)
    lb = jnp.zeros(lg.shape[1:], F32)
    for j in range(1, layer + 1):
        lb = lb + sm[j]
    return lb


def _forget_gate(z, lbd):
    sp = jnp.log(1.0 + jnp.exp(-jnp.abs(z)))
    ls = jnp.minimum(z, 0.0) - sp
    kk = (1.0 - lbd) * jnp.exp(jnp.minimum(-z, 0.0) - sp)
    a = jnp.log(lbd)
    t = jnp.log(1.0 - lbd) + ls
    mx = jnp.maximum(a, t)
    mn = jnp.minimum(a, t)
    return mx + jnp.log(1.0 + jnp.exp(mn - mx)), kk


def _inproj_kernel(x_ref, mod_ref, g_ref, w_ref, cos_ref, sin_ref, qg_ref, kg_ref, gmat_ref, lbl_ref,
                   hq_ref, lff_ref, lfb_ref, kkf_ref, kkb_ref, hi_ref, hog_ref, cb_ref, cu_ref, aq_ref, ak_ref,
                   av_ref, ga_ref, gb_ref, gc_ref, *, d, hgw, cw, aw, kvw, hd, layer):
    x = x_ref[0]
    m = mod_ref[0]
    h = _modulated_norm(x, g_ref[...], m[:, 0:d], m[:, d:2 * d]).astype(BF16)

    def proj(lo, width):
        return _dot(h, w_ref[:, lo:lo + width])

    lb = _lower_bounds(lbl_ref[...], layer)
    o = 0
    hq_ref[0] = proj(o, hgw).astype(BF16); o += hgw
    lf, kk = _forget_gate(proj(o, hgw), lb[0:1, :]); o += hgw
    lff_ref[0] = _chunk_cumsum(lf, False)
    kkf_ref[0] = kk.astype(BF16)
    lf, kk = _forget_gate(proj(o, hgw), lb[1:2, :]); o += hgw
    lfb_ref[0] = _chunk_cumsum(lf, True)
    kkb_ref[0] = kk.astype(BF16)
    hi_ref[0] = proj(o, hgw).astype(BF16); o += hgw
    g = proj(o, hgw); o += hgw
    hog_ref[0] = (g * _sigmoid(g)).astype(BF16)
    cb_ref[0] = proj(o, cw).astype(BF16); o += cw
    cc = proj(o, cw); o += cw
    cx = proj(o, cw); o += cw
    cu_ref[0] = (cc * cx).astype(BF16)
    cos = cos_ref[...]
    sin = sin_ref[...]
    gmat = gmat_ref[...]
    q = proj(o, aw); o += aw
    aq_ref[0] = _headnorm_rope(q, qg_ref[...], gmat, cos, sin, hd, hd ** -0.5).astype(BF16)
    k = proj(o, kvw); o += kvw
    kn = _headnorm_rope(k, kg_ref[...], gmat, cos, sin, hd, 1.0).astype(BF16)
    vv = proj(o, kvw).astype(BF16); o += kvw
    for g in range(kvw // hd):
        ak_ref[0, g] = kn[:, g * hd:(g + 1) * hd]
        av_ref[0, g] = vv[:, g * hd:(g + 1) * hd]
    ga_ref[0] = _sigmoid(proj(o, d)).astype(BF16); o += d
    gb_ref[0] = _sigmoid(proj(o, d)).astype(BF16); o += d
    gc_ref[0] = _sigmoid(proj(o, d)).astype(BF16); o += d


def _in_projection(xc, mod_l, norm_g, w_bf, cos_t, sin_t, qg, kg, gmat, lb_logits, layer, dims):
    b, t, d = xc.shape
    hgw, cw, aw, kvw, hd, nct = dims["hgw"], dims["cw"], dims["aw"], dims["kvw"], dims["hd"], dims["nct"]
    nt = t // TOK_TILE
    in_w = w_bf.shape[1]
    nb = mod_l.shape[0] - 1

    def tok(width):
        return pl.BlockSpec((1, TOK_TILE, width), lambda bi, i: (bi, i, 0))

    def const(shape):
        return pl.BlockSpec(shape, lambda bi, i: (0,) * len(shape))

    widths = [(hgw, BF16), (hgw, F32), (hgw, F32), (hgw, BF16), (hgw, BF16), (hgw, BF16), (hgw, BF16),
              (cw, BF16), (cw, BF16), (aw, BF16), (kvw, BF16), (kvw, BF16), (d, BF16), (d, BF16), (d, BF16)]
    ng = kvw // hd
    out_specs = [tok(w) for w, _ in widths]
    out_shape = [jax.ShapeDtypeStruct((b, t, w), dt) for w, dt in widths]
    for idx in (10, 11):
        out_specs[idx] = pl.BlockSpec((1, ng, TOK_TILE, hd), lambda bi, i: (bi, 0, i, 0))
        out_shape[idx] = jax.ShapeDtypeStruct((b, ng, t, hd), BF16)
    return pl.pallas_call(
        functools.partial(_inproj_kernel, d=d, hgw=hgw, cw=cw, aw=aw, kvw=kvw, hd=hd, layer=layer),
        grid=(b, nt),
        in_specs=[
            tok(d),
            pl.BlockSpec((1, 1, N_MOD * d), lambda bi, i: (jnp.where(i < nct, nb, bi), 0, 0)),
            const((1, d)),
            pl.BlockSpec((d, in_w), lambda bi, i: (0, 0), pipeline_mode=pl.Buffered(1)),
            pl.BlockSpec((TOK_TILE, LANES), lambda bi, i: (i, 0)),
            pl.BlockSpec((TOK_TILE, LANES), lambda bi, i: (i, 0)),
            const((1, LANES)),
            const((1, LANES)),
            const((LANES, LANES)),
            const(lb_logits.shape),
        ],
        out_specs=out_specs,
        out_shape=out_shape,
        compiler_params=_params(("parallel", "parallel")),
        name="in_projection",
    )(xc, mod_l, norm_g, w_bf, cos_t, sin_t, qg, kg, gmat, lb_logits)


def _chunk_cumsum(x, reverse):
    n = x.shape[0]
    pos = lax.broadcasted_iota(jnp.int32, x.shape, 0) % HG_CHUNK
    s = 1
    while s < HG_CHUNK:
        if reverse:
            x = x + jnp.where(pos < HG_CHUNK - s, pltpu.roll(x, n - s, axis=0), 0.0)
        else:
            x = x + jnp.where(pos >= s, pltpu.roll(x, s, axis=0), 0.0)
        s *= 2
    return x


def _hgrn_chunk(q, cs, kk, v, st_ref, reverse):
    c = q.shape[0]
    nsb = c // HG_SUB
    anchors = []
    for i in range(nsb):
        r = i * HG_SUB + (HG_SUB // 2 if reverse else HG_SUB // 2 - 1)
        anchors.append(cs[r:r + 1, :])
    c_anchor = jnp.concatenate([jnp.broadcast_to(a, (HG_SUB, a.shape[1])) for a in anchors], axis=0)
    c_end = cs[0:1, :] if reverse else cs[c - 1:c, :]
    qh = q * jnp.exp(cs - c_anchor)
    zero = jnp.zeros((HG_SUB, q.shape[1]), F32)
    q_ext = jnp.concatenate(
        [jnp.concatenate([qh[i * HG_SUB:(i + 1) * HG_SUB] if j == i else zero for j in range(nsb)], axis=0)
         for i in range(nsb)], axis=1)
    k_ext = jnp.concatenate([kk * jnp.exp(jnp.minimum(a - cs, HG_EXP_CLAMP)) for a in anchors], axis=1)
    a = _dot_nt(q_ext.astype(BF16), k_ext.astype(BF16))
    row = lax.broadcasted_iota(jnp.int32, (c, c), 0)
    col = lax.broadcasted_iota(jnp.int32, (c, c), 1)
    keep = (col >= row) if reverse else (col <= row)
    st = st_ref[...]
    o = (_dot(jnp.where(keep, a, 0.0).astype(BF16), v)
         + _dot_nt((qh * jnp.exp(c_anchor)).astype(BF16), st.astype(BF16)))
    kh = (kk * jnp.exp(c_end - cs)).astype(BF16)
    st_ref[...] = st * jnp.exp(c_end) + _dot_tn(v, kh)
    return o


def _hgrn_kernel(hq_ref, csf_ref, csb_ref, kkf_ref, kkb_ref, hi_ref, hog_ref, ng_ref, a_ref,
                 of_ref, ob_ref, stf_ref, stb_ref, *, n_chunks, n_ctx_chunks, n_heads):
    stf_ref[...] = jnp.zeros_like(stf_ref)
    stb_ref[...] = jnp.zeros_like(stb_ref)
    c = HG_CHUNK

    def step(s, carry):
        rf = pl.multiple_of(s * c, c)
        cbk = jnp.where(s < n_ctx_chunks, n_ctx_chunks - 1 - s, n_chunks - 1 - (s - n_ctx_chunks))
        rb = pl.multiple_of(cbk * c, c)
        for h in range(n_heads):
            ln = slice(h * HG_DK, (h + 1) * HG_DK)
            of_ref[pl.ds(rf, c), ln] = _hgrn_chunk(
                hq_ref[0, pl.ds(rf, c), ln].astype(F32), csf_ref[0, pl.ds(rf, c), ln],
                kkf_ref[0, pl.ds(rf, c), ln].astype(F32), hi_ref[0, pl.ds(rf, c), ln], stf_ref.at[h], False)
            ob_ref[pl.ds(rb, c), ln] = _hgrn_chunk(
                hq_ref[0, pl.ds(rb, c), ln].astype(F32), csb_ref[0, pl.ds(rb, c), ln],
                kkb_ref[0, pl.ds(rb, c), ln].astype(F32), hi_ref[0, pl.ds(rb, c), ln], stb_ref.at[h], True)
        return carry

    lax.fori_loop(0, n_chunks, step, 0)

    def readout(i, carry):
        r0 = pl.multiple_of(i * TOK_TILE, TOK_TILE)
        for h in range(n_heads):
            ln = slice(h * HG_DK, (h + 1) * HG_DK)
            o = of_ref[pl.ds(r0, TOK_TILE), ln] + ob_ref[pl.ds(r0, TOK_TILE), ln]
            ms = jnp.mean(o * o, axis=-1, keepdims=True)
            y = o * lax.rsqrt(ms + EPS) * ng_ref[:, ln]
            a_ref[0, pl.ds(r0, TOK_TILE), ln] = (y * hog_ref[0, pl.ds(r0, TOK_TILE), ln].astype(F32)).astype(BF16)
        return carry

    lax.fori_loop(0, (n_chunks * c) // TOK_TILE, readout, 0)


def _hgrn(hq, csf, csb, kkf, kkb, hi, hog, norm_g, ctx_len):
    b, t, hgw = hq.shape
    nh = HG_HEADS_PER_STEP
    wid = nh * HG_DK

    def seq():
        return pl.BlockSpec((1, t, wid), lambda bi, h: (bi, 0, h))

    return pl.pallas_call(
        functools.partial(_hgrn_kernel, n_chunks=t // HG_CHUNK, n_ctx_chunks=ctx_len // HG_CHUNK, n_heads=nh),
        grid=(b, hgw // wid),
        in_specs=[seq(), seq(), seq(), seq(), seq(), seq(), seq(),
                  pl.BlockSpec((1, wid), lambda bi, h: (0, h))],
        out_specs=seq(),
        out_shape=jax.ShapeDtypeStruct((b, t, hgw), BF16),
        scratch_shapes=[pltpu.VMEM((t, wid), F32), pltpu.VMEM((t, wid), F32),
                        pltpu.VMEM((nh, HG_DK, HG_DK), F32), pltpu.VMEM((nh, HG_DK, HG_DK), F32)],
        compiler_params=_params(("parallel", "parallel")),
        name="hgrn2_bidirectional",
    )(hq, csf, csb, kkf, kkb, hi, hog, norm_g)


def _attn_kernel(q_ref, k_ref, v_ref, o_ref, *, group, hd, nct, ctx_len, t_all):
    i = pl.program_id(2)

    def run(n_keys):
        k = k_ref[0, 0, 0:n_keys, :]
        v = v_ref[0, 0, 0:n_keys, :]
        outs = []
        for h in range(group):
            s = _dot_nt(q_ref[0, :, h * hd:(h + 1) * hd], k)
            p = jnp.exp(s - jnp.max(s, axis=1, keepdims=True))
            l = jnp.sum(p, axis=1, keepdims=True)
            outs.append((_dot(p.astype(BF16), v) / l).astype(BF16))
        o_ref[0] = jnp.concatenate(outs, axis=1)

    @pl.when(i < nct)
    def _():
        run(ctx_len)

    @pl.when(i >= nct)
    def _():
        run(t_all)


def _attention(aq, ak, av, ctx_len):
    b, t, aw = aq.shape
    n_groups, hd = ak.shape[1], ak.shape[3]
    group = aw // (n_groups * hd)
    nt = t // TOK_TILE
    return pl.pallas_call(
        functools.partial(_attn_kernel, group=group, hd=hd, nct=ctx_len // TOK_TILE, ctx_len=ctx_len, t_all=t),
        grid=(b, n_groups, nt),
        in_specs=[
            pl.BlockSpec((1, TOK_TILE, group * hd), lambda bi, g, i: (bi, i, g)),
            pl.BlockSpec((1, 1, t, hd), lambda bi, g, i: (bi, g, 0, 0)),
            pl.BlockSpec((1, 1, t, hd), lambda bi, g, i: (bi, g, 0, 0)),
        ],
        out_specs=pl.BlockSpec((1, TOK_TILE, group * hd), lambda bi, g, i: (bi, i, g)),
        out_shape=jax.ShapeDtypeStruct((b, t, aw), BF16),
        compiler_params=_params(("parallel", "parallel", "parallel")),
        name="gqa_attention",
    )(aq, ak, av)


def _merge_kernel(x_ref, mod_ref, a_ref, cb_ref, cu_ref, cup_ref, cun_ref, att_ref, ga_ref, gb_ref, gc_ref,
                  cw_ref, wa_ref, wb_ref, wc_ref, wo_ref, g2_ref, rwt_ref,
                  x1_ref, h2_ref, aff_ref, *, d, nct, nt):
    i = pl.program_id(1)
    m = mod_ref[0]
    rows = x_ref.shape[1]
    u = cu_ref[0].astype(F32)
    row = lax.broadcasted_iota(jnp.int32, u.shape, 0)
    has_prev = jnp.logical_and(i != 0, i != nct)
    has_next = jnp.logical_and(i != nct - 1, i != nt - 1)
    prev_row = jnp.where(has_prev, cup_ref[0, HALO - 1:HALO, :].astype(F32), 0.0)
    next_row = jnp.where(has_next, cun_ref[0, 0:1, :].astype(F32), 0.0)
    u_prev = jnp.where(row == 0, prev_row, pltpu.roll(u, 1, axis=0))
    u_next = jnp.where(row == rows - 1, next_row, pltpu.roll(u, rows - 1, axis=0))
    cw = cw_ref[...]
    conv = cw[0:1, :] * u_prev + cw[1:2, :] * u + cw[2:3, :] * u_next
    bb = (cb_ref[0].astype(F32) * conv).astype(BF16)
    y = (ga_ref[0].astype(F32) * _dot(a_ref[0], wa_ref[...])
         + gb_ref[0].astype(F32) * _dot(bb, wb_ref[...])
         + gc_ref[0].astype(F32) * _dot(att_ref[0], wc_ref[...]))
    x1 = x_ref[0] + m[:, 2 * d:3 * d] * _dot(y.astype(BF16), wo_ref[...])
    x1_ref[0] = x1
    h2 = _modulated_norm(x1, g2_ref[...], m[:, 3 * d:4 * d], m[:, 4 * d:5 * d])
    h2_ref[0] = h2.astype(BF16)
    logits = _dot_nt(rwt_ref[...], h2, precision=HIGHEST)
    ex = jnp.exp(logits - jnp.max(logits, axis=0, keepdims=True))
    aff_ref[0] = ex / jnp.sum(ex, axis=0, keepdims=True)


def _merge(xc, mod_l, a, cb, cu, att, ga, gb, gc, conv_w, wa, wb, wc, wo, norm2_g, rwt, nct):
    b, t, d = xc.shape
    nt = t // TOK_TILE
    ne = rwt.shape[0]
    nb = mod_l.shape[0] - 1
    sub = TOK_TILE // HALO
    n8 = t // HALO

    def tok(width):
        return pl.BlockSpec((1, TOK_TILE, width), lambda bi, i: (bi, i, 0))

    def const(shape):
        return pl.BlockSpec(shape, lambda bi, i: (0,) * len(shape))

    cwid = cu.shape[2]
    return pl.pallas_call(
        functools.partial(_merge_kernel, d=d, nct=nct, nt=nt),
        grid=(b, nt),
        in_specs=[
            tok(d),
            pl.BlockSpec((1, 1, N_MOD * d), lambda bi, i: (jnp.where(i < nct, nb, bi), 0, 0)),
            tok(a.shape[2]), tok(cwid), tok(cwid),
            pl.BlockSpec((1, HALO, cwid), lambda bi, i: (bi, jnp.maximum(i * sub - 1, 0), 0)),
            pl.BlockSpec((1, HALO, cwid), lambda bi, i: (bi, jnp.minimum((i + 1) * sub, n8 - 1), 0)),
            tok(att.shape[2]), tok(d), tok(d), tok(d),
            const(conv_w.shape), const(wa.shape), const(wb.shape), const(wc.shape), const(wo.shape),
            const((1, d)), const(rwt.shape),
        ],
        out_specs=[tok(d), tok(d), pl.BlockSpec((1, ne, TOK_TILE), lambda bi, i: (bi, 0, i))],
        out_shape=[jax.ShapeDtypeStruct((b, t, d), F32), jax.ShapeDtypeStruct((b, t, d), BF16),
                   jax.ShapeDtypeStruct((b, ne, t), F32)],
        compiler_params=_params(("parallel", "parallel")),
        name="merge_residual_router",
    )(xc, mod_l, a, cb, cu, cu, cu, att, ga, gb, gc, conv_w, wa, wb, wc, wo, norm2_g, rwt)


def _prefix_count(mask, tri):
    e, n = mask.shape
    carry = jnp.zeros((e, 1), F32)
    outs = []
    for blk in range(n // LANES):
        xb = jnp.where(mask[:, blk * LANES:(blk + 1) * LANES], 1.0, 0.0).astype(BF16)
        pre = _dot(xb, tri) + carry
        carry = pre[:, LANES - 1:LANES]
        outs.append(pre)
    return jnp.concatenate(outs, axis=1) if len(outs) > 1 else outs[0]


def _topk_kernel(aff_ref, tri_ref, pos_ref, *, off, n, cap):
    a = aff_ref[0][:, off:off + n]
    bits = pltpu.bitcast(a, jnp.int32)
    thr = jnp.zeros((a.shape[0], 1), jnp.int32)
    for bit in range(30, -1, -1):
        cand = thr | (1 << bit)
        cnt = jnp.sum(jnp.where(bits >= cand, 1.0, 0.0), axis=1, keepdims=True)
        thr = jnp.where(cnt >= cap, cand, thr)
    gt = bits > thr
    eq = bits == thr
    need = cap - jnp.sum(jnp.where(gt, 1.0, 0.0), axis=1, keepdims=True)
    tri = tri_ref[...]
    eq_rank = _prefix_count(eq, tri)
    sel = jnp.logical_or(gt, jnp.logical_and(eq, eq_rank <= need))
    slot = _prefix_count(sel, tri) - 1.0
    pos_ref[0] = jnp.where(sel, slot, -1.0).astype(jnp.int32)


def _topk_positions(aff_t, tri, off, n, cap):
    b, ne, t = aff_t.shape
    return pl.pallas_call(
        functools.partial(_topk_kernel, off=off, n=n, cap=cap),
        grid=(b,),
        in_specs=[pl.BlockSpec((1, ne, t), lambda bi: (bi, 0, 0)), pl.BlockSpec((LANES, LANES), lambda bi: (0, 0))],
        out_specs=pl.BlockSpec((1, ne, n), lambda bi: (bi, 0, 0)),
        out_shape=jax.ShapeDtypeStruct((b, ne, n), jnp.int32),
        compiler_params=_params(("parallel",)),
        name="expert_choice_topk",
    )(aff_t, tri)


def _gather_kernel(h_ref, pos_ref, xs_ref, *, off, n, cap, kt):
    pos = pos_ref[0, 0]
    acc = jnp.zeros(xs_ref.shape[1:], F32)
    for c0 in range(0, n, kt):
        slot = lax.broadcasted_iota(jnp.int32, (cap, kt), 0)
        onehot = jnp.where(pos[:, c0:c0 + kt] == slot, 1.0, 0.0).astype(BF16)
        acc = acc + _dot(onehot, h_ref[0, off + c0:off + c0 + kt, :])
    xs_ref[0] = acc.astype(BF16)


def _gather_tokens(h2, pos, off, n, cap):
    b, t, d = h2.shape
    ne = pos.shape[1]
    kt = min(n, 512)
    return pl.pallas_call(
        functools.partial(_gather_kernel, off=off, n=n, cap=cap, kt=kt),
        grid=(b, ne),
        in_specs=[pl.BlockSpec((1, t, d), lambda bi, e: (bi, 0, 0)),
                  pl.BlockSpec((1, 1, 1, n), lambda bi, e: (bi, e, 0, 0))],
        out_specs=pl.BlockSpec((1, cap, d), lambda bi, e: (e, bi, 0)),
        out_shape=jax.ShapeDtypeStruct((ne, b * cap, d), BF16),
        compiler_params=_params(("parallel", "parallel")),
        name="expert_gather",
    )(h2, pos.reshape(b, ne, 1, n))


def _ffn_kernel(xs_ref, wg_ref, wu_ref, wd_ref, ys_ref):
    x = xs_ref[0]
    g = _dot(x, wg_ref[0])
    hid = (g * _sigmoid(g)) * _dot(x, wu_ref[0])
    ys_ref[0] = _dot(hid.astype(BF16), wd_ref[0]).astype(BF16)


def _expert_ffn(xs, wg, wu, wd):
    ne, m, d = xs.shape
    ff = wg.shape[2]
    tm = min(m, 512)
    return pl.pallas_call(
        _ffn_kernel,
        grid=(ne, m // tm),
        in_specs=[pl.BlockSpec((1, tm, d), lambda e, j: (e, j, 0)),
                  pl.BlockSpec((1, d, ff), lambda e, j: (e, 0, 0)),
                  pl.BlockSpec((1, d, ff), lambda e, j: (e, 0, 0)),
                  pl.BlockSpec((1, ff, d), lambda e, j: (e, 0, 0))],
        out_specs=pl.BlockSpec((1, tm, d), lambda e, j: (e, j, 0)),
        out_shape=jax.ShapeDtypeStruct((ne, m, d), BF16),
        compiler_params=_params(("parallel", "parallel")),
        name="expert_ffn",
    )(xs, wg, wu, wd)


def _scatter_kernel(x_ref, mod_ref, ys_ref, pos_ref, aff_ref, fg_ref, o_ref, *, d, ne, cap, final):
    m = mod_ref[0]
    pos = pos_ref[0]
    aff = aff_ref[0]
    rows = pos.shape[0]
    slot = lax.broadcasted_iota(jnp.int32, (rows, cap), 1)
    acc = jnp.zeros((rows, d), F32)
    for e in range(ne):
        onehot = jnp.where(pos[:, e:e + 1] == slot, 1.0, 0.0).astype(BF16)
        acc = acc + aff[:, e:e + 1] * _dot(onehot, ys_ref[e])
    x2 = x_ref[0] + m[:, 5 * d:6 * d] * acc
    if final:
        ms = jnp.mean(x2 * x2, axis=-1, keepdims=True)
        x2 = x2 * lax.rsqrt(ms + EPS) * fg_ref[...]
    o_ref[0] = x2


def _scatter_residual(x1, mod_l, ys, pos_n, aff_n, final_g, off, n, cap, mod_row_ctx, final):
    b, t, d = x1.shape
    ne = ys.shape[0]
    nb = mod_l.shape[0] - 1
    ot = off // TOK_TILE
    if final:
        out_shape = jax.ShapeDtypeStruct((b, n, d), F32)
        out_spec = pl.BlockSpec((1, TOK_TILE, d), lambda bi, i: (bi, i, 0))
        aliases = {}
    else:
        out_shape = jax.ShapeDtypeStruct((b, t, d), F32)
        out_spec = pl.BlockSpec((1, TOK_TILE, d), lambda bi, i: (bi, i + ot, 0))
        aliases = {0: 0}
    return pl.pallas_call(
        functools.partial(_scatter_kernel, d=d, ne=ne, cap=cap, final=final),
        grid=(b, n // TOK_TILE),
        in_specs=[
            pl.BlockSpec((1, TOK_TILE, d), lambda bi, i: (bi, i + ot, 0)),
            pl.BlockSpec((1, 1, N_MOD * d), lambda bi, i: (nb if mod_row_ctx else bi, 0, 0)),
            pl.BlockSpec((ne, cap, d), lambda bi, i: (0, bi, 0)),
            pl.BlockSpec((1, TOK_TILE, ne), lambda bi, i: (bi, i, 0)),
            pl.BlockSpec((1, TOK_TILE, ne), lambda bi, i: (bi, i + ot, 0)),
            pl.BlockSpec((1, d), lambda bi, i: (0, 0)),
        ],
        out_specs=out_spec,
        out_shape=out_shape,
        input_output_aliases=aliases,
        compiler_params=_params(("parallel", "parallel")),
        name="expert_scatter_residual",
    )(x1, mod_l, ys, pos_n, aff_n, final_g)


def _rope_tables(ctx_len, seq, hd):
    rows = seq // GRID_W
    row = jnp.repeat(jnp.arange(rows), GRID_W).astype(F32)
    col = jnp.tile(jnp.arange(GRID_W), rows).astype(F32)
    inv = ROPE_THETA ** (-jnp.arange(0, hd // 2, 2, dtype=F32) / (hd // 2))
    ang = jnp.concatenate([row[:, None] * inv, col[:, None] * inv], axis=-1)
    cos = jnp.repeat(jnp.cos(ang), 2, axis=-1)
    sin = jnp.repeat(jnp.sin(ang), 2, axis=-1) * jnp.tile(jnp.array([-1.0, 1.0], F32), hd // 2)
    cos = jnp.concatenate([jnp.ones((ctx_len, hd), F32), cos], axis=0)
    sin = jnp.concatenate([jnp.zeros((ctx_len, hd), F32), sin], axis=0)
    rep = LANES // hd
    return jnp.tile(cos, (1, rep)), jnp.tile(sin, (1, rep))


def kernel(x, c, ctx, c_ctx, ada_w, ada_b, norm1_g, norm2_g, w_in, hg_lb_logits, hg_norm_g, conv_w, q_norm_g, k_norm_g, w_proj_a, w_proj_b, w_proj_c, w_out, router_w, w_gate, w_up, w_down, final_norm_g):
    b, s, d = x.shape
    ctx_len = ctx.shape[1]
    t = ctx_len + s
    depth = w_in.shape[0]
    hgw = hg_norm_g.shape[1]
    cw = conv_w.shape[2]
    aw = w_proj_c.shape[1]
    hd = q_norm_g.shape[1]
    kvw = (w_in.shape[2] - 5 * hgw - 3 * cw - aw - 3 * d) // 2
    ne = router_w.shape[2]
    nct = ctx_len // TOK_TILE
    assert ctx_len % TOK_TILE == 0 and s % TOK_TILE == 0 and LANES % hd == 0 and cw == d
    dims = dict(hgw=hgw, cw=cw, aw=aw, kvw=kvw, hd=hd, nct=nct)

    xc = jnp.concatenate([ctx, x], axis=1)
    n_rows = -(-(b + 1) // 8) * 8
    cvec = jnp.concatenate([c, c_ctx[None, :], jnp.zeros((n_rows - b - 1, d), F32)], axis=0)
    mod = _modulation(cvec, ada_w, ada_b)
    mod = mod[:, :b + 1].reshape(depth, b + 1, 1, N_MOD * d)

    cos_t, sin_t = _rope_tables(ctx_len, s, hd)
    lane = jnp.arange(LANES)
    gmat = (lane[:, None] // hd == lane[None, :] // hd).astype(BF16)
    tri = (lane[:, None] <= lane[None, :]).astype(BF16)
    rep = LANES // hd
    lb_logits = hg_lb_logits.astype(F32)

    out = None
    for l in range(depth):
        last = l == depth - 1
        mod_l = mod[l]
        (hq, lff, lfb, kkf, kkb, hi, hog, cb, cu, aq, ak, av, ga, gb, gc) = _in_projection(
            xc, mod_l, norm1_g[l][None, :], w_in[l].astype(BF16), cos_t, sin_t,
            jnp.tile(q_norm_g[l], rep)[None, :], jnp.tile(k_norm_g[l], rep)[None, :], gmat, lb_logits, l, dims)
        a = _hgrn(hq, lff, lfb, kkf, kkb, hi, hog, hg_norm_g[l][None, :], ctx_len)
        att = _attention(aq, ak, av, ctx_len)
        x1, h2, aff_t = _merge(xc, mod_l, a, cb, cu, att, ga, gb, gc, conv_w[l],
                               w_proj_a[l].astype(BF16), w_proj_b[l].astype(BF16), w_proj_c[l].astype(BF16),
                               w_out[l].astype(BF16), norm2_g[l][None, :], router_w[l].T, nct)
        aff_n = jnp.swapaxes(aff_t, 1, 2)
        wg, wu, wd = w_gate[l].astype(BF16), w_up[l].astype(BF16), w_down[l].astype(BF16)

        def moe(stream, off, n, mod_row_ctx, final):
            cap = CAPACITY_FACTOR * n // ne
            pos = _topk_positions(aff_t, tri, off, n, cap)
            xs = _gather_tokens(h2, pos, off, n, cap)
            ys = _expert_ffn(xs, wg, wu, wd)
            return _scatter_residual(stream, mod_l, ys, jnp.swapaxes(pos, 1, 2), aff_n,
                                     final_norm_g[None, :], off, n, cap, mod_row_ctx, final)

        if last:
            out = moe(x1, ctx_len, s, False, True)
        else:
            xc = moe(x1, ctx_len, s, False, False)
            xc = moe(xc, 0, ctx_len, True, False)
    return out
```

```python
import functools

import jax
import jax.numpy as jnp
from jax import lax
from jax.experimental import pallas as pl
from jax.experimental.pallas import tpu as pltpu

F32 = jnp.float32
BF16 = jnp.bfloat16

EPS = 1e-6
N_MOD = 6
HG_DK = 128
GRID_W = 64
ROPE_THETA = 10000.0
CAPACITY_FACTOR = 2

LANES = 128
TOK_TILE = 256
HG_CHUNK = 64
HG_SUB = 32
HG_EXP_CLAMP = 80.0
HG_HEADS_PER_STEP = 2
KV_TILE = 256
HALO = 16
ATTN_GROUPS_PER_STEP = 2
LOG2E = 1.4426950408889634
VMEM_LIMIT = 56 * 1024 * 1024

HIGHEST = lax.Precision.HIGHEST


def _dot(a, b, precision=None):
    return jnp.dot(a, b, preferred_element_type=F32, precision=precision)


def _dot_nt(a, b, precision=None):
    return lax.dot_general(a, b, (((1,), (1,)), ((), ())), preferred_element_type=F32, precision=precision)


def _dot_tn(a, b):
    return lax.dot_general(a, b, (((0,), (0,)), ((), ())), preferred_element_type=F32)


def _sigmoid(x):
    return 1.0 / (1.0 + jnp.exp(-x))


def _params(sem, vmem=VMEM_LIMIT):
    return pltpu.CompilerParams(dimension_semantics=sem, vmem_limit_bytes=vmem)


def _mod_kernel(c_ref, w_ref, b_ref, o_ref):
    c = c_ref[...]
    sc = c * _sigmoid(c)
    o_ref[0] = _dot(sc, w_ref[0], precision=HIGHEST) + b_ref[0]


def _modulation(cvec, ada_w, ada_b):
    depth, d, n = ada_w.shape
    r = cvec.shape[0]
    tn = d
    return pl.pallas_call(
        _mod_kernel,
        grid=(depth, n // tn),
        in_specs=[
            pl.BlockSpec((r, d), lambda l, j: (0, 0)),
            pl.BlockSpec((1, d, tn), lambda l, j: (l, 0, j)),
            pl.BlockSpec((1, 1, tn), lambda l, j: (l, 0, j)),
        ],
        out_specs=pl.BlockSpec((1, r, tn), lambda l, j: (l, 0, j)),
        out_shape=jax.ShapeDtypeStruct((depth, r, n), F32),
        compiler_params=_params(("parallel", "parallel")),
        name="adaln_modulation",
    )(cvec, ada_w, ada_b.reshape(depth, 1, n))


def _modulated_norm(x, g, shift, scale):
    ms = jnp.mean(x * x, axis=-1, keepdims=True)
    return (x * lax.rsqrt(ms + EPS) * g) * (1.0 + scale) + shift


def _headnorm_rope(p, gain, gmat, cos, sin, hd, post_scale):
    rows, width = p.shape
    lane = lax.broadcasted_iota(jnp.int32, (rows, LANES), 1)
    even = (lane % 2) == 0
    outs = []
    for cb in range(width // LANES):
        xb = p[:, cb * LANES:(cb + 1) * LANES]
        sq = xb * xb
        hi = sq.astype(BF16)
        lo = (sq - hi.astype(F32)).astype(BF16)
        ss = _dot(hi, gmat) + _dot(lo, gmat)
        y = xb * lax.rsqrt(ss * (1.0 / hd) + EPS) * gain
        y_next = pltpu.roll(y, LANES - 1, axis=1)
        y_prev = pltpu.roll(y, 1, axis=1)
        ysw = jnp.where(even, y_next, y_prev)
        outs.append((y * cos + ysw * sin) * post_scale)
    return jnp.concatenate(outs, axis=1) if len(outs) > 1 else outs[0]


def _lower_bounds(lg, layer):
    e = jnp.exp(lg - jnp.max(lg, axis=0, keepdims=True))
    sm = e / jnp.sum(e, axis=0, keepdims=True)
    lb = jnp.zeros(lg.shape[1:], F32)
    for j in range(1, layer + 1):
        lb = lb + sm[j]
    return lb


def _forget_gate(z, lbd):
    sp = jnp.log(1.0 + jnp.exp(-jnp.abs(z)))
    ls = jnp.minimum(z, 0.0) - sp
    kk = (1.0 - lbd) * jnp.exp(jnp.minimum(-z, 0.0) - sp)
    a = jnp.log(lbd)
    t = jnp.log(1.0 - lbd) + ls
    mx = jnp.maximum(a, t)
    mn = jnp.minimum(a, t)
    return mx + jnp.log(1.0 + jnp.exp(mn - mx)), kk


def _inproj_kernel(x_ref, mod_ref, g_ref, w_ref, cos_ref, sin_ref, qg_ref, kg_ref, gmat_ref, lbl_ref,
                   hq_ref, lff_ref, lfb_ref, kkf_ref, kkb_ref, hi_ref, hog_ref, cb_ref, cu_ref, aq_ref, ak_ref,
                   av_ref, ga_ref, gb_ref, gc_ref, *, d, hgw, cw, aw, kvw, hd, layer):
    x = x_ref[0]
    m = mod_ref[0]
    h = _modulated_norm(x, g_ref[...], m[:, 0:d], m[:, d:2 * d]).astype(BF16)

    def proj(lo, width):
        return _dot(h, w_ref[:, lo:lo + width])

    lb = _lower_bounds(lbl_ref[...], layer)
    o = 0
    hq_ref[0] = proj(o, hgw).astype(BF16); o += hgw
    lf, kk = _forget_gate(proj(o, hgw), lb[0:1, :]); o += hgw
    lff_ref[0] = _chunk_cumsum(lf, False)
    kkf_ref[0] = kk.astype(BF16)
    lf, kk = _forget_gate(proj(o, hgw), lb[1:2, :]); o += hgw
    lfb_ref[0] = _chunk_cumsum(lf, True)
    kkb_ref[0] = kk.astype(BF16)
    hi_ref[0] = proj(o, hgw).astype(BF16); o += hgw
    g = proj(o, hgw); o += hgw
    hog_ref[0] = (g * _sigmoid(g)).astype(BF16)
    cb_ref[0] = proj(o, cw).astype(BF16); o += cw
    cc = proj(o, cw); o += cw
    cx = proj(o, cw); o += cw
    cu_ref[0] = (cc * cx).astype(BF16)
    cos = cos_ref[...]
    sin = sin_ref[...]
    gmat = gmat_ref[...]
    q = proj(o, aw); o += aw
    aq_ref[0] = _headnorm_rope(q, qg_ref[...], gmat, cos, sin, hd, hd ** -0.5 * LOG2E).astype(BF16)
    k = proj(o, kvw); o += kvw
    kn = _headnorm_rope(k, kg_ref[...], gmat, cos, sin, hd, 1.0).astype(BF16)
    vv = proj(o, kvw).astype(BF16); o += kvw
    ones_col = jnp.where(lax.broadcasted_iota(jnp.int32, (vv.shape[0], hd), 1) == 0, 1.0, 0.0).astype(BF16)
    for g in range(kvw // hd):
        ak_ref[0, g] = kn[:, g * hd:(g + 1) * hd]
        av_ref[0, g] = jnp.concatenate([vv[:, g * hd:(g + 1) * hd], ones_col], axis=1)
    ga_ref[0] = _sigmoid(proj(o, d)).astype(BF16); o += d
    gb_ref[0] = _sigmoid(proj(o, d)).astype(BF16); o += d
    gc_ref[0] = _sigmoid(proj(o, d)).astype(BF16); o += d


def _in_projection(xc, mod_l, norm_g, w_bf, cos_t, sin_t, qg, kg, gmat, lb_logits, layer, dims):
    b, t, d = xc.shape
    hgw, cw, aw, kvw, hd, nct = dims["hgw"], dims["cw"], dims["aw"], dims["kvw"], dims["hd"], dims["nct"]
    nt = t // TOK_TILE
    in_w = w_bf.shape[1]
    nb = mod_l.shape[0] - 1

    def tok(width):
        return pl.BlockSpec((1, TOK_TILE, width), lambda bi, i: (bi, i, 0))

    def const(shape):
        return pl.BlockSpec(shape, lambda bi, i: (0,) * len(shape))

    widths = [(hgw, BF16), (hgw, F32), (hgw, F32), (hgw, BF16), (hgw, BF16), (hgw, BF16), (hgw, BF16),
              (cw, BF16), (cw, BF16), (aw, BF16), (kvw, BF16), (kvw, BF16), (d, BF16), (d, BF16), (d, BF16)]
    ng = kvw // hd
    out_specs = [tok(w) for w, _ in widths]
    out_shape = [jax.ShapeDtypeStruct((b, t, w), dt) for w, dt in widths]
    for idx, wid in ((10, hd), (11, 2 * hd)):
        out_specs[idx] = pl.BlockSpec((1, ng, TOK_TILE, wid), lambda bi, i: (bi, 0, i, 0))
        out_shape[idx] = jax.ShapeDtypeStruct((b, ng, t, wid), BF16)
    return pl.pallas_call(
        functools.partial(_inproj_kernel, d=d, hgw=hgw, cw=cw, aw=aw, kvw=kvw, hd=hd, layer=layer),
        grid=(b, nt),
        in_specs=[
            tok(d),
            pl.BlockSpec((1, 1, N_MOD * d), lambda bi, i: (jnp.where(i < nct, nb, bi), 0, 0)),
            const((1, d)),
            pl.BlockSpec((d, in_w), lambda bi, i: (0, 0), pipeline_mode=pl.Buffered(1)),
            pl.BlockSpec((TOK_TILE, LANES), lambda bi, i: (i, 0)),
            pl.BlockSpec((TOK_TILE, LANES), lambda bi, i: (i, 0)),
            const((1, LANES)),
            const((1, LANES)),
            const((LANES, LANES)),
            const(lb_logits.shape),
        ],
        out_specs=out_specs,
        out_shape=out_shape,
        compiler_params=_params(("parallel", "parallel")),
        name="in_projection",
    )(xc, mod_l, norm_g, w_bf, cos_t, sin_t, qg, kg, gmat, lb_logits)


def _chunk_cumsum(x, reverse):
    n = x.shape[0]
    pos = lax.broadcasted_iota(jnp.int32, x.shape, 0) % HG_CHUNK
    s = 1
    while s < HG_CHUNK:
        if reverse:
            x = x + jnp.where(pos < HG_CHUNK - s, pltpu.roll(x, n - s, axis=0), 0.0)
        else:
            x = x + jnp.where(pos >= s, pltpu.roll(x, s, axis=0), 0.0)
        s *= 2
    return x


def _hgrn_chunk(q, cs, kk, v, st_ref, reverse):
    c = q.shape[0]
    nsb = c // HG_SUB
    anchors = []
    for i in range(nsb):
        r = i * HG_SUB + (HG_SUB // 2 if reverse else HG_SUB // 2 - 1)
        anchors.append(cs[r:r + 1, :])
    c_anchor = jnp.concatenate([jnp.broadcast_to(a, (HG_SUB, a.shape[1])) for a in anchors], axis=0)
    c_end = cs[0:1, :] if reverse else cs[c - 1:c, :]
    qh = q * jnp.exp(cs - c_anchor)
    zero = jnp.zeros((HG_SUB, q.shape[1]), F32)
    q_ext = jnp.concatenate(
        [jnp.concatenate([qh[i * HG_SUB:(i + 1) * HG_SUB] if j == i else zero for j in range(nsb)], axis=0)
         for i in range(nsb)], axis=1)
    k_ext = jnp.concatenate([kk * jnp.exp(jnp.minimum(a - cs, HG_EXP_CLAMP)) for a in anchors], axis=1)
    a = _dot_nt(q_ext.astype(BF16), k_ext.astype(BF16))
    row = lax.broadcasted_iota(jnp.int32, (c, c), 0)
    col = lax.broadcasted_iota(jnp.int32, (c, c), 1)
    keep = (col >= row) if reverse else (col <= row)
    st = st_ref[...]
    o = (_dot(jnp.where(keep, a, 0.0).astype(BF16), v)
         + _dot_nt((qh * jnp.exp(c_anchor)).astype(BF16), st.astype(BF16)))
    kh = (kk * jnp.exp(c_end - cs)).astype(BF16)
    st_ref[...] = st * jnp.exp(c_end) + _dot_tn(v, kh)
    return o


def _hgrn_kernel(hq_ref, csf_ref, csb_ref, kkf_ref, kkb_ref, hi_ref, hog_ref, ng_ref, a_ref,
                 of_ref, ob_ref, stf_ref, stb_ref, *, n_chunks, n_ctx_chunks, n_heads):
    stf_ref[...] = jnp.zeros_like(stf_ref)
    stb_ref[...] = jnp.zeros_like(stb_ref)
    c = HG_CHUNK

    def step(s, carry):
        rf = pl.multiple_of(s * c, c)
        cbk = jnp.where(s < n_ctx_chunks, n_ctx_chunks - 1 - s, n_chunks - 1 - (s - n_ctx_chunks))
        rb = pl.multiple_of(cbk * c, c)
        for h in range(n_heads):
            ln = slice(h * HG_DK, (h + 1) * HG_DK)
            of_ref[pl.ds(rf, c), ln] = _hgrn_chunk(
                hq_ref[0, pl.ds(rf, c), ln].astype(F32), csf_ref[0, pl.ds(rf, c), ln],
                kkf_ref[0, pl.ds(rf, c), ln].astype(F32), hi_ref[0, pl.ds(rf, c), ln], stf_ref.at[h], False)
            ob_ref[pl.ds(rb, c), ln] = _hgrn_chunk(
                hq_ref[0, pl.ds(rb, c), ln].astype(F32), csb_ref[0, pl.ds(rb, c), ln],
                kkb_ref[0, pl.ds(rb, c), ln].astype(F32), hi_ref[0, pl.ds(rb, c), ln], stb_ref.at[h], True)
        return carry

    lax.fori_loop(0, n_chunks, step, 0)

    def readout(i, carry):
        r0 = pl.multiple_of(i * TOK_TILE, TOK_TILE)
        for h in range(n_heads):
            ln = slice(h * HG_DK, (h + 1) * HG_DK)
            o = of_ref[pl.ds(r0, TOK_TILE), ln] + ob_ref[pl.ds(r0, TOK_TILE), ln]
            ms = jnp.mean(o * o, axis=-1, keepdims=True)
            y = o * lax.rsqrt(ms + EPS) * ng_ref[:, ln]
            a_ref[0, pl.ds(r0, TOK_TILE), ln] = (y * hog_ref[0, pl.ds(r0, TOK_TILE), ln].astype(F32)).astype(BF16)
        return carry

    lax.fori_loop(0, (n_chunks * c) // TOK_TILE, readout, 0)


def _hgrn(hq, csf, csb, kkf, kkb, hi, hog, norm_g, ctx_len):
    b, t, hgw = hq.shape
    nh = HG_HEADS_PER_STEP
    wid = nh * HG_DK

    def seq():
        return pl.BlockSpec((1, t, wid), lambda bi, h: (bi, 0, h))

    return pl.pallas_call(
        functools.partial(_hgrn_kernel, n_chunks=t // HG_CHUNK, n_ctx_chunks=ctx_len // HG_CHUNK, n_heads=nh),
        grid=(b, hgw // wid),
        in_specs=[seq(), seq(), seq(), seq(), seq(), seq(), seq(),
                  pl.BlockSpec((1, wid), lambda bi, h: (0, h))],
        out_specs=seq(),
        out_shape=jax.ShapeDtypeStruct((b, t, hgw), BF16),
        scratch_shapes=[pltpu.VMEM((t, wid), F32), pltpu.VMEM((t, wid), F32),
                        pltpu.VMEM((nh, HG_DK, HG_DK), F32), pltpu.VMEM((nh, HG_DK, HG_DK), F32)],
        compiler_params=_params(("parallel", "parallel")),
        name="hgrn2_bidirectional",
    )(hq, csf, csb, kkf, kkb, hi, hog, norm_g)


def _attn_kernel(q_ref, k_ref, v_ref, o_ref, *, n_g, group, hd, nct, ctx_len, t_all):
    i = pl.program_id(2)

    def run(n_keys):
        outs = []
        for g in range(n_g):
            k = k_ref[0, g, 0:n_keys, :]
            v = v_ref[0, g, 0:n_keys, :]
            for h in range(group):
                c0 = (g * group + h) * hd
                s = _dot_nt(q_ref[0, :, c0:c0 + hd], k)
                p = jnp.exp2(s - jnp.max(s, axis=1, keepdims=True))
                ov = _dot(p.astype(BF16), v)
                outs.append((ov[:, 0:hd] / ov[:, hd:hd + 1]).astype(BF16))
        o_ref[0] = jnp.concatenate(outs, axis=1)

    @pl.when(i < nct)
    def _():
        run(ctx_len)

    @pl.when(i >= nct)
    def _():
        run(t_all)


def _attention(aq, ak, av, ctx_len):
    b, t, aw = aq.shape
    n_groups, hd = ak.shape[1], ak.shape[3]
    group = aw // (n_groups * hd)
    nt = t // TOK_TILE
    n_g = ATTN_GROUPS_PER_STEP
    return pl.pallas_call(
        functools.partial(_attn_kernel, n_g=n_g, group=group, hd=hd, nct=ctx_len // TOK_TILE, ctx_len=ctx_len,
                          t_all=t),
        grid=(b, n_groups // n_g, nt),
        in_specs=[
            pl.BlockSpec((1, TOK_TILE, n_g * group * hd), lambda bi, g, i: (bi, i, g)),
            pl.BlockSpec((1, n_g, t, hd), lambda bi, g, i: (bi, g, 0, 0)),
            pl.BlockSpec((1, n_g, t, 2 * hd), lambda bi, g, i: (bi, g, 0, 0)),
        ],
        out_specs=pl.BlockSpec((1, TOK_TILE, n_g * group * hd), lambda bi, g, i: (bi, i, g)),
        out_shape=jax.ShapeDtypeStruct((b, t, aw), BF16),
        compiler_params=_params(("parallel", "parallel", "parallel")),
        name="gqa_attention",
    )(aq, ak, av)


def _merge_kernel(x_ref, mod_ref, a_ref, cb_ref, cu_ref, cup_ref, cun_ref, att_ref, ga_ref, gb_ref, gc_ref,
                  cw_ref, wa_ref, wb_ref, wc_ref, wo_ref, g2_ref, rwt_ref,
                  x1_ref, h2_ref, aff_ref, *, d, nct, nt):
    i = pl.program_id(1)
    m = mod_ref[0]
    rows = x_ref.shape[1]
    u = cu_ref[0].astype(F32)
    row = lax.broadcasted_iota(jnp.int32, u.shape, 0)
    has_prev = jnp.logical_and(i != 0, i != nct)
    has_next = jnp.logical_and(i != nct - 1, i != nt - 1)
    prev_row = jnp.where(has_prev, cup_ref[0, HALO - 1:HALO, :].astype(F32), 0.0)
    next_row = jnp.where(has_next, cun_ref[0, 0:1, :].astype(F32), 0.0)
    u_prev = jnp.where(row == 0, prev_row, pltpu.roll(u, 1, axis=0))
    u_next = jnp.where(row == rows - 1, next_row, pltpu.roll(u, rows - 1, axis=0))
    cw = cw_ref[...]
    conv = cw[0:1, :] * u_prev + cw[1:2, :] * u + cw[2:3, :] * u_next
    bb = (cb_ref[0].astype(F32) * conv).astype(BF16)
    y = (ga_ref[0].astype(F32) * _dot(a_ref[0], wa_ref[...])
         + gb_ref[0].astype(F32) * _dot(bb, wb_ref[...])
         + gc_ref[0].astype(F32) * _dot(att_ref[0], wc_ref[...]))
    x1 = x_ref[0] + m[:, 2 * d:3 * d] * _dot(y.astype(BF16), wo_ref[...])
    x1_ref[0] = x1
    h2 = _modulated_norm(x1, g2_ref[...], m[:, 3 * d:4 * d], m[:, 4 * d:5 * d])
    h2_ref[0] = h2.astype(BF16)
    logits = _dot_nt(rwt_ref[...], h2, precision=HIGHEST)
    ex = jnp.exp(logits - jnp.max(logits, axis=0, keepdims=True))
    aff_ref[0] = ex / jnp.sum(ex, axis=0, keepdims=True)


def _merge(xc, mod_l, a, cb, cu, att, ga, gb, gc, conv_w, wa, wb, wc, wo, norm2_g, rwt, nct):
    b, t, d = xc.shape
    nt = t // TOK_TILE
    ne = rwt.shape[0]
    nb = mod_l.shape[0] - 1
    sub = TOK_TILE // HALO
    n8 = t // HALO

    def tok(width):
        return pl.BlockSpec((1, TOK_TILE, width), lambda bi, i: (bi, i, 0))

    def const(shape):
        return pl.BlockSpec(shape, lambda bi, i: (0,) * len(shape))

    cwid = cu.shape[2]
    return pl.pallas_call(
        functools.partial(_merge_kernel, d=d, nct=nct, nt=nt),
        grid=(b, nt),
        in_specs=[
            tok(d),
            pl.BlockSpec((1, 1, N_MOD * d), lambda bi, i: (jnp.where(i < nct, nb, bi), 0, 0)),
            tok(a.shape[2]), tok(cwid), tok(cwid),
            pl.BlockSpec((1, HALO, cwid), lambda bi, i: (bi, jnp.maximum(i * sub - 1, 0), 0)),
            pl.BlockSpec((1, HALO, cwid), lambda bi, i: (bi, jnp.minimum((i + 1) * sub, n8 - 1), 0)),
            tok(att.shape[2]), tok(d), tok(d), tok(d),
            const(conv_w.shape), const(wa.shape), const(wb.shape), const(wc.shape), const(wo.shape),
            const((1, d)), const(rwt.shape),
        ],
        out_specs=[tok(d), tok(d), pl.BlockSpec((1, ne, TOK_TILE), lambda bi, i: (bi, 0, i))],
        out_shape=[jax.ShapeDtypeStruct((b, t, d), F32), jax.ShapeDtypeStruct((b, t, d), BF16),
                   jax.ShapeDtypeStruct((b, ne, t), F32)],
        compiler_params=_params(("parallel", "parallel")),
        name="merge_residual_router",
    )(xc, mod_l, a, cb, cu, cu, cu, att, ga, gb, gc, conv_w, wa, wb, wc, wo, norm2_g, rwt)


def _prefix_count(mask, tri):
    e, n = mask.shape
    carry = jnp.zeros((e, 1), F32)
    outs = []
    for blk in range(n // LANES):
        xb = jnp.where(mask[:, blk * LANES:(blk + 1) * LANES], 1.0, 0.0).astype(BF16)
        pre = _dot(xb, tri) + carry
        carry = pre[:, LANES - 1:LANES]
        outs.append(pre)
    return jnp.concatenate(outs, axis=1) if len(outs) > 1 else outs[0]


def _topk_kernel(aff_ref, tri_ref, pos_ref, *, off, n, cap):
    a = aff_ref[0][:, off:off + n]
    bits = pltpu.bitcast(a, jnp.int32)
    thr = jnp.zeros((a.shape[0], 1), jnp.int32)
    for bit in range(30, -1, -1):
        cand = thr | (1 << bit)
        cnt = jnp.sum(jnp.where(bits >= cand, 1.0, 0.0), axis=1, keepdims=True)
        thr = jnp.where(cnt >= cap, cand, thr)
    gt = bits > thr
    eq = bits == thr
    need = cap - jnp.sum(jnp.where(gt, 1.0, 0.0), axis=1, keepdims=True)
    tri = tri_ref[...]
    eq_rank = _prefix_count(eq, tri)
    sel = jnp.logical_or(gt, jnp.logical_and(eq, eq_rank <= need))
    slot = _prefix_count(sel, tri) - 1.0
    pos_ref[0] = jnp.where(sel, slot, -1.0).astype(jnp.int32)


def _topk_positions(aff_t, tri, off, n, cap):
    b, ne, t = aff_t.shape
    return pl.pallas_call(
        functools.partial(_topk_kernel, off=off, n=n, cap=cap),
        grid=(b,),
        in_specs=[pl.BlockSpec((1, ne, t), lambda bi: (bi, 0, 0)), pl.BlockSpec((LANES, LANES), lambda bi: (0, 0))],
        out_specs=pl.BlockSpec((1, ne, n), lambda bi: (bi, 0, 0)),
        out_shape=jax.ShapeDtypeStruct((b, ne, n), jnp.int32),
        compiler_params=_params(("parallel",)),
        name="expert_choice_topk",
    )(aff_t, tri)


def _gather_kernel(h_ref, pos_ref, xs_ref, *, off, n, cap, kt):
    pos = pos_ref[0, 0]
    acc = jnp.zeros(xs_ref.shape[1:], F32)
    for c0 in range(0, n, kt):
        slot = lax.broadcasted_iota(jnp.int32, (cap, kt), 0)
        onehot = jnp.where(pos[:, c0:c0 + kt] == slot, 1.0, 0.0).astype(BF16)
        acc = acc + _dot(onehot, h_ref[0, off + c0:off + c0 + kt, :])
    xs_ref[0] = acc.astype(BF16)


def _gather_tokens(h2, pos, off, n, cap):
    b, t, d = h2.shape
    ne = pos.shape[1]
    kt = min(n, 512)
    return pl.pallas_call(
        functools.partial(_gather_kernel, off=off, n=n, cap=cap, kt=kt),
        grid=(b, ne),
        in_specs=[pl.BlockSpec((1, t, d), lambda bi, e: (bi, 0, 0)),
                  pl.BlockSpec((1, 1, 1, n), lambda bi, e: (bi, e, 0, 0))],
        out_specs=pl.BlockSpec((1, cap, d), lambda bi, e: (e, bi, 0)),
        out_shape=jax.ShapeDtypeStruct((ne, b * cap, d), BF16),
        compiler_params=_params(("parallel", "parallel")),
        name="expert_gather",
    )(h2, pos.reshape(b, ne, 1, n))


def _ffn_kernel(xs_ref, wg_ref, wu_ref, wd_ref, ys_ref):
    x = xs_ref[0]
    g = _dot(x, wg_ref[0])
    hid = (g * _sigmoid(g)) * _dot(x, wu_ref[0])
    ys_ref[0] = _dot(hid.astype(BF16), wd_ref[0]).astype(BF16)


def _expert_ffn(xs, wg, wu, wd):
    ne, m, d = xs.shape
    ff = wg.shape[2]
    tm = min(m, 512)
    return pl.pallas_call(
        _ffn_kernel,
        grid=(ne, m // tm),
        in_specs=[pl.BlockSpec((1, tm, d), lambda e, j: (e, j, 0)),
                  pl.BlockSpec((1, d, ff), lambda e, j: (e, 0, 0)),
                  pl.BlockSpec((1, d, ff), lambda e, j: (e, 0, 0)),
                  pl.BlockSpec((1, ff, d), lambda e, j: (e, 0, 0))],
        out_specs=pl.BlockSpec((1, tm, d), lambda e, j: (e, j, 0)),
        out_shape=jax.ShapeDtypeStruct((ne, m, d), BF16),
        compiler_params=_params(("parallel", "parallel")),
        name="expert_ffn",
    )(xs, wg, wu, wd)


def _scatter_kernel(x_ref, mod_ref, ys_ref, pos_ref, aff_ref, fg_ref, o_ref, *, d, ne, cap, final):
    m = mod_ref[0]
    pos = pos_ref[0]
    aff = aff_ref[0]
    rows = pos.shape[0]
    slot = lax.broadcasted_iota(jnp.int32, (rows, cap), 1)
    acc = jnp.zeros((rows, d), F32)
    for e in range(ne):
        onehot = jnp.where(pos[:, e:e + 1] == slot, 1.0, 0.0).astype(BF16)
        acc = acc + aff[:, e:e + 1] * _dot(onehot, ys_ref[e])
    x2 = x_ref[0] + m[:, 5 * d:6 * d] * acc
    if final:
        ms = jnp.mean(x2 * x2, axis=-1, keepdims=True)
        x2 = x2 * lax.rsqrt(ms + EPS) * fg_ref[...]
    o_ref[0] = x2


def _scatter_residual(x1, mod_l, ys, pos_n, aff_n, final_g, off, n, cap, mod_row_ctx, final):
    b, t, d = x1.shape
    ne = ys.shape[0]
    nb = mod_l.shape[0] - 1
    ot = off // TOK_TILE
    if final:
        out_shape = jax.ShapeDtypeStruct((b, n, d), F32)
        out_spec = pl.BlockSpec((1, TOK_TILE, d), lambda bi, i: (bi, i, 0))
        aliases = {}
    else:
        out_shape = jax.ShapeDtypeStruct((b, t, d), F32)
        out_spec = pl.BlockSpec((1, TOK_TILE, d), lambda bi, i: (bi, i + ot, 0))
        aliases = {0: 0}
    return pl.pallas_call(
        functools.partial(_scatter_kernel, d=d, ne=ne, cap=cap, final=final),
        grid=(b, n // TOK_TILE),
        in_specs=[
            pl.BlockSpec((1, TOK_TILE, d), lambda bi, i: (bi, i + ot, 0)),
            pl.BlockSpec((1, 1, N_MOD * d), lambda bi, i: (nb if mod_row_ctx else bi, 0, 0)),
            pl.BlockSpec((ne, cap, d), lambda bi, i: (0, bi, 0)),
            pl.BlockSpec((1, TOK_TILE, ne), lambda bi, i: (bi, i, 0)),
            pl.BlockSpec((1, TOK_TILE, ne), lambda bi, i: (bi, i + ot, 0)),
            pl.BlockSpec((1, d), lambda bi, i: (0, 0)),
        ],
        out_specs=out_spec,
        out_shape=out_shape,
        input_output_aliases=aliases,
        compiler_params=_params(("parallel", "parallel")),
        name="expert_scatter_residual",
    )(x1, mod_l, ys, pos_n, aff_n, final_g)


def _rope_tables(ctx_len, seq, hd):
    rows = seq // GRID_W
    row = jnp.repeat(jnp.arange(rows), GRID_W).astype(F32)
    col = jnp.tile(jnp.arange(GRID_W), rows).astype(F32)
    inv = ROPE_THETA ** (-jnp.arange(0, hd // 2, 2, dtype=F32) / (hd // 2))
    ang = jnp.concatenate([row[:, None] * inv, col[:, None] * inv], axis=-1)
    cos = jnp.repeat(jnp.cos(ang), 2, axis=-1)
    sin = jnp.repeat(jnp.sin(ang), 2, axis=-1) * jnp.tile(jnp.array([-1.0, 1.0], F32), hd // 2)
    cos = jnp.concatenate([jnp.ones((ctx_len, hd), F32), cos], axis=0)
    sin = jnp.concatenate([jnp.zeros((ctx_len, hd), F32), sin], axis=0)
    rep = LANES // hd
    return jnp.tile(cos, (1, rep)), jnp.tile(sin, (1, rep))


def kernel(x, c, ctx, c_ctx, ada_w, ada_b, norm1_g, norm2_g, w_in, hg_lb_logits, hg_norm_g, conv_w, q_norm_g, k_norm_g, w_proj_a, w_proj_b, w_proj_c, w_out, router_w, w_gate, w_up, w_down, final_norm_g):
    b, s, d = x.shape
    ctx_len = ctx.shape[1]
    t = ctx_len + s
    depth = w_in.shape[0]
    hgw = hg_norm_g.shape[1]
    cw = conv_w.shape[2]
    aw = w_proj_c.shape[1]
    hd = q_norm_g.shape[1]
    kvw = (w_in.shape[2] - 5 * hgw - 3 * cw - aw - 3 * d) // 2
    ne = router_w.shape[2]
    nct = ctx_len // TOK_TILE
    assert ctx_len % TOK_TILE == 0 and s % TOK_TILE == 0 and LANES % hd == 0 and cw == d
    dims = dict(hgw=hgw, cw=cw, aw=aw, kvw=kvw, hd=hd, nct=nct)

    xc = jnp.concatenate([ctx, x], axis=1)
    n_rows = -(-(b + 1) // 8) * 8
    cvec = jnp.concatenate([c, c_ctx[None, :], jnp.zeros((n_rows - b - 1, d), F32)], axis=0)
    mod = _modulation(cvec, ada_w, ada_b)
    mod = mod[:, :b + 1].reshape(depth, b + 1, 1, N_MOD * d)

    cos_t, sin_t = _rope_tables(ctx_len, s, hd)
    lane = jnp.arange(LANES)
    gmat = (lane[:, None] // hd == lane[None, :] // hd).astype(BF16)
    tri = (lane[:, None] <= lane[None, :]).astype(BF16)
    rep = LANES // hd
    lb_logits = hg_lb_logits.astype(F32)

    out = None
    for l in range(depth):
        last = l == depth - 1
        mod_l = mod[l]
        (hq, lff, lfb, kkf, kkb, hi, hog, cb, cu, aq, ak, av, ga, gb, gc) = _in_projection(
            xc, mod_l, norm1_g[l][None, :], w_in[l].astype(BF16), cos_t, sin_t,
            jnp.tile(q_norm_g[l], rep)[None, :], jnp.tile(k_norm_g[l], rep)[None, :], gmat, lb_logits, l, dims)
        a = _hgrn(hq, lff, lfb, kkf, kkb, hi, hog, hg_norm_g[l][None, :], ctx_len)
        att = _attention(aq, ak, av, ctx_len)
        x1, h2, aff_t = _merge(xc, mod_l, a, cb, cu, att, ga, gb, gc, conv_w[l],
                               w_proj_a[l].astype(BF16), w_proj_b[l].astype(BF16), w_proj_c[l].astype(BF16),
                               w_out[l].astype(BF16), norm2_g[l][None, :], router_w[l].T, nct)
        aff_n = jnp.swapaxes(aff_t, 1, 2)
        wg, wu, wd = w_gate[l].astype(BF16), w_up[l].astype(BF16), w_down[l].astype(BF16)

        def moe(stream, off, n, mod_row_ctx, final):
            cap = CAPACITY_FACTOR * n // ne
            pos = _topk_positions(aff_t, tri, off, n, cap)
            xs = _gather_tokens(h2, pos, off, n, cap)
            ys = _expert_ffn(xs, wg, wu, wd)
            return _scatter_residual(stream, mod_l, ys, jnp.swapaxes(pos, 1, 2), aff_n,
                                     final_norm_g[None, :], off, n, cap, mod_row_ctx, final)

        if last:
            out = moe(x1, ctx_len, s, False, True)
        else:
            xc = moe(x1, ctx_len, s, False, False)
            xc = moe(xc, 0, ctx_len, True, False)
    return out
```

```python
import functools

import jax
import jax.numpy as jnp
from jax import lax
from jax.experimental import pallas as pl
from jax.experimental.pallas import tpu as pltpu

F32 = jnp.float32
BF16 = jnp.bfloat16

EPS = 1e-6
N_MOD = 6
HG_DK = 128
GRID_W = 64
ROPE_THETA = 10000.0
CAPACITY_FACTOR = 2

LANES = 128
TOK_TILE = 256
HG_CHUNK = 64
HG_SUB = 32
HG_EXP_CLAMP = 80.0
HG_HEADS_PER_STEP = 2
HG_UNROLL = 4
KV_TILE = 256
HALO = 16
ATTN_GROUPS_PER_STEP = 2
SLOT_WINDOW = 128
SLOT_ALIGN = 16
MERGE_SAMPLES = 2
LOG2E = 1.4426950408889634
VMEM_LIMIT = 56 * 1024 * 1024

HIGHEST = lax.Precision.HIGHEST


def _dot(a, b, precision=None):
    return jnp.dot(a, b, preferred_element_type=F32, precision=precision)


def _dot_nt(a, b, precision=None):
    return lax.dot_general(a, b, (((1,), (1,)), ((), ())), preferred_element_type=F32, precision=precision)


def _dot_tn(a, b):
    return lax.dot_general(a, b, (((0,), (0,)), ((), ())), preferred_element_type=F32)


def _sigmoid(x):
    return 1.0 / (1.0 + jnp.exp(-x))


def _params(sem, vmem=VMEM_LIMIT):
    return pltpu.CompilerParams(dimension_semantics=sem, vmem_limit_bytes=vmem)


def _mod_kernel(c_ref, w_ref, b_ref, o_ref):
    c = c_ref[...]
    sc = c * _sigmoid(c)
    o_ref[0] = _dot(sc, w_ref[0], precision=HIGHEST) + b_ref[0]


def _modulation(cvec, ada_w, ada_b):
    depth, d, n = ada_w.shape
    r = cvec.shape[0]
    tn = d
    return pl.pallas_call(
        _mod_kernel,
        grid=(depth, n // tn),
        in_specs=[
            pl.BlockSpec((r, d), lambda l, j: (0, 0)),
            pl.BlockSpec((1, d, tn), lambda l, j: (l, 0, j)),
            pl.BlockSpec((1, 1, tn), lambda l, j: (l, 0, j)),
        ],
        out_specs=pl.BlockSpec((1, r, tn), lambda l, j: (l, 0, j)),
        out_shape=jax.ShapeDtypeStruct((depth, r, n), F32),
        compiler_params=_params(("parallel", "parallel")),
        name="adaln_modulation",
    )(cvec, ada_w, ada_b.reshape(depth, 1, n))


def _modulated_norm(x, g, shift, scale):
    ms = jnp.mean(x * x, axis=-1, keepdims=True)
    return (x * lax.rsqrt(ms + EPS) * g) * (1.0 + scale) + shift


def _headnorm_rope(p, gain, gmat, cos, sin, hd, post_scale):
    rows, width = p.shape
    lane = lax.broadcasted_iota(jnp.int32, (rows, LANES), 1)
    even = (lane % 2) == 0
    outs = []
    for cb in range(width // LANES):
        xb = p[:, cb * LANES:(cb + 1) * LANES]
        sq = xb * xb
        hi = sq.astype(BF16)
        lo = (sq - hi.astype(F32)).astype(BF16)
        ss = _dot(hi, gmat) + _dot(lo, gmat)
        y = xb * lax.rsqrt(ss * (1.0 / hd) + EPS) * gain
        y_next = pltpu.roll(y, LANES - 1, axis=1)
        y_prev = pltpu.roll(y, 1, axis=1)
        ysw = jnp.where(even, y_next, y_prev)
        outs.append((y * cos + ysw * sin) * post_scale)
    return jnp.concatenate(outs, axis=1) if len(outs) > 1 else outs[0]


def _lower_bounds(lg, layer):
    e = jnp.exp(lg - jnp.max(lg, axis=0, keepdims=True))
    sm = e / jnp.sum(e, axis=0, keepdims=True)
    lb = jnp.zeros(lg.shape[1:], F32)
    for j in range(1, layer + 1):
        lb = lb + sm[j]
    return lb


def _forget_gate(z, lbd):
    sp = jnp.log(1.0 + jnp.exp(-jnp.abs(z)))
    ls = jnp.minimum(z, 0.0) - sp
    kk = (1.0 - lbd) * jnp.exp(jnp.minimum(-z, 0.0) - sp)
    a = jnp.log(lbd)
    t = jnp.log(1.0 - lbd) + ls
    mx = jnp.maximum(a, t)
    mn = jnp.minimum(a, t)
    return mx + jnp.log(1.0 + jnp.exp(mn - mx)), kk


def _inproj_kernel(x_ref, mod_ref, g_ref, w_ref, cos_ref, sin_ref, qg_ref, kg_ref, gmat_ref, lbl_ref,
                   hq_ref, lff_ref, lfb_ref, kkf_ref, kkb_ref, hi_ref, hog_ref, cb_ref, cu_ref, aq_ref, ak_ref,
                   av_ref, ga_ref, gb_ref, gc_ref, *, d, hgw, cw, aw, kvw, hd, layer):
    x = x_ref[0]
    m = mod_ref[0]
    h = _modulated_norm(x, g_ref[...], m[:, 0:d], m[:, d:2 * d]).astype(BF16)

    def proj(lo, width):
        return _dot(h, w_ref[:, lo:lo + width])

    lb = _lower_bounds(lbl_ref[...], layer)
    o = 0
    hq_ref[0] = proj(o, hgw).astype(BF16); o += hgw
    lf, kk = _forget_gate(proj(o, hgw), lb[0:1, :]); o += hgw
    lff_ref[0] = _chunk_cumsum(lf, False)
    kkf_ref[0] = kk.astype(BF16)
    lf, kk = _forget_gate(proj(o, hgw), lb[1:2, :]); o += hgw
    lfb_ref[0] = _chunk_cumsum(lf, True)
    kkb_ref[0] = kk.astype(BF16)
    hi_ref[0] = proj(o, hgw).astype(BF16); o += hgw
    g = proj(o, hgw); o += hgw
    hog_ref[0] = (g * _sigmoid(g)).astype(BF16)
    cb_ref[0] = proj(o, cw).astype(BF16); o += cw
    cc = proj(o, cw); o += cw
    cx = proj(o, cw); o += cw
    cu_ref[0] = (cc * cx).astype(BF16)
    cos = cos_ref[...]
    sin = sin_ref[...]
    gmat = gmat_ref[...]
    q = proj(o, aw); o += aw
    aq_ref[0] = _headnorm_rope(q, qg_ref[...], gmat, cos, sin, hd, hd ** -0.5 * LOG2E).astype(BF16)
    k = proj(o, kvw); o += kvw
    kn = _headnorm_rope(k, kg_ref[...], gmat, cos, sin, hd, 1.0).astype(BF16)
    vv = proj(o, kvw).astype(BF16); o += kvw
    ones_col = jnp.where(lax.broadcasted_iota(jnp.int32, (vv.shape[0], hd), 1) == 0, 1.0, 0.0).astype(BF16)
    for g in range(kvw // hd):
        ak_ref[0, g] = kn[:, g * hd:(g + 1) * hd]
        av_ref[0, g] = jnp.concatenate([vv[:, g * hd:(g + 1) * hd], ones_col], axis=1)
    ga_ref[0] = _sigmoid(proj(o, d)).astype(BF16); o += d
    gb_ref[0] = _sigmoid(proj(o, d)).astype(BF16); o += d
    gc_ref[0] = _sigmoid(proj(o, d)).astype(BF16); o += d


def _in_projection(xc, mod_l, norm_g, w_bf, cos_t, sin_t, qg, kg, gmat, lb_logits, layer, dims):
    b, t, d = xc.shape
    hgw, cw, aw, kvw, hd, nct = dims["hgw"], dims["cw"], dims["aw"], dims["kvw"], dims["hd"], dims["nct"]
    nt = t // TOK_TILE
    in_w = w_bf.shape[1]
    nb = mod_l.shape[0] - 1

    def tok(width):
        return pl.BlockSpec((1, TOK_TILE, width), lambda bi, i: (bi, i, 0))

    def const(shape):
        return pl.BlockSpec(shape, lambda bi, i: (0,) * len(shape))

    widths = [(hgw, BF16), (hgw, F32), (hgw, F32), (hgw, BF16), (hgw, BF16), (hgw, BF16), (hgw, BF16),
              (cw, BF16), (cw, BF16), (aw, BF16), (kvw, BF16), (kvw, BF16), (d, BF16), (d, BF16), (d, BF16)]
    ng = kvw // hd
    out_specs = [tok(w) for w, _ in widths]
    out_shape = [jax.ShapeDtypeStruct((b, t, w), dt) for w, dt in widths]
    for idx, wid in ((10, hd), (11, 2 * hd)):
        out_specs[idx] = pl.BlockSpec((1, ng, TOK_TILE, wid), lambda bi, i: (bi, 0, i, 0))
        out_shape[idx] = jax.ShapeDtypeStruct((b, ng, t, wid), BF16)
    return pl.pallas_call(
        functools.partial(_inproj_kernel, d=d, hgw=hgw, cw=cw, aw=aw, kvw=kvw, hd=hd, layer=layer),
        grid=(b, nt),
        in_specs=[
            tok(d),
            pl.BlockSpec((1, 1, N_MOD * d), lambda bi, i: (jnp.where(i < nct, nb, bi), 0, 0)),
            const((1, d)),
            pl.BlockSpec((d, in_w), lambda bi, i: (0, 0), pipeline_mode=pl.Buffered(1)),
            pl.BlockSpec((TOK_TILE, LANES), lambda bi, i: (i, 0)),
            pl.BlockSpec((TOK_TILE, LANES), lambda bi, i: (i, 0)),
            const((1, LANES)),
            const((1, LANES)),
            const((LANES, LANES)),
            const(lb_logits.shape),
        ],
        out_specs=out_specs,
        out_shape=out_shape,
        compiler_params=_params(("parallel", "parallel")),
        name="in_projection",
    )(xc, mod_l, norm_g, w_bf, cos_t, sin_t, qg, kg, gmat, lb_logits)


def _chunk_cumsum(x, reverse):
    n = x.shape[0]
    pos = lax.broadcasted_iota(jnp.int32, x.shape, 0) % HG_CHUNK
    s = 1
    while s < HG_CHUNK:
        if reverse:
            x = x + jnp.where(pos < HG_CHUNK - s, pltpu.roll(x, n - s, axis=0), 0.0)
        else:
            x = x + jnp.where(pos >= s, pltpu.roll(x, s, axis=0), 0.0)
        s *= 2
    return x


def _hgrn_chunk(q, cs, kk, v, st_ref, reverse):
    c = q.shape[0]
    nsb = c // HG_SUB
    anchors = []
    for i in range(nsb):
        r = i * HG_SUB + (HG_SUB // 2 if reverse else HG_SUB // 2 - 1)
        anchors.append(cs[r:r + 1, :])
    c_anchor = jnp.concatenate([jnp.broadcast_to(a, (HG_SUB, a.shape[1])) for a in anchors], axis=0)
    c_end = cs[0:1, :] if reverse else cs[c - 1:c, :]
    qh = q * jnp.exp(cs - c_anchor)
    zero = jnp.zeros((HG_SUB, q.shape[1]), F32)
    q_ext = jnp.concatenate(
        [jnp.concatenate([qh[i * HG_SUB:(i + 1) * HG_SUB] if j == i else zero for j in range(nsb)], axis=0)
         for i in range(nsb)], axis=1)
    k_ext = jnp.concatenate([kk * jnp.exp(jnp.minimum(a - cs, HG_EXP_CLAMP)) for a in anchors], axis=1)
    a = _dot_nt(q_ext.astype(BF16), k_ext.astype(BF16))
    row = lax.broadcasted_iota(jnp.int32, (c, c), 0)
    col = lax.broadcasted_iota(jnp.int32, (c, c), 1)
    keep = (col >= row) if reverse else (col <= row)
    st = st_ref[...]
    o = (_dot(jnp.where(keep, a, 0.0).astype(BF16), v)
         + _dot_nt((qh * jnp.exp(c_anchor)).astype(BF16), st.astype(BF16)))
    kh = (kk * jnp.exp(c_end - cs)).astype(BF16)
    st_ref[...] = st * jnp.exp(c_end) + _dot_tn(v, kh)
    return o


def _hgrn_kernel(hq_ref, csf_ref, csb_ref, kkf_ref, kkb_ref, hi_ref, hog_ref, ng_ref, a_ref,
                 of_ref, ob_ref, stf_ref, stb_ref, *, n_chunks, n_ctx_chunks, n_heads):
    stf_ref[...] = jnp.zeros_like(stf_ref)
    stb_ref[...] = jnp.zeros_like(stb_ref)
    c = HG_CHUNK

    def step(s, carry):
        rf = pl.multiple_of(s * c, c)
        cbk = jnp.where(s < n_ctx_chunks, n_ctx_chunks - 1 - s, n_chunks - 1 - (s - n_ctx_chunks))
        rb = pl.multiple_of(cbk * c, c)
        for h in range(n_heads):
            ln = slice(h * HG_DK, (h + 1) * HG_DK)
            of_ref[pl.ds(rf, c), ln] = _hgrn_chunk(
                hq_ref[0, pl.ds(rf, c), ln].astype(F32), csf_ref[0, pl.ds(rf, c), ln],
                kkf_ref[0, pl.ds(rf, c), ln].astype(F32), hi_ref[0, pl.ds(rf, c), ln], stf_ref.at[h], False)
            ob_ref[pl.ds(rb, c), ln] = _hgrn_chunk(
                hq_ref[0, pl.ds(rb, c), ln].astype(F32), csb_ref[0, pl.ds(rb, c), ln],
                kkb_ref[0, pl.ds(rb, c), ln].astype(F32), hi_ref[0, pl.ds(rb, c), ln], stb_ref.at[h], True)
        return carry

    lax.fori_loop(0, n_chunks, step, 0, unroll=HG_UNROLL)

    def readout(i, carry):
        r0 = pl.multiple_of(i * TOK_TILE, TOK_TILE)
        for h in range(n_heads):
            ln = slice(h * HG_DK, (h + 1) * HG_DK)
            o = of_ref[pl.ds(r0, TOK_TILE), ln] + ob_ref[pl.ds(r0, TOK_TILE), ln]
            ms = jnp.mean(o * o, axis=-1, keepdims=True)
            y = o * lax.rsqrt(ms + EPS) * ng_ref[:, ln]
            a_ref[0, pl.ds(r0, TOK_TILE), ln] = (y * hog_ref[0, pl.ds(r0, TOK_TILE), ln].astype(F32)).astype(BF16)
        return carry

    lax.fori_loop(0, (n_chunks * c) // TOK_TILE, readout, 0)


def _hgrn(hq, csf, csb, kkf, kkb, hi, hog, norm_g, ctx_len):
    b, t, hgw = hq.shape
    nh = HG_HEADS_PER_STEP
    wid = nh * HG_DK

    def seq():
        return pl.BlockSpec((1, t, wid), lambda bi, h: (bi, 0, h))

    return pl.pallas_call(
        functools.partial(_hgrn_kernel, n_chunks=t // HG_CHUNK, n_ctx_chunks=ctx_len // HG_CHUNK, n_heads=nh),
        grid=(b, hgw // wid),
        in_specs=[seq(), seq(), seq(), seq(), seq(), seq(), seq(),
                  pl.BlockSpec((1, wid), lambda bi, h: (0, h))],
        out_specs=seq(),
        out_shape=jax.ShapeDtypeStruct((b, t, hgw), BF16),
        scratch_shapes=[pltpu.VMEM((t, wid), F32), pltpu.VMEM((t, wid), F32),
                        pltpu.VMEM((nh, HG_DK, HG_DK), F32), pltpu.VMEM((nh, HG_DK, HG_DK), F32)],
        compiler_params=_params(("parallel", "parallel")),
        name="hgrn2_bidirectional",
    )(hq, csf, csb, kkf, kkb, hi, hog, norm_g)


def _attn_kernel(q_ref, k_ref, v_ref, o_ref, *, n_g, group, hd, nct, ctx_len, t_all):
    i = pl.program_id(2)

    def run(n_keys):
        outs = []
        for g in range(n_g):
            k = k_ref[0, g, 0:n_keys, :]
            v = v_ref[0, g, 0:n_keys, :]
            for h in range(group):
                c0 = (g * group + h) * hd
                s = _dot_nt(q_ref[0, :, c0:c0 + hd], k)
                p = jnp.exp2(s - jnp.max(s, axis=1, keepdims=True))
                ov = _dot(p.astype(BF16), v)
                outs.append((ov[:, 0:hd] / ov[:, hd:hd + 1]).astype(BF16))
        o_ref[0] = jnp.concatenate(outs, axis=1)

    @pl.when(i < nct)
    def _():
        run(ctx_len)

    @pl.when(i >= nct)
    def _():
        run(t_all)


def _attention(aq, ak, av, ctx_len):
    b, t, aw = aq.shape
    n_groups, hd = ak.shape[1], ak.shape[3]
    group = aw // (n_groups * hd)
    nt = t // TOK_TILE
    n_g = ATTN_GROUPS_PER_STEP
    return pl.pallas_call(
        functools.partial(_attn_kernel, n_g=n_g, group=group, hd=hd, nct=ctx_len // TOK_TILE, ctx_len=ctx_len,
                          t_all=t),
        grid=(b, n_groups // n_g, nt),
        in_specs=[
            pl.BlockSpec((1, TOK_TILE, n_g * group * hd), lambda bi, g, i: (bi, i, g)),
            pl.BlockSpec((1, n_g, t, hd), lambda bi, g, i: (bi, g, 0, 0)),
            pl.BlockSpec((1, n_g, t, 2 * hd), lambda bi, g, i: (bi, g, 0, 0)),
        ],
        out_specs=pl.BlockSpec((1, TOK_TILE, n_g * group * hd), lambda bi, g, i: (bi, i, g)),
        out_shape=jax.ShapeDtypeStruct((b, t, aw), BF16),
        compiler_params=_params(("parallel", "parallel", "parallel")),
        name="gqa_attention",
    )(aq, ak, av)


def _merge_kernel(x_ref, mod_ref, a_ref, cb_ref, cu_ref, cup_ref, cun_ref, att_ref, ga_ref, gb_ref, gc_ref,
                  cw_ref, wa_ref, wb_ref, wc_ref, wo_ref, g2_ref, rwt_ref,
                  x1_ref, h2_ref, aff_ref, *, d, nct, nt):
    i = pl.program_id(1)
    ns, rows = x_ref.shape[0], x_ref.shape[1]
    n = ns * rows

    def stacked(ref):
        return ref[...].reshape(n, ref.shape[2])

    row = lax.broadcasted_iota(jnp.int32, (n, 1), 0)

    def per_sample(lo):
        out = mod_ref[ns - 1][:, lo:lo + d]
        for s in range(ns - 2, -1, -1):
            out = jnp.where(row < (s + 1) * rows, mod_ref[s][:, lo:lo + d], out)
        return out

    u = stacked(cu_ref).astype(F32)
    has_prev = jnp.logical_and(i != 0, i != nct)
    has_next = jnp.logical_and(i != nct - 1, i != nt - 1)
    u_prev = pltpu.roll(u, 1, axis=0)
    u_next = pltpu.roll(u, n - 1, axis=0)
    for s in range(ns):
        prev_row = jnp.where(has_prev, cup_ref[s, HALO - 1:HALO, :].astype(F32), 0.0)
        next_row = jnp.where(has_next, cun_ref[s, 0:1, :].astype(F32), 0.0)
        u_prev = jnp.where(row == s * rows, prev_row, u_prev)
        u_next = jnp.where(row == (s + 1) * rows - 1, next_row, u_next)
    cw = cw_ref[...]
    conv = cw[0:1, :] * u_prev + cw[1:2, :] * u + cw[2:3, :] * u_next
    bb = (stacked(cb_ref).astype(F32) * conv).astype(BF16)
    y = (stacked(ga_ref).astype(F32) * _dot(stacked(a_ref), wa_ref[...])
         + stacked(gb_ref).astype(F32) * _dot(bb, wb_ref[...])
         + stacked(gc_ref).astype(F32) * _dot(stacked(att_ref), wc_ref[...]))
    x1 = stacked(x_ref) + per_sample(2 * d) * _dot(y.astype(BF16), wo_ref[...])
    x1_ref[...] = x1.reshape(ns, rows, d)
    h2 = _modulated_norm(x1, g2_ref[...], per_sample(3 * d), per_sample(4 * d))
    h2_ref[...] = h2.astype(BF16).reshape(ns, rows, d)
    logits = _dot_nt(rwt_ref[...], h2, precision=HIGHEST)
    ex = jnp.exp(logits - jnp.max(logits, axis=0, keepdims=True))
    aff = ex / jnp.sum(ex, axis=0, keepdims=True)
    for s in range(ns):
        aff_ref[s] = aff[:, s * rows:(s + 1) * rows]


def _merge(xc, mod_l, a, cb, cu, att, ga, gb, gc, conv_w, wa, wb, wc, wo, norm2_g, rwt, nct):
    b, t, d = xc.shape
    nt = t // TOK_TILE
    ne = rwt.shape[0]
    ns = MERGE_SAMPLES if b % MERGE_SAMPLES == 0 else 1
    mod_m = jnp.concatenate([mod_l[:b]] + [mod_l[b:b + 1]] * ns, axis=0)
    ctx_blk = b // ns
    sub = TOK_TILE // HALO
    n8 = t // HALO

    def tok(width):
        return pl.BlockSpec((ns, TOK_TILE, width), lambda bi, i: (bi, i, 0))

    def const(shape):
        return pl.BlockSpec(shape, lambda bi, i: (0,) * len(shape))

    cwid = cu.shape[2]
    return pl.pallas_call(
        functools.partial(_merge_kernel, d=d, nct=nct, nt=nt),
        grid=(b // ns, nt),
        in_specs=[
            tok(d),
            pl.BlockSpec((ns, 1, N_MOD * d), lambda bi, i: (jnp.where(i < nct, ctx_blk, bi), 0, 0)),
            tok(a.shape[2]), tok(cwid), tok(cwid),
            pl.BlockSpec((ns, HALO, cwid), lambda bi, i: (bi, jnp.maximum(i * sub - 1, 0), 0)),
            pl.BlockSpec((ns, HALO, cwid), lambda bi, i: (bi, jnp.minimum((i + 1) * sub, n8 - 1), 0)),
            tok(att.shape[2]), tok(d), tok(d), tok(d),
            const(conv_w.shape), const(wa.shape), const(wb.shape), const(wc.shape), const(wo.shape),
            const((1, d)), const(rwt.shape),
        ],
        out_specs=[tok(d), tok(d), pl.BlockSpec((ns, ne, TOK_TILE), lambda bi, i: (bi, 0, i))],
        out_shape=[jax.ShapeDtypeStruct((b, t, d), F32), jax.ShapeDtypeStruct((b, t, d), BF16),
                   jax.ShapeDtypeStruct((b, ne, t), F32)],
        compiler_params=_params(("parallel", "parallel")),
        name="merge_residual_router",
    )(xc, mod_m, a, cb, cu, cu, cu, att, ga, gb, gc, conv_w, wa, wb, wc, wo, norm2_g, rwt)


def _prefix_count(mask, tri):
    e, n = mask.shape
    carry = jnp.zeros((e, 1), F32)
    outs = []
    for blk in range(n // LANES):
        xb = jnp.where(mask[:, blk * LANES:(blk + 1) * LANES], 1.0, 0.0).astype(BF16)
        pre = _dot(xb, tri) + carry
        carry = pre[:, LANES - 1:LANES]
        outs.append(pre)
    return jnp.concatenate(outs, axis=1) if len(outs) > 1 else outs[0]


def _topk_kernel(aff_ref, tri_ref, tind_ref, sut_ref, pos_ref, before_ref, *, off, n, cap):
    a = aff_ref[0][:, off:off + n]
    bits = pltpu.bitcast(a, jnp.int32)
    thr = jnp.zeros((a.shape[0], 1), jnp.int32)
    for bit in range(30, -1, -1):
        cand = thr | (1 << bit)
        cnt = jnp.sum(jnp.where(bits >= cand, 1.0, 0.0), axis=1, keepdims=True)
        thr = jnp.where(cnt >= cap, cand, thr)
    gt = bits > thr
    eq = bits == thr
    need = cap - jnp.sum(jnp.where(gt, 1.0, 0.0), axis=1, keepdims=True)
    tri = tri_ref[...]
    eq_rank = _prefix_count(eq, tri)
    sel = jnp.logical_or(gt, jnp.logical_and(eq, eq_rank <= need))
    slot = _prefix_count(sel, tri) - 1.0
    pos_ref[0] = jnp.where(sel, slot, -1.0).astype(jnp.int32)
    tile_cnt = _dot(jnp.where(sel, 1.0, 0.0).astype(BF16), tind_ref[...])
    before_ref[0] = _dot(tile_cnt.astype(BF16), sut_ref[...]).astype(jnp.int32)


def _topk_positions(aff_t, tri, off, n, cap):
    b, ne, t = aff_t.shape
    assert n // TOK_TILE < LANES and TOK_TILE <= 256
    lane = jnp.arange(LANES)
    tind = (jnp.arange(n)[:, None] // TOK_TILE == lane[None, :]).astype(BF16)
    sut = (lane[:, None] < lane[None, :]).astype(BF16)

    def const(shape):
        return pl.BlockSpec(shape, lambda bi: (0,) * len(shape))

    return pl.pallas_call(
        functools.partial(_topk_kernel, off=off, n=n, cap=cap),
        grid=(b,),
        in_specs=[pl.BlockSpec((1, ne, t), lambda bi: (bi, 0, 0)), const((LANES, LANES)), const(tind.shape),
                  const(sut.shape)],
        out_specs=[pl.BlockSpec((1, ne, n), lambda bi: (bi, 0, 0)), pl.BlockSpec((1, ne, LANES), lambda bi: (bi, 0, 0))],
        out_shape=[jax.ShapeDtypeStruct((b, ne, n), jnp.int32), jax.ShapeDtypeStruct((b, ne, LANES), jnp.int32)],
        compiler_params=_params(("parallel",)),
        name="expert_choice_topk",
    )(aff_t, tri, tind, sut)


def _gather_kernel(h_ref, pos_ref, xs_ref, *, off, n, cap, kt):
    pos = pos_ref[0, 0]
    acc = jnp.zeros(xs_ref.shape[1:], F32)
    for c0 in range(0, n, kt):
        slot = lax.broadcasted_iota(jnp.int32, (cap, kt), 0)
        onehot = jnp.where(pos[:, c0:c0 + kt] == slot, 1.0, 0.0).astype(BF16)
        acc = acc + _dot(onehot, h_ref[0, off + c0:off + c0 + kt, :])
    xs_ref[0] = acc.astype(BF16)


def _gather_tokens(h2, pos, off, n, cap):
    b, t, d = h2.shape
    ne = pos.shape[1]
    kt = min(n, 512)
    return pl.pallas_call(
        functools.partial(_gather_kernel, off=off, n=n, cap=cap, kt=kt),
        grid=(b, ne),
        in_specs=[pl.BlockSpec((1, t, d), lambda bi, e: (bi, 0, 0)),
                  pl.BlockSpec((1, 1, 1, n), lambda bi, e: (bi, e, 0, 0))],
        out_specs=pl.BlockSpec((1, cap, d), lambda bi, e: (e, bi, 0)),
        out_shape=jax.ShapeDtypeStruct((ne, b * cap, d), BF16),
        compiler_params=_params(("parallel", "parallel")),
        name="expert_gather",
    )(h2, pos.reshape(b, ne, 1, n))


def _ffn_kernel(xs_ref, wg_ref, wu_ref, wd_ref, ys_ref):
    x = xs_ref[0]
    g = _dot(x, wg_ref[0])
    hid = (g * _sigmoid(g)) * _dot(x, wu_ref[0])
    ys_ref[0] = _dot(hid.astype(BF16), wd_ref[0]).astype(BF16)


def _expert_ffn(xs, wg, wu, wd):
    ne, m, d = xs.shape
    ff = wg.shape[2]
    tm = min(m, 512)
    return pl.pallas_call(
        _ffn_kernel,
        grid=(ne, m // tm),
        in_specs=[pl.BlockSpec((1, tm, d), lambda e, j: (e, j, 0)),
                  pl.BlockSpec((1, d, ff), lambda e, j: (e, 0, 0)),
                  pl.BlockSpec((1, d, ff), lambda e, j: (e, 0, 0)),
                  pl.BlockSpec((1, ff, d), lambda e, j: (e, 0, 0))],
        out_specs=pl.BlockSpec((1, tm, d), lambda e, j: (e, j, 0)),
        out_shape=jax.ShapeDtypeStruct((ne, m, d), BF16),
        compiler_params=_params(("parallel", "parallel")),
        name="expert_ffn",
    )(xs, wg, wu, wd)


def _scatter_kernel(before_ref, x_ref, mod_ref, ys_ref, pos_ref, aff_ref, fg_ref, o_ref, acc_ref, *,
                    d, ne, cap, win, final):
    bi = pl.program_id(0)
    i = pl.program_id(1)
    m = mod_ref[0]
    pos = pos_ref[0]
    aff = aff_ref[0]
    rows = pos.shape[0]
    lane = lax.broadcasted_iota(jnp.int32, (rows, win), 1)
    acc = jnp.zeros((rows, d), F32)
    spans = []
    overflow = False
    for e in range(ne):
        base = (bi * ne + e) * LANES
        c0 = before_ref[base + i]
        c1 = before_ref[base + i + 1]
        w0 = pl.multiple_of(jnp.minimum((c0 // SLOT_ALIGN) * SLOT_ALIGN, cap - win), SLOT_ALIGN)
        onehot = jnp.where(pos[:, e:e + 1] - w0 == lane, 1.0, 0.0).astype(BF16)
        acc = acc + aff[:, e:e + 1] * _dot(onehot, ys_ref[e, pl.ds(w0, win), :])
        spans.append((c1, w0))
        overflow = jnp.logical_or(overflow, c1 > w0 + win)
    acc_ref[...] = acc

    @pl.when(overflow)
    def _():
        for e, (c1, w0) in enumerate(spans):
            def more(k, carry, e=e, w0=w0):
                ws = w0 + k * win
                wc = pl.multiple_of(jnp.minimum(ws, cap - win), SLOT_ALIGN)
                hit = jnp.logical_and(pos_ref[0][:, e:e + 1] - wc == lane, lane >= ws - wc)
                acc_ref[...] += aff_ref[0][:, e:e + 1] * _dot(jnp.where(hit, 1.0, 0.0).astype(BF16),
                                                              ys_ref[e, pl.ds(wc, win), :])
                return carry

            lax.fori_loop(1, (c1 - w0 + win - 1) // win, more, 0)

    x2 = x_ref[0] + m[:, 5 * d:6 * d] * acc_ref[...]
    if final:
        ms = jnp.mean(x2 * x2, axis=-1, keepdims=True)
        x2 = x2 * lax.rsqrt(ms + EPS) * fg_ref[...]
    o_ref[0] = x2


def _scatter_residual(x1, mod_l, ys, pos_n, aff_n, before, final_g, off, n, cap, mod_row_ctx, final):
    b, t, d = x1.shape
    ne = ys.shape[0]
    nb = mod_l.shape[0] - 1
    ot = off // TOK_TILE
    win = min(cap, SLOT_WINDOW)
    if final:
        out_shape = jax.ShapeDtypeStruct((b, n, d), F32)
        out_spec = pl.BlockSpec((1, TOK_TILE, d), lambda bi, i, bf: (bi, i, 0))
        aliases = {}
    else:
        out_shape = jax.ShapeDtypeStruct((b, t, d), F32)
        out_spec = pl.BlockSpec((1, TOK_TILE, d), lambda bi, i, bf: (bi, i + ot, 0))
        aliases = {1: 0}
    grid_spec = pltpu.PrefetchScalarGridSpec(
        num_scalar_prefetch=1,
        grid=(b, n // TOK_TILE),
        in_specs=[
            pl.BlockSpec((1, TOK_TILE, d), lambda bi, i, bf: (bi, i + ot, 0)),
            pl.BlockSpec((1, 1, N_MOD * d), lambda bi, i, bf: (nb if mod_row_ctx else bi, 0, 0)),
            pl.BlockSpec((ne, cap, d), lambda bi, i, bf: (0, bi, 0)),
            pl.BlockSpec((1, TOK_TILE, ne), lambda bi, i, bf: (bi, i, 0)),
            pl.BlockSpec((1, TOK_TILE, ne), lambda bi, i, bf: (bi, i + ot, 0)),
            pl.BlockSpec((1, d), lambda bi, i, bf: (0, 0)),
        ],
        out_specs=out_spec,
        scratch_shapes=[pltpu.VMEM((TOK_TILE, d), F32)],
    )
    return pl.pallas_call(
        functools.partial(_scatter_kernel, d=d, ne=ne, cap=cap, win=win, final=final),
        grid_spec=grid_spec,
        out_shape=out_shape,
        input_output_aliases=aliases,
        compiler_params=_params(("parallel", "parallel")),
        name="expert_scatter_residual",
    )(before.reshape(-1), x1, mod_l, ys, pos_n, aff_n, final_g)


def _rope_tables(ctx_len, seq, hd):
    rows = seq // GRID_W
    row = jnp.repeat(jnp.arange(rows), GRID_W).astype(F32)
    col = jnp.tile(jnp.arange(GRID_W), rows).astype(F32)
    inv = ROPE_THETA ** (-jnp.arange(0, hd // 2, 2, dtype=F32) / (hd // 2))
    ang = jnp.concatenate([row[:, None] * inv, col[:, None] * inv], axis=-1)
    cos = jnp.repeat(jnp.cos(ang), 2, axis=-1)
    sin = jnp.repeat(jnp.sin(ang), 2, axis=-1) * jnp.tile(jnp.array([-1.0, 1.0], F32), hd // 2)
    cos = jnp.concatenate([jnp.ones((ctx_len, hd), F32), cos], axis=0)
    sin = jnp.concatenate([jnp.zeros((ctx_len, hd), F32), sin], axis=0)
    rep = LANES // hd
    return jnp.tile(cos, (1, rep)), jnp.tile(sin, (1, rep))


def kernel(x, c, ctx, c_ctx, ada_w, ada_b, norm1_g, norm2_g, w_in, hg_lb_logits, hg_norm_g, conv_w, q_norm_g, k_norm_g, w_proj_a, w_proj_b, w_proj_c, w_out, router_w, w_gate, w_up, w_down, final_norm_g):
    b, s, d = x.shape
    ctx_len = ctx.shape[1]
    t = ctx_len + s
    depth = w_in.shape[0]
    hgw = hg_norm_g.shape[1]
    cw = conv_w.shape[2]
    aw = w_proj_c.shape[1]
    hd = q_norm_g.shape[1]
    kvw = (w_in.shape[2] - 5 * hgw - 3 * cw - aw - 3 * d) // 2
    ne = router_w.shape[2]
    nct = ctx_len // TOK_TILE
    assert ctx_len % TOK_TILE == 0 and s % TOK_TILE == 0 and LANES % hd == 0 and cw == d
    dims = dict(hgw=hgw, cw=cw, aw=aw, kvw=kvw, hd=hd, nct=nct)

    xc = jnp.concatenate([ctx, x], axis=1)
    n_rows = -(-(b + 1) // 8) * 8
    cvec = jnp.concatenate([c, c_ctx[None, :], jnp.zeros((n_rows - b - 1, d), F32)], axis=0)
    mod = _modulation(cvec, ada_w, ada_b)
    mod = mod[:, :b + 1].reshape(depth, b + 1, 1, N_MOD * d)

    cos_t, sin_t = _rope_tables(ctx_len, s, hd)
    lane = jnp.arange(LANES)
    gmat = (lane[:, None] // hd == lane[None, :] // hd).astype(BF16)
    tri = (lane[:, None] <= lane[None, :]).astype(BF16)
    rep = LANES // hd
    lb_logits = hg_lb_logits.astype(F32)

    out = None
    for l in range(depth):
        last = l == depth - 1
        mod_l = mod[l]
        (hq, lff, lfb, kkf, kkb, hi, hog, cb, cu, aq, ak, av, ga, gb, gc) = _in_projection(
            xc, mod_l, norm1_g[l][None, :], w_in[l].astype(BF16), cos_t, sin_t,
            jnp.tile(q_norm_g[l], rep)[None, :], jnp.tile(k_norm_g[l], rep)[None, :], gmat, lb_logits, l, dims)
        a = _hgrn(hq, lff, lfb, kkf, kkb, hi, hog, hg_norm_g[l][None, :], ctx_len)
        att = _attention(aq, ak, av, ctx_len)
        x1, h2, aff_t = _merge(xc, mod_l, a, cb, cu, att, ga, gb, gc, conv_w[l],
                               w_proj_a[l].astype(BF16), w_proj_b[l].astype(BF16), w_proj_c[l].astype(BF16),
                               w_out[l].astype(BF16), norm2_g[l][None, :], router_w[l].T, nct)
        aff_n = jnp.swapaxes(aff_t, 1, 2)
        wg, wu, wd = w_gate[l].astype(BF16), w_up[l].astype(BF16), w_down[l].astype(BF16)

        def moe(stream, off, n, mod_row_ctx, final):
            cap = CAPACITY_FACTOR * n // ne
            pos, before = _topk_positions(aff_t, tri, off, n, cap)
            xs = _gather_tokens(h2, pos, off, n, cap)
            ys = _expert_ffn(xs, wg, wu, wd)
            return _scatter_residual(stream, mod_l, ys, jnp.swapaxes(pos, 1, 2), aff_n, before,
                                     final_norm_g[None, :], off, n, cap, mod_row_ctx, final)

        if last:
            out = moe(x1, ctx_len, s, False, True)
        else:
            xc = moe(x1, ctx_len, s, False, False)
            xc = moe(xc, 0, ctx_len, True, False)
    return out
```

```python
import functools

import jax
import jax.numpy as jnp
from jax import lax
from jax.experimental import pallas as pl
from jax.experimental.pallas import tpu as pltpu

F32 = jnp.float32
BF16 = jnp.bfloat16

EPS = 1e-6
N_MOD = 6
HG_DK = 128
GRID_W = 64
ROPE_THETA = 10000.0
CAPACITY_FACTOR = 2

LANES = 128
TOK_TILE = 256
HG_CHUNK = 64
HG_SUB = 32
HG_HEADS_PER_STEP = 2
HG_UNROLL = 4
KV_TILE = 256
HALO = 16
ATTN_GROUPS_PER_STEP = 2
SLOT_WINDOW = 128
SLOT_ALIGN = 16
MERGE_SAMPLES = 2
LOG2E = 1.4426950408889634
VMEM_LIMIT = 56 * 1024 * 1024

HIGHEST = lax.Precision.HIGHEST


def _dot(a, b, precision=None):
    return jnp.dot(a, b, preferred_element_type=F32, precision=precision)


def _dot_nt(a, b, precision=None):
    return lax.dot_general(a, b, (((1,), (1,)), ((), ())), preferred_element_type=F32, precision=precision)


def _dot_tn(a, b):
    return lax.dot_general(a, b, (((0,), (0,)), ((), ())), preferred_element_type=F32)


def _sigmoid(x):
    return 1.0 / (1.0 + jnp.exp(-x))


def _params(sem, vmem=VMEM_LIMIT):
    return pltpu.CompilerParams(dimension_semantics=sem, vmem_limit_bytes=vmem)


def _mod_kernel(c_ref, w_ref, b_ref, o_ref):
    c = c_ref[...]
    sc = c * _sigmoid(c)
    o_ref[0] = _dot(sc, w_ref[0], precision=HIGHEST) + b_ref[0]


def _modulation(cvec, ada_w, ada_b):
    depth, d, n = ada_w.shape
    r = cvec.shape[0]
    tn = d
    return pl.pallas_call(
        _mod_kernel,
        grid=(depth, n // tn),
        in_specs=[
            pl.BlockSpec((r, d), lambda l, j: (0, 0)),
            pl.BlockSpec((1, d, tn), lambda l, j: (l, 0, j)),
            pl.BlockSpec((1, 1, tn), lambda l, j: (l, 0, j)),
        ],
        out_specs=pl.BlockSpec((1, r, tn), lambda l, j: (l, 0, j)),
        out_shape=jax.ShapeDtypeStruct((depth, r, n), F32),
        compiler_params=_params(("parallel", "parallel")),
        name="adaln_modulation",
    )(cvec, ada_w, ada_b.reshape(depth, 1, n))


def _modulated_norm(x, g, shift, scale):
    ms = jnp.mean(x * x, axis=-1, keepdims=True)
    return (x * lax.rsqrt(ms + EPS) * g) * (1.0 + scale) + shift


def _headnorm_rope(p, gain, gmat, cos, sin, hd, post_scale):
    rows, width = p.shape
    lane = lax.broadcasted_iota(jnp.int32, (rows, LANES), 1)
    even = (lane % 2) == 0
    outs = []
    for cb in range(width // LANES):
        xb = p[:, cb * LANES:(cb + 1) * LANES]
        sq = xb * xb
        hi = sq.astype(BF16)
        lo = (sq - hi.astype(F32)).astype(BF16)
        ss = _dot(hi, gmat) + _dot(lo, gmat)
        y = xb * lax.rsqrt(ss * (1.0 / hd) + EPS) * gain
        y_next = pltpu.roll(y, LANES - 1, axis=1)
        y_prev = pltpu.roll(y, 1, axis=1)
        ysw = jnp.where(even, y_next, y_prev)
        outs.append((y * cos + ysw * sin) * post_scale)
    return jnp.concatenate(outs, axis=1) if len(outs) > 1 else outs[0]


def _lower_bounds(lg, layer):
    e = jnp.exp(lg - jnp.max(lg, axis=0, keepdims=True))
    sm = e / jnp.sum(e, axis=0, keepdims=True)
    lb = jnp.zeros(lg.shape[1:], F32)
    for j in range(1, layer + 1):
        lb = lb + sm[j]
    return lb


def _forget_gate(z, lbd, lb_is_zero):
    zs = z * LOG2E
    sp = jnp.log2(1.0 + jnp.exp2(-jnp.abs(zs)))
    ls = jnp.minimum(zs, 0.0) - sp
    sneg = jnp.exp2(jnp.minimum(-zs, 0.0) - sp)
    if lb_is_zero:
        return ls, sneg
    a = jnp.log2(lbd)
    t = jnp.log2(1.0 - lbd) + ls
    mx = jnp.maximum(a, t)
    mn = jnp.minimum(a, t)
    return mx + jnp.log2(1.0 + jnp.exp2(mn - mx)), (1.0 - lbd) * sneg


def _inproj_kernel(x_ref, mod_ref, g_ref, w_ref, cos_ref, sin_ref, qg_ref, kg_ref, gmat_ref, lbl_ref,
                   hq_ref, lff_ref, lfb_ref, kkf_ref, kkb_ref, hi_ref, hog_ref, cb_ref, cu_ref, aq_ref, ak_ref,
                   av_ref, ga_ref, gb_ref, gc_ref, *, d, hgw, cw, aw, kvw, hd, layer):
    x = x_ref[0]
    m = mod_ref[0]
    h = _modulated_norm(x, g_ref[...], m[:, 0:d], m[:, d:2 * d]).astype(BF16)

    def proj(lo, width):
        return _dot(h, w_ref[:, lo:lo + width])

    lb = _lower_bounds(lbl_ref[...], layer)
    o = 0
    hq_ref[0] = proj(o, hgw).astype(BF16); o += hgw
    lf, kk = _forget_gate(proj(o, hgw), lb[0:1, :], layer == 0); o += hgw
    lff_ref[0] = _chunk_cumsum(lf, False)
    kkf_ref[0] = kk.astype(BF16)
    lf, kk = _forget_gate(proj(o, hgw), lb[1:2, :], layer == 0); o += hgw
    lfb_ref[0] = _chunk_cumsum(lf, True)
    kkb_ref[0] = kk.astype(BF16)
    hi_ref[0] = proj(o, hgw).astype(BF16); o += hgw
    g = proj(o, hgw); o += hgw
    hog_ref[0] = (g * _sigmoid(g)).astype(BF16)
    cb_ref[0] = proj(o, cw).astype(BF16); o += cw
    cc = proj(o, cw); o += cw
    cx = proj(o, cw); o += cw
    cu_ref[0] = (cc * cx).astype(BF16)
    cos = cos_ref[...]
    sin = sin_ref[...]
    gmat = gmat_ref[...]
    q = proj(o, aw); o += aw
    aq_ref[0] = _headnorm_rope(q, qg_ref[...], gmat, cos, sin, hd, hd ** -0.5 * LOG2E).astype(BF16)
    k = proj(o, kvw); o += kvw
    kn = _headnorm_rope(k, kg_ref[...], gmat, cos, sin, hd, 1.0).astype(BF16)
    vv = proj(o, kvw).astype(BF16); o += kvw
    ones_col = jnp.where(lax.broadcasted_iota(jnp.int32, (vv.shape[0], hd), 1) == 0, 1.0, 0.0).astype(BF16)
    for g in range(kvw // hd):
        ak_ref[0, g] = kn[:, g * hd:(g + 1) * hd]
        av_ref[0, g] = jnp.concatenate([vv[:, g * hd:(g + 1) * hd], ones_col], axis=1)
    ga_ref[0] = _sigmoid(proj(o, d)).astype(BF16); o += d
    gb_ref[0] = _sigmoid(proj(o, d)).astype(BF16); o += d
    gc_ref[0] = _sigmoid(proj(o, d)).astype(BF16); o += d


def _in_projection(xc, mod_l, norm_g, w_bf, cos_t, sin_t, qg, kg, gmat, lb_logits, layer, dims):
    b, t, d = xc.shape
    hgw, cw, aw, kvw, hd, nct = dims["hgw"], dims["cw"], dims["aw"], dims["kvw"], dims["hd"], dims["nct"]
    nt = t // TOK_TILE
    in_w = w_bf.shape[1]
    nb = mod_l.shape[0] - 1

    def tok(width):
        return pl.BlockSpec((1, TOK_TILE, width), lambda bi, i: (bi, i, 0))

    def const(shape):
        return pl.BlockSpec(shape, lambda bi, i: (0,) * len(shape))

    widths = [(hgw, BF16), (hgw, F32), (hgw, F32), (hgw, BF16), (hgw, BF16), (hgw, BF16), (hgw, BF16),
              (cw, BF16), (cw, BF16), (aw, BF16), (kvw, BF16), (kvw, BF16), (d, BF16), (d, BF16), (d, BF16)]
    ng = kvw // hd
    out_specs = [tok(w) for w, _ in widths]
    out_shape = [jax.ShapeDtypeStruct((b, t, w), dt) for w, dt in widths]
    for idx, wid in ((10, hd), (11, 2 * hd)):
        out_specs[idx] = pl.BlockSpec((1, ng, TOK_TILE, wid), lambda bi, i: (bi, 0, i, 0))
        out_shape[idx] = jax.ShapeDtypeStruct((b, ng, t, wid), BF16)
    return pl.pallas_call(
        functools.partial(_inproj_kernel, d=d, hgw=hgw, cw=cw, aw=aw, kvw=kvw, hd=hd, layer=layer),
        grid=(b, nt),
        in_specs=[
            tok(d),
            pl.BlockSpec((1, 1, N_MOD * d), lambda bi, i: (jnp.where(i < nct, nb, bi), 0, 0)),
            const((1, d)),
            pl.BlockSpec((d, in_w), lambda bi, i: (0, 0), pipeline_mode=pl.Buffered(1)),
            pl.BlockSpec((TOK_TILE, LANES), lambda bi, i: (i, 0)),
            pl.BlockSpec((TOK_TILE, LANES), lambda bi, i: (i, 0)),
            const((1, LANES)),
            const((1, LANES)),
            const((LANES, LANES)),
            const(lb_logits.shape),
        ],
        out_specs=out_specs,
        out_shape=out_shape,
        compiler_params=_params(("parallel", "parallel")),
        name="in_projection",
    )(xc, mod_l, norm_g, w_bf, cos_t, sin_t, qg, kg, gmat, lb_logits)


def _chunk_cumsum(x, reverse):
    n = x.shape[0]
    pos = lax.broadcasted_iota(jnp.int32, x.shape, 0) % HG_CHUNK
    s = 1
    while s < HG_CHUNK:
        if reverse:
            x = x + jnp.where(pos < HG_CHUNK - s, pltpu.roll(x, n - s, axis=0), 0.0)
        else:
            x = x + jnp.where(pos >= s, pltpu.roll(x, s, axis=0), 0.0)
        s *= 2
    return x


def _hgrn_chunk(q, cs, kk, v, st_ref, reverse):
    c = q.shape[0]
    nsb = c // HG_SUB
    anchors = []
    for i in range(nsb):
        r = i * HG_SUB + (HG_SUB // 2 if reverse else HG_SUB // 2 - 1)
        anchors.append(cs[r:r + 1, :])
    c_anchor = jnp.concatenate([jnp.broadcast_to(a, (HG_SUB, a.shape[1])) for a in anchors], axis=0)
    c_end = cs[0:1, :] if reverse else cs[c - 1:c, :]
    qh = q * jnp.exp2(cs - c_anchor)

    def seg_rows(i, rows_of):
        return jnp.concatenate([rows_of(j) if keep_j else jnp.zeros((HG_SUB, q.shape[1]), F32)
                                for j, keep_j in enumerate(i)], axis=0)

    q_ext = jnp.concatenate(
        [seg_rows([j == i for j in range(nsb)], lambda j: qh[j * HG_SUB:(j + 1) * HG_SUB]) for i in range(nsb)],
        axis=1)
    k_ext = jnp.concatenate(
        [seg_rows([(j >= i) if reverse else (j <= i) for j in range(nsb)],
                  lambda j, i=i: kk[j * HG_SUB:(j + 1) * HG_SUB]
                  * jnp.exp2(anchors[i] - cs[j * HG_SUB:(j + 1) * HG_SUB])) for i in range(nsb)],
        axis=1)
    a = _dot_nt(q_ext.astype(BF16), k_ext.astype(BF16))
    row = lax.broadcasted_iota(jnp.int32, (c, c), 0)
    col = lax.broadcasted_iota(jnp.int32, (c, c), 1)
    keep = (col >= row) if reverse else (col <= row)
    st = st_ref[...]
    o = (_dot(jnp.where(keep, a, 0.0).astype(BF16), v)
         + _dot_nt((qh * jnp.exp2(c_anchor)).astype(BF16), st.astype(BF16)))
    kh = (kk * jnp.exp2(c_end - cs)).astype(BF16)
    st_ref[...] = st * jnp.exp2(c_end) + _dot_tn(v, kh)
    return o


def _hgrn_kernel(hq_ref, csf_ref, csb_ref, kkf_ref, kkb_ref, hi_ref, hog_ref, ng_ref, a_ref,
                 of_ref, ob_ref, stf_ref, stb_ref, *, n_chunks, n_ctx_chunks, n_heads):
    stf_ref[...] = jnp.zeros_like(stf_ref)
    stb_ref[...] = jnp.zeros_like(stb_ref)
    c = HG_CHUNK

    def step(s, carry):
        rf = pl.multiple_of(s * c, c)
        cbk = jnp.where(s < n_ctx_chunks, n_ctx_chunks - 1 - s, n_chunks - 1 - (s - n_ctx_chunks))
        rb = pl.multiple_of(cbk * c, c)
        for h in range(n_heads):
            ln = slice(h * HG_DK, (h + 1) * HG_DK)
            of_ref[pl.ds(rf, c), ln] = _hgrn_chunk(
                hq_ref[0, pl.ds(rf, c), ln].astype(F32), csf_ref[0, pl.ds(rf, c), ln],
                kkf_ref[0, pl.ds(rf, c), ln].astype(F32), hi_ref[0, pl.ds(rf, c), ln], stf_ref.at[h], False)
            ob_ref[pl.ds(rb, c), ln] = _hgrn_chunk(
                hq_ref[0, pl.ds(rb, c), ln].astype(F32), csb_ref[0, pl.ds(rb, c), ln],
                kkb_ref[0, pl.ds(rb, c), ln].astype(F32), hi_ref[0, pl.ds(rb, c), ln], stb_ref.at[h], True)
        return carry

    lax.fori_loop(0, n_chunks, step, 0, unroll=HG_UNROLL)

    def readout(i, carry):
        r0 = pl.multiple_of(i * TOK_TILE, TOK_TILE)
        for h in range(n_heads):
            ln = slice(h * HG_DK, (h + 1) * HG_DK)
            o = of_ref[pl.ds(r0, TOK_TILE), ln] + ob_ref[pl.ds(r0, TOK_TILE), ln]
            ms = jnp.mean(o * o, axis=-1, keepdims=True)
            y = o * lax.rsqrt(ms + EPS) * ng_ref[:, ln]
            a_ref[0, pl.ds(r0, TOK_TILE), ln] = (y * hog_ref[0, pl.ds(r0, TOK_TILE), ln].astype(F32)).astype(BF16)
        return carry

    lax.fori_loop(0, (n_chunks * c) // TOK_TILE, readout, 0)


def _hgrn(hq, csf, csb, kkf, kkb, hi, hog, norm_g, ctx_len):
    b, t, hgw = hq.shape
    nh = HG_HEADS_PER_STEP
    wid = nh * HG_DK

    def seq():
        return pl.BlockSpec((1, t, wid), lambda bi, h: (bi, 0, h))

    return pl.pallas_call(
        functools.partial(_hgrn_kernel, n_chunks=t // HG_CHUNK, n_ctx_chunks=ctx_len // HG_CHUNK, n_heads=nh),
        grid=(b, hgw // wid),
        in_specs=[seq(), seq(), seq(), seq(), seq(), seq(), seq(),
                  pl.BlockSpec((1, wid), lambda bi, h: (0, h))],
        out_specs=seq(),
        out_shape=jax.ShapeDtypeStruct((b, t, hgw), BF16),
        scratch_shapes=[pltpu.VMEM((t, wid), F32), pltpu.VMEM((t, wid), F32),
                        pltpu.VMEM((nh, HG_DK, HG_DK), F32), pltpu.VMEM((nh, HG_DK, HG_DK), F32)],
        compiler_params=_params(("parallel", "parallel")),
        name="hgrn2_bidirectional",
    )(hq, csf, csb, kkf, kkb, hi, hog, norm_g)


def _attn_kernel(q_ref, k_ref, v_ref, o_ref, *, n_g, group, hd, nct, ctx_len, t_all):
    i = pl.program_id(2)

    def run(n_keys):
        outs = []
        for g in range(n_g):
            k = k_ref[0, g, 0:n_keys, :]
            v = v_ref[0, g, 0:n_keys, :]
            for h in range(group):
                c0 = (g * group + h) * hd
                s = _dot_nt(q_ref[0, :, c0:c0 + hd], k)
                p = jnp.exp2(s - jnp.max(s, axis=1, keepdims=True))
                ov = _dot(p.astype(BF16), v)
                outs.append((ov[:, 0:hd] / ov[:, hd:hd + 1]).astype(BF16))
        o_ref[0] = jnp.concatenate(outs, axis=1)

    @pl.when(i < nct)
    def _():
        run(ctx_len)

    @pl.when(i >= nct)
    def _():
        run(t_all)


def _attention(aq, ak, av, ctx_len):
    b, t, aw = aq.shape
    n_groups, hd = ak.shape[1], ak.shape[3]
    group = aw // (n_groups * hd)
    nt = t // TOK_TILE
    n_g = ATTN_GROUPS_PER_STEP
    return pl.pallas_call(
        functools.partial(_attn_kernel, n_g=n_g, group=group, hd=hd, nct=ctx_len // TOK_TILE, ctx_len=ctx_len,
                          t_all=t),
        grid=(b, n_groups // n_g, nt),
        in_specs=[
            pl.BlockSpec((1, TOK_TILE, n_g * group * hd), lambda bi, g, i: (bi, i, g)),
            pl.BlockSpec((1, n_g, t, hd), lambda bi, g, i: (bi, g, 0, 0)),
            pl.BlockSpec((1, n_g, t, 2 * hd), lambda bi, g, i: (bi, g, 0, 0)),
        ],
        out_specs=pl.BlockSpec((1, TOK_TILE, n_g * group * hd), lambda bi, g, i: (bi, i, g)),
        out_shape=jax.ShapeDtypeStruct((b, t, aw), BF16),
        compiler_params=_params(("parallel", "parallel", "parallel")),
        name="gqa_attention",
    )(aq, ak, av)


def _merge_kernel(x_ref, mod_ref, a_ref, cb_ref, cu_ref, cup_ref, cun_ref, att_ref, ga_ref, gb_ref, gc_ref,
                  cw_ref, wa_ref, wb_ref, wc_ref, wo_ref, g2_ref, rwt_ref,
                  x1_ref, h2_ref, aff_ref, *, d, nct, nt):
    i = pl.program_id(1)
    ns, rows = x_ref.shape[0], x_ref.shape[1]
    n = ns * rows

    def stacked(ref):
        return ref[...].reshape(n, ref.shape[2])

    row = lax.broadcasted_iota(jnp.int32, (n, 1), 0)

    def per_sample(lo):
        out = mod_ref[ns - 1][:, lo:lo + d]
        for s in range(ns - 2, -1, -1):
            out = jnp.where(row < (s + 1) * rows, mod_ref[s][:, lo:lo + d], out)
        return out

    u = stacked(cu_ref).astype(F32)
    has_prev = jnp.logical_and(i != 0, i != nct)
    has_next = jnp.logical_and(i != nct - 1, i != nt - 1)
    u_prev = pltpu.roll(u, 1, axis=0)
    u_next = pltpu.roll(u, n - 1, axis=0)
    for s in range(ns):
        prev_row = jnp.where(has_prev, cup_ref[s, HALO - 1:HALO, :].astype(F32), 0.0)
        next_row = jnp.where(has_next, cun_ref[s, 0:1, :].astype(F32), 0.0)
        u_prev = jnp.where(row == s * rows, prev_row, u_prev)
        u_next = jnp.where(row == (s + 1) * rows - 1, next_row, u_next)
    cw = cw_ref[...]
    conv = cw[0:1, :] * u_prev + cw[1:2, :] * u + cw[2:3, :] * u_next
    bb = (stacked(cb_ref).astype(F32) * conv).astype(BF16)
    y = (stacked(ga_ref).astype(F32) * _dot(stacked(a_ref), wa_ref[...])
         + stacked(gb_ref).astype(F32) * _dot(bb, wb_ref[...])
         + stacked(gc_ref).astype(F32) * _dot(stacked(att_ref), wc_ref[...]))
    x1 = stacked(x_ref) + per_sample(2 * d) * _dot(y.astype(BF16), wo_ref[...])
    x1_ref[...] = x1.reshape(ns, rows, d)
    h2 = _modulated_norm(x1, g2_ref[...], per_sample(3 * d), per_sample(4 * d))
    h2_ref[...] = h2.astype(BF16).reshape(ns, rows, d)
    logits = _dot_nt(rwt_ref[...], h2, precision=HIGHEST)
    ex = jnp.exp(logits - jnp.max(logits, axis=0, keepdims=True))
    aff = ex / jnp.sum(ex, axis=0, keepdims=True)
    for s in range(ns):
        aff_ref[s] = aff[:, s * rows:(s + 1) * rows]


def _merge(xc, mod_l, a, cb, cu, att, ga, gb, gc, conv_w, wa, wb, wc, wo, norm2_g, rwt, nct):
    b, t, d = xc.shape
    nt = t // TOK_TILE
    ne = rwt.shape[0]
    ns = MERGE_SAMPLES if b % MERGE_SAMPLES == 0 else 1
    mod_m = jnp.concatenate([mod_l[:b]] + [mod_l[b:b + 1]] * ns, axis=0)
    ctx_blk = b // ns
    sub = TOK_TILE // HALO
    n8 = t // HALO

    def tok(width):
        return pl.BlockSpec((ns, TOK_TILE, width), lambda bi, i: (bi, i, 0))

    def const(shape):
        return pl.BlockSpec(shape, lambda bi, i: (0,) * len(shape))

    cwid = cu.shape[2]
    return pl.pallas_call(
        functools.partial(_merge_kernel, d=d, nct=nct, nt=nt),
        grid=(b // ns, nt),
        in_specs=[
            tok(d),
            pl.BlockSpec((ns, 1, N_MOD * d), lambda bi, i: (jnp.where(i < nct, ctx_blk, bi), 0, 0)),
            tok(a.shape[2]), tok(cwid), tok(cwid),
            pl.BlockSpec((ns, HALO, cwid), lambda bi, i: (bi, jnp.maximum(i * sub - 1, 0), 0)),
            pl.BlockSpec((ns, HALO, cwid), lambda bi, i: (bi, jnp.minimum((i + 1) * sub, n8 - 1), 0)),
            tok(att.shape[2]), tok(d), tok(d), tok(d),
            const(conv_w.shape), const(wa.shape), const(wb.shape), const(wc.shape), const(wo.shape),
            const((1, d)), const(rwt.shape),
        ],
        out_specs=[tok(d), tok(d), pl.BlockSpec((ns, ne, TOK_TILE), lambda bi, i: (bi, 0, i))],
        out_shape=[jax.ShapeDtypeStruct((b, t, d), F32), jax.ShapeDtypeStruct((b, t, d), BF16),
                   jax.ShapeDtypeStruct((b, ne, t), F32)],
        compiler_params=_params(("parallel", "parallel")),
        name="merge_residual_router",
    )(xc, mod_m, a, cb, cu, cu, cu, att, ga, gb, gc, conv_w, wa, wb, wc, wo, norm2_g, rwt)


def _prefix_count(mask, tri):
    e, n = mask.shape
    carry = jnp.zeros((e, 1), F32)
    outs = []
    for blk in range(n // LANES):
        xb = jnp.where(mask[:, blk * LANES:(blk + 1) * LANES], 1.0, 0.0).astype(BF16)
        pre = _dot(xb, tri) + carry
        carry = pre[:, LANES - 1:LANES]
        outs.append(pre)
    return jnp.concatenate(outs, axis=1) if len(outs) > 1 else outs[0]


def _topk_kernel(aff_ref, tri_ref, tind_ref, sut_ref, pos_ref, before_ref, *, off, n, cap):
    a = aff_ref[0][:, off:off + n]
    bits = pltpu.bitcast(a, jnp.int32)
    thr = jnp.zeros((a.shape[0], 1), jnp.int32)
    for bit in range(30, -1, -1):
        cand = thr | (1 << bit)
        cnt = jnp.sum(jnp.where(bits >= cand, 1.0, 0.0), axis=1, keepdims=True)
        thr = jnp.where(cnt >= cap, cand, thr)
    gt = bits > thr
    eq = bits == thr
    need = cap - jnp.sum(jnp.where(gt, 1.0, 0.0), axis=1, keepdims=True)
    tri = tri_ref[...]
    eq_rank = _prefix_count(eq, tri)
    sel = jnp.logical_or(gt, jnp.logical_and(eq, eq_rank <= need))
    slot = _prefix_count(sel, tri) - 1.0
    pos_ref[0] = jnp.where(sel, slot, -1.0).astype(jnp.int32)
    tile_cnt = _dot(jnp.where(sel, 1.0, 0.0).astype(BF16), tind_ref[...])
    before_ref[0] = _dot(tile_cnt.astype(BF16), sut_ref[...]).astype(jnp.int32)


def _topk_positions(aff_t, tri, off, n, cap):
    b, ne, t = aff_t.shape
    assert n // TOK_TILE < LANES and TOK_TILE <= 256
    lane = jnp.arange(LANES)
    tind = (jnp.arange(n)[:, None] // TOK_TILE == lane[None, :]).astype(BF16)
    sut = (lane[:, None] < lane[None, :]).astype(BF16)

    def const(shape):
        return pl.BlockSpec(shape, lambda bi: (0,) * len(shape))

    return pl.pallas_call(
        functools.partial(_topk_kernel, off=off, n=n, cap=cap),
        grid=(b,),
        in_specs=[pl.BlockSpec((1, ne, t), lambda bi: (bi, 0, 0)), const((LANES, LANES)), const(tind.shape),
                  const(sut.shape)],
        out_specs=[pl.BlockSpec((1, ne, n), lambda bi: (bi, 0, 0)), pl.BlockSpec((1, ne, LANES), lambda bi: (bi, 0, 0))],
        out_shape=[jax.ShapeDtypeStruct((b, ne, n), jnp.int32), jax.ShapeDtypeStruct((b, ne, LANES), jnp.int32)],
        compiler_params=_params(("parallel",)),
        name="expert_choice_topk",
    )(aff_t, tri, tind, sut)


def _gather_kernel(before_ref, h_ref, pos_ref, xs_ref, *, ne, cap, win):
    bi = pl.program_id(0)
    tt = pl.program_id(1)

    @pl.when(tt == 0)
    def _():
        xs_ref[...] = jnp.zeros_like(xs_ref)

    h = h_ref[0]
    pos = pos_ref[0]
    rows = h.shape[0]
    sub = lax.broadcasted_iota(jnp.int32, (win, rows), 0)
    hots, spans = [], []
    overflow = False
    for e in range(ne):
        base = (bi * ne + e) * LANES
        c0 = before_ref[base + tt]
        c1 = before_ref[base + tt + 1]
        w0 = pl.multiple_of(jnp.minimum((c0 // SLOT_ALIGN) * SLOT_ALIGN, cap - win), SLOT_ALIGN)
        hots.append(jnp.where(pos[e:e + 1, :] - w0 == sub, 1.0, 0.0).astype(BF16))
        spans.append((c1, w0))
        overflow = jnp.logical_or(overflow, c1 > w0 + win)
    picked = _dot(jnp.concatenate(hots, axis=0), h).astype(BF16)
    for e, (c1, w0) in enumerate(spans):
        xs_ref[e, pl.ds(w0, win), :] += picked[e * win:(e + 1) * win, :]

    @pl.when(overflow)
    def _():
        for e, (c1, w0) in enumerate(spans):
            def more(k, carry, e=e, w0=w0):
                ws = w0 + k * win
                wc = pl.multiple_of(jnp.minimum(ws, cap - win), SLOT_ALIGN)
                hit = jnp.logical_and(pos_ref[0][e:e + 1, :] - wc == sub, sub >= ws - wc)
                xs_ref[e, pl.ds(wc, win), :] += _dot(jnp.where(hit, 1.0, 0.0).astype(BF16), h_ref[0]).astype(BF16)
                return carry

            lax.fori_loop(1, (c1 - w0 + win - 1) // win, more, 0)


def _gather_tokens(h2, pos, before, off, n, cap):
    b, t, d = h2.shape
    ne = pos.shape[1]
    ot = off // TOK_TILE
    win = min(cap, SLOT_WINDOW)
    grid_spec = pltpu.PrefetchScalarGridSpec(
        num_scalar_prefetch=1,
        grid=(b, n // TOK_TILE),
        in_specs=[pl.BlockSpec((1, TOK_TILE, d), lambda bi, tt, bf: (bi, tt + ot, 0)),
                  pl.BlockSpec((1, ne, TOK_TILE), lambda bi, tt, bf: (bi, 0, tt))],
        out_specs=pl.BlockSpec((ne, cap, d), lambda bi, tt, bf: (0, bi, 0)),
    )
    return pl.pallas_call(
        functools.partial(_gather_kernel, ne=ne, cap=cap, win=win),
        grid_spec=grid_spec,
        out_shape=jax.ShapeDtypeStruct((ne, b * cap, d), BF16),
        compiler_params=_params(("parallel", "arbitrary")),
        name="expert_gather",
    )(before.reshape(-1), h2, pos)


def _ffn_kernel(xs_ref, wg_ref, wu_ref, wd_ref, ys_ref, wgb_ref, wub_ref, wdb_ref):
    @pl.when(pl.program_id(1) == 0)
    def _():
        wgb_ref[...] = wg_ref[0].astype(BF16)
        wub_ref[...] = wu_ref[0].astype(BF16)
        wdb_ref[...] = wd_ref[0].astype(BF16)

    x = xs_ref[0]
    g = _dot(x, wgb_ref[...])
    hid = (g * _sigmoid(g)) * _dot(x, wub_ref[...])
    ys_ref[0] = _dot(hid.astype(BF16), wdb_ref[...]).astype(BF16)


def _expert_ffn(xs, wg, wu, wd):
    ne, m, d = xs.shape
    ff = wg.shape[2]
    tm = min(m, 512)
    return pl.pallas_call(
        _ffn_kernel,
        grid=(ne, m // tm),
        in_specs=[pl.BlockSpec((1, tm, d), lambda e, j: (e, j, 0)),
                  pl.BlockSpec((1, d, ff), lambda e, j: (e, 0, 0)),
                  pl.BlockSpec((1, d, ff), lambda e, j: (e, 0, 0)),
                  pl.BlockSpec((1, ff, d), lambda e, j: (e, 0, 0))],
        out_specs=pl.BlockSpec((1, tm, d), lambda e, j: (e, j, 0)),
        out_shape=jax.ShapeDtypeStruct((ne, m, d), BF16),
        scratch_shapes=[pltpu.VMEM((d, ff), BF16), pltpu.VMEM((d, ff), BF16), pltpu.VMEM((ff, d), BF16)],
        compiler_params=_params(("parallel", "arbitrary")),
        name="expert_ffn",
    )(xs, wg, wu, wd)


def _scatter_kernel(before_ref, x_ref, mod_ref, ys_ref, pos_ref, aff_ref, fg_ref, o_ref, acc_ref, *,
                    d, ne, cap, win, final):
    bi = pl.program_id(0)
    i = pl.program_id(1)
    m = mod_ref[0]
    pos = pos_ref[0]
    aff = aff_ref[0]
    rows = pos.shape[0]
    lane = lax.broadcasted_iota(jnp.int32, (rows, win), 1)
    acc = jnp.zeros((rows, d), F32)
    spans = []
    overflow = False
    for e in range(ne):
        base = (bi * ne + e) * LANES
        c0 = before_ref[base + i]
        c1 = before_ref[base + i + 1]
        w0 = pl.multiple_of(jnp.minimum((c0 // SLOT_ALIGN) * SLOT_ALIGN, cap - win), SLOT_ALIGN)
        onehot = jnp.where(pos[:, e:e + 1] - w0 == lane, 1.0, 0.0).astype(BF16)
        acc = acc + aff[:, e:e + 1] * _dot(onehot, ys_ref[e, pl.ds(w0, win), :])
        spans.append((c1, w0))
        overflow = jnp.logical_or(overflow, c1 > w0 + win)
    acc_ref[...] = acc

    @pl.when(overflow)
    def _():
        for e, (c1, w0) in enumerate(spans):
            def more(k, carry, e=e, w0=w0):
                ws = w0 + k * win
                wc = pl.multiple_of(jnp.minimum(ws, cap - win), SLOT_ALIGN)
                hit = jnp.logical_and(pos_ref[0][:, e:e + 1] - wc == lane, lane >= ws - wc)
                acc_ref[...] += aff_ref[0][:, e:e + 1] * _dot(jnp.where(hit, 1.0, 0.0).astype(BF16),
                                                              ys_ref[e, pl.ds(wc, win), :])
                return carry

            lax.fori_loop(1, (c1 - w0 + win - 1) // win, more, 0)

    x2 = x_ref[0] + m[:, 5 * d:6 * d] * acc_ref[...]
    if final:
        ms = jnp.mean(x2 * x2, axis=-1, keepdims=True)
        x2 = x2 * lax.rsqrt(ms + EPS) * fg_ref[...]
    o_ref[0] = x2


def _scatter_residual(x1, mod_l, ys, pos_n, aff_n, before, final_g, off, n, cap, mod_row_ctx, final):
    b, t, d = x1.shape
    ne = ys.shape[0]
    nb = mod_l.shape[0] - 1
    ot = off // TOK_TILE
    win = min(cap, SLOT_WINDOW)
    if final:
        out_shape = jax.ShapeDtypeStruct((b, n, d), F32)
        out_spec = pl.BlockSpec((1, TOK_TILE, d), lambda bi, i, bf: (bi, i, 0))
        aliases = {}
    else:
        out_shape = jax.ShapeDtypeStruct((b, t, d), F32)
        out_spec = pl.BlockSpec((1, TOK_TILE, d), lambda bi, i, bf: (bi, i + ot, 0))
        aliases = {1: 0}
    grid_spec = pltpu.PrefetchScalarGridSpec(
        num_scalar_prefetch=1,
        grid=(b, n // TOK_TILE),
        in_specs=[
            pl.BlockSpec((1, TOK_TILE, d), lambda bi, i, bf: (bi, i + ot, 0)),
            pl.BlockSpec((1, 1, N_MOD * d), lambda bi, i, bf: (nb if mod_row_ctx else bi, 0, 0)),
            pl.BlockSpec((ne, cap, d), lambda bi, i, bf: (0, bi, 0)),
            pl.BlockSpec((1, TOK_TILE, ne), lambda bi, i, bf: (bi, i, 0)),
            pl.BlockSpec((1, TOK_TILE, ne), lambda bi, i, bf: (bi, i + ot, 0)),
            pl.BlockSpec((1, d), lambda bi, i, bf: (0, 0)),
        ],
        out_specs=out_spec,
        scratch_shapes=[pltpu.VMEM((TOK_TILE, d), F32)],
    )
    return pl.pallas_call(
        functools.partial(_scatter_kernel, d=d, ne=ne, cap=cap, win=win, final=final),
        grid_spec=grid_spec,
        out_shape=out_shape,
        input_output_aliases=aliases,
        compiler_params=_params(("parallel", "parallel")),
        name="expert_scatter_residual",
    )(before.reshape(-1), x1, mod_l, ys, pos_n, aff_n, final_g)


def _rope_tables(ctx_len, seq, hd):
    rows = seq // GRID_W
    row = jnp.repeat(jnp.arange(rows), GRID_W).astype(F32)
    col = jnp.tile(jnp.arange(GRID_W), rows).astype(F32)
    inv = ROPE_THETA ** (-jnp.arange(0, hd // 2, 2, dtype=F32) / (hd // 2))
    ang = jnp.concatenate([row[:, None] * inv, col[:, None] * inv], axis=-1)
    cos = jnp.repeat(jnp.cos(ang), 2, axis=-1)
    sin = jnp.repeat(jnp.sin(ang), 2, axis=-1) * jnp.tile(jnp.array([-1.0, 1.0], F32), hd // 2)
    cos = jnp.concatenate([jnp.ones((ctx_len, hd), F32), cos], axis=0)
    sin = jnp.concatenate([jnp.zeros((ctx_len, hd), F32), sin], axis=0)
    rep = LANES // hd
    return jnp.tile(cos, (1, rep)), jnp.tile(sin, (1, rep))


def kernel(x, c, ctx, c_ctx, ada_w, ada_b, norm1_g, norm2_g, w_in, hg_lb_logits, hg_norm_g, conv_w, q_norm_g, k_norm_g, w_proj_a, w_proj_b, w_proj_c, w_out, router_w, w_gate, w_up, w_down, final_norm_g):
    b, s, d = x.shape
    ctx_len = ctx.shape[1]
    t = ctx_len + s
    depth = w_in.shape[0]
    hgw = hg_norm_g.shape[1]
    cw = conv_w.shape[2]
    aw = w_proj_c.shape[1]
    hd = q_norm_g.shape[1]
    kvw = (w_in.shape[2] - 5 * hgw - 3 * cw - aw - 3 * d) // 2
    ne = router_w.shape[2]
    nct = ctx_len // TOK_TILE
    assert ctx_len % TOK_TILE == 0 and s % TOK_TILE == 0 and LANES % hd == 0 and cw == d
    dims = dict(hgw=hgw, cw=cw, aw=aw, kvw=kvw, hd=hd, nct=nct)

    xc = jnp.concatenate([ctx, x], axis=1)
    n_rows = -(-(b + 1) // 8) * 8
    cvec = jnp.concatenate([c, c_ctx[None, :], jnp.zeros((n_rows - b - 1, d), F32)], axis=0)
    mod = _modulation(cvec, ada_w, ada_b)
    mod = mod[:, :b + 1].reshape(depth, b + 1, 1, N_MOD * d)

    cos_t, sin_t = _rope_tables(ctx_len, s, hd)
    lane = jnp.arange(LANES)
    gmat = (lane[:, None] // hd == lane[None, :] // hd).astype(BF16)
    tri = (lane[:, None] <= lane[None, :]).astype(BF16)
    rep = LANES // hd
    lb_logits = hg_lb_logits.astype(F32)

    out = None
    for l in range(depth):
        last = l == depth - 1
        mod_l = mod[l]
        (hq, lff, lfb, kkf, kkb, hi, hog, cb, cu, aq, ak, av, ga, gb, gc) = _in_projection(
            xc, mod_l, norm1_g[l][None, :], w_in[l].astype(BF16), cos_t, sin_t,
            jnp.tile(q_norm_g[l], rep)[None, :], jnp.tile(k_norm_g[l], rep)[None, :], gmat, lb_logits, l, dims)
        a = _hgrn(hq, lff, lfb, kkf, kkb, hi, hog, hg_norm_g[l][None, :], ctx_len)
        att = _attention(aq, ak, av, ctx_len)
        x1, h2, aff_t = _merge(xc, mod_l, a, cb, cu, att, ga, gb, gc, conv_w[l],
                               w_proj_a[l].astype(BF16), w_proj_b[l].astype(BF16), w_proj_c[l].astype(BF16),
                               w_out[l].astype(BF16), norm2_g[l][None, :], router_w[l].T, nct)
        aff_n = jnp.swapaxes(aff_t, 1, 2)
        wg, wu, wd = w_gate[l], w_up[l], w_down[l]

        def moe(stream, off, n, mod_row_ctx, final):
            cap = CAPACITY_FACTOR * n // ne
            pos, before = _topk_positions(aff_t, tri, off, n, cap)
            xs = _gather_tokens(h2, pos, before, off, n, cap)
            ys = _expert_ffn(xs, wg, wu, wd)
            return _scatter_residual(stream, mod_l, ys, jnp.swapaxes(pos, 1, 2), aff_n, before,
                                     final_norm_g[None, :], off, n, cap, mod_row_ctx, final)

        if last:
            out = moe(x1, ctx_len, s, False, True)
        else:
            xc = moe(x1, ctx_len, s, False, False)
            xc = moe(xc, 0, ctx_len, True, False)
    return out
```

```python
import functools

import jax
import jax.numpy as jnp
from jax import lax
from jax.experimental import pallas as pl
from jax.experimental.pallas import tpu as pltpu

F32 = jnp.float32
BF16 = jnp.bfloat16

EPS = 1e-6
N_MOD = 6
HG_DK = 128
GRID_W = 64
ROPE_THETA = 10000.0
CAPACITY_FACTOR = 2

LANES = 128
TOK_TILE = 256
HG_CHUNK = 64
HG_SUB = 32
HG_HEADS_PER_STEP = 2
HG_UNROLL = 4
KV_TILE = 256
HALO = 16
ATTN_GROUPS_PER_STEP = 2
SLOT_WINDOW = 128
SLOT_ALIGN = 16
MERGE_SAMPLES = 2
LOG2E = 1.4426950408889634
VMEM_LIMIT = 56 * 1024 * 1024

HIGHEST = lax.Precision.HIGHEST


def _dot(a, b, precision=None):
    return jnp.dot(a, b, preferred_element_type=F32, precision=precision)


def _dot_nt(a, b, precision=None):
    return lax.dot_general(a, b, (((1,), (1,)), ((), ())), preferred_element_type=F32, precision=precision)


def _dot_tn(a, b):
    return lax.dot_general(a, b, (((0,), (0,)), ((), ())), preferred_element_type=F32)


def _sigmoid(x):
    return 1.0 / (1.0 + jnp.exp(-x))


def _params(sem, vmem=VMEM_LIMIT):
    return pltpu.CompilerParams(dimension_semantics=sem, vmem_limit_bytes=vmem)


def _mod_kernel(c_ref, w_ref, b_ref, o_ref):
    c = c_ref[...]
    sc = c * _sigmoid(c)
    o_ref[0] = _dot(sc, w_ref[0], precision=HIGHEST) + b_ref[0]


def _modulation(cvec, ada_w, ada_b):
    depth, d, n = ada_w.shape
    r = cvec.shape[0]
    tn = d
    return pl.pallas_call(
        _mod_kernel,
        grid=(depth, n // tn),
        in_specs=[
            pl.BlockSpec((r, d), lambda l, j: (0, 0)),
            pl.BlockSpec((1, d, tn), lambda l, j: (l, 0, j)),
            pl.BlockSpec((1, 1, tn), lambda l, j: (l, 0, j)),
        ],
        out_specs=pl.BlockSpec((1, r, tn), lambda l, j: (l, 0, j)),
        out_shape=jax.ShapeDtypeStruct((depth, r, n), F32),
        compiler_params=_params(("parallel", "parallel")),
        name="adaln_modulation",
    )(cvec, ada_w, ada_b.reshape(depth, 1, n))


def _modulated_norm(x, g, shift, scale):
    ms = jnp.mean(x * x, axis=-1, keepdims=True)
    return (x * lax.rsqrt(ms + EPS) * g) * (1.0 + scale) + shift


def _headnorm_rope(p, gain, gmat, cos, sin, hd, post_scale):
    rows, width = p.shape
    lane = lax.broadcasted_iota(jnp.int32, (rows, LANES), 1)
    even = (lane % 2) == 0
    outs = []
    for cb in range(width // LANES):
        xb = p[:, cb * LANES:(cb + 1) * LANES]
        ss = _dot((xb * xb).astype(BF16), gmat)
        y = xb * lax.rsqrt(ss * (1.0 / hd) + EPS) * gain
        y_next = pltpu.roll(y, LANES - 1, axis=1)
        y_prev = pltpu.roll(y, 1, axis=1)
        ysw = jnp.where(even, y_next, y_prev)
        outs.append((y * cos + ysw * sin) * post_scale)
    return jnp.concatenate(outs, axis=1) if len(outs) > 1 else outs[0]


def _lower_bounds(lg, layer):
    e = jnp.exp(lg - jnp.max(lg, axis=0, keepdims=True))
    sm = e / jnp.sum(e, axis=0, keepdims=True)
    lb = jnp.zeros(lg.shape[1:], F32)
    for j in range(1, layer + 1):
        lb = lb + sm[j]
    return lb


def _forget_gate(z, lbd, lb_is_zero):
    zs = z * LOG2E
    sp = jnp.log2(1.0 + jnp.exp2(-jnp.abs(zs)))
    ls = jnp.minimum(zs, 0.0) - sp
    sneg = jnp.exp2(jnp.minimum(-zs, 0.0) - sp)
    if lb_is_zero:
        return ls, sneg
    a = jnp.log2(lbd)
    t = jnp.log2(1.0 - lbd) + ls
    mx = jnp.maximum(a, t)
    mn = jnp.minimum(a, t)
    return mx + jnp.log2(1.0 + jnp.exp2(mn - mx)), (1.0 - lbd) * sneg


def _inproj_kernel(x_ref, mod_ref, g_ref, w_ref, cos_ref, sin_ref, qg_ref, kg_ref, gmat_ref, lbl_ref,
                   hq_ref, lff_ref, lfb_ref, kkf_ref, kkb_ref, hi_ref, hog_ref, cb_ref, cu_ref, aq_ref, ak_ref,
                   av_ref, ga_ref, gb_ref, gc_ref, *, d, hgw, cw, aw, kvw, hd, layer):
    x = x_ref[0]
    m = mod_ref[0]
    h = _modulated_norm(x, g_ref[...], m[:, 0:d], m[:, d:2 * d]).astype(BF16)

    def proj(lo, width):
        return _dot(h, w_ref[:, lo:lo + width])

    lb = _lower_bounds(lbl_ref[...], layer)
    o = 0
    hq_ref[0] = proj(o, hgw).astype(BF16); o += hgw
    lf, kk = _forget_gate(proj(o, hgw), lb[0:1, :], layer == 0); o += hgw
    lff_ref[0] = _chunk_cumsum(lf, False)
    kkf_ref[0] = kk.astype(BF16)
    lf, kk = _forget_gate(proj(o, hgw), lb[1:2, :], layer == 0); o += hgw
    lfb_ref[0] = _chunk_cumsum(lf, True)
    kkb_ref[0] = kk.astype(BF16)
    hi_ref[0] = proj(o, hgw).astype(BF16); o += hgw
    g = proj(o, hgw); o += hgw
    hog_ref[0] = (g * _sigmoid(g)).astype(BF16)
    cb_ref[0] = proj(o, cw).astype(BF16); o += cw
    cc = proj(o, cw); o += cw
    cx = proj(o, cw); o += cw
    cu_ref[0] = (cc * cx).astype(BF16)
    cos = cos_ref[...]
    sin = sin_ref[...]
    gmat = gmat_ref[...]
    q = proj(o, aw); o += aw
    aq_ref[0] = _headnorm_rope(q, qg_ref[...], gmat, cos, sin, hd, hd ** -0.5 * LOG2E).astype(BF16)
    k = proj(o, kvw); o += kvw
    kn = _headnorm_rope(k, kg_ref[...], gmat, cos, sin, hd, 1.0).astype(BF16)
    vv = proj(o, kvw).astype(BF16); o += kvw
    ones_col = jnp.where(lax.broadcasted_iota(jnp.int32, (vv.shape[0], hd), 1) == 0, 1.0, 0.0).astype(BF16)
    for g in range(kvw // hd):
        ak_ref[0, g] = kn[:, g * hd:(g + 1) * hd]
        av_ref[0, g] = jnp.concatenate([vv[:, g * hd:(g + 1) * hd], ones_col], axis=1)
    ga_ref[0] = _sigmoid(proj(o, d)).astype(BF16); o += d
    gb_ref[0] = _sigmoid(proj(o, d)).astype(BF16); o += d
    gc_ref[0] = _sigmoid(proj(o, d)).astype(BF16); o += d


def _in_projection(xc, mod_l, norm_g, w_bf, cos_t, sin_t, qg, kg, gmat, lb_logits, layer, dims):
    b, t, d = xc.shape
    hgw, cw, aw, kvw, hd, nct = dims["hgw"], dims["cw"], dims["aw"], dims["kvw"], dims["hd"], dims["nct"]
    nt = t // TOK_TILE
    in_w = w_bf.shape[1]
    nb = mod_l.shape[0] - 1

    def tok(width):
        return pl.BlockSpec((1, TOK_TILE, width), lambda bi, i: (bi, i, 0))

    def const(shape):
        return pl.BlockSpec(shape, lambda bi, i: (0,) * len(shape))

    widths = [(hgw, BF16), (hgw, F32), (hgw, F32), (hgw, BF16), (hgw, BF16), (hgw, BF16), (hgw, BF16),
              (cw, BF16), (cw, BF16), (aw, BF16), (kvw, BF16), (kvw, BF16), (d, BF16), (d, BF16), (d, BF16)]
    ng = kvw // hd
    out_specs = [tok(w) for w, _ in widths]
    out_shape = [jax.ShapeDtypeStruct((b, t, w), dt) for w, dt in widths]
    for idx, wid in ((10, hd), (11, 2 * hd)):
        out_specs[idx] = pl.BlockSpec((1, ng, TOK_TILE, wid), lambda bi, i: (bi, 0, i, 0))
        out_shape[idx] = jax.ShapeDtypeStruct((b, ng, t, wid), BF16)
    return pl.pallas_call(
        functools.partial(_inproj_kernel, d=d, hgw=hgw, cw=cw, aw=aw, kvw=kvw, hd=hd, layer=layer),
        grid=(b, nt),
        in_specs=[
            tok(d),
            pl.BlockSpec((1, 1, N_MOD * d), lambda bi, i: (jnp.where(i < nct, nb, bi), 0, 0)),
            const((1, d)),
            pl.BlockSpec((d, in_w), lambda bi, i: (0, 0), pipeline_mode=pl.Buffered(1)),
            pl.BlockSpec((TOK_TILE, LANES), lambda bi, i: (i, 0)),
            pl.BlockSpec((TOK_TILE, LANES), lambda bi, i: (i, 0)),
            const((1, LANES)),
            const((1, LANES)),
            const((LANES, LANES)),
            const(lb_logits.shape),
        ],
        out_specs=out_specs,
        out_shape=out_shape,
        compiler_params=_params(("parallel", "parallel")),
        name="in_projection",
    )(xc, mod_l, norm_g, w_bf, cos_t, sin_t, qg, kg, gmat, lb_logits)


def _chunk_cumsum(x, reverse):
    n = x.shape[0]
    pos = lax.broadcasted_iota(jnp.int32, x.shape, 0) % HG_CHUNK
    s = 1
    while s < HG_CHUNK:
        if reverse:
            x = x + jnp.where(pos < HG_CHUNK - s, pltpu.roll(x, n - s, axis=0), 0.0)
        else:
            x = x + jnp.where(pos >= s, pltpu.roll(x, s, axis=0), 0.0)
        s *= 2
    return x


def _hgrn_chunk(q, cs, kk, v, st_ref, reverse):
    c = q.shape[0]
    nsb = c // HG_SUB
    anchors = []
    for i in range(nsb):
        r = i * HG_SUB + (HG_SUB // 2 if reverse else HG_SUB // 2 - 1)
        anchors.append(cs[r:r + 1, :])
    c_anchor = jnp.concatenate([jnp.broadcast_to(a, (HG_SUB, a.shape[1])) for a in anchors], axis=0)
    c_end = cs[0:1, :] if reverse else cs[c - 1:c, :]
    qh = q * jnp.exp2(cs - c_anchor)

    def seg_rows(i, rows_of):
        return jnp.concatenate([rows_of(j) if keep_j else jnp.zeros((HG_SUB, q.shape[1]), F32)
                                for j, keep_j in enumerate(i)], axis=0)

    q_ext = jnp.concatenate(
        [seg_rows([j == i for j in range(nsb)], lambda j: qh[j * HG_SUB:(j + 1) * HG_SUB]) for i in range(nsb)],
        axis=1)
    k_ext = jnp.concatenate(
        [seg_rows([(j >= i) if reverse else (j <= i) for j in range(nsb)],
                  lambda j, i=i: kk[j * HG_SUB:(j + 1) * HG_SUB]
                  * jnp.exp2(anchors[i] - cs[j * HG_SUB:(j + 1) * HG_SUB])) for i in range(nsb)],
        axis=1)
    a = _dot_nt(q_ext.astype(BF16), k_ext.astype(BF16))
    row = lax.broadcasted_iota(jnp.int32, (c, c), 0)
    col = lax.broadcasted_iota(jnp.int32, (c, c), 1)
    keep = (col >= row) if reverse else (col <= row)
    st = st_ref[...]
    o = (_dot(jnp.where(keep, a, 0.0).astype(BF16), v)
         + _dot_nt((qh * jnp.exp2(c_anchor)).astype(BF16), st.astype(BF16)))
    kh = (kk * jnp.exp2(c_end - cs)).astype(BF16)
    st_ref[...] = st * jnp.exp2(c_end) + _dot_tn(v, kh)
    return o


def _hgrn_kernel(hq_ref, csf_ref, csb_ref, kkf_ref, kkb_ref, hi_ref, hog_ref, ng_ref, a_ref,
                 of_ref, ob_ref, stf_ref, stb_ref, *, n_chunks, n_ctx_chunks, n_heads):
    stf_ref[...] = jnp.zeros_like(stf_ref)
    stb_ref[...] = jnp.zeros_like(stb_ref)
    c = HG_CHUNK

    def step(s, carry):
        rf = pl.multiple_of(s * c, c)
        cbk = jnp.where(s < n_ctx_chunks, n_ctx_chunks - 1 - s, n_chunks - 1 - (s - n_ctx_chunks))
        rb = pl.multiple_of(cbk * c, c)
        for h in range(n_heads):
            ln = slice(h * HG_DK, (h + 1) * HG_DK)
            of_ref[pl.ds(rf, c), ln] = _hgrn_chunk(
                hq_ref[0, pl.ds(rf, c), ln].astype(F32), csf_ref[0, pl.ds(rf, c), ln],
                kkf_ref[0, pl.ds(rf, c), ln].astype(F32), hi_ref[0, pl.ds(rf, c), ln], stf_ref.at[h], False)
            ob_ref[pl.ds(rb, c), ln] = _hgrn_chunk(
                hq_ref[0, pl.ds(rb, c), ln].astype(F32), csb_ref[0, pl.ds(rb, c), ln],
                kkb_ref[0, pl.ds(rb, c), ln].astype(F32), hi_ref[0, pl.ds(rb, c), ln], stb_ref.at[h], True)
        return carry

    lax.fori_loop(0, n_chunks, step, 0, unroll=HG_UNROLL)

    def readout(i, carry):
        r0 = pl.multiple_of(i * TOK_TILE, TOK_TILE)
        for h in range(n_heads):
            ln = slice(h * HG_DK, (h + 1) * HG_DK)
            o = of_ref[pl.ds(r0, TOK_TILE), ln] + ob_ref[pl.ds(r0, TOK_TILE), ln]
            ms = jnp.mean(o * o, axis=-1, keepdims=True)
            y = o * lax.rsqrt(ms + EPS) * ng_ref[:, ln]
            a_ref[0, pl.ds(r0, TOK_TILE), ln] = (y * hog_ref[0, pl.ds(r0, TOK_TILE), ln].astype(F32)).astype(BF16)
        return carry

    lax.fori_loop(0, (n_chunks * c) // TOK_TILE, readout, 0)


def _hgrn(hq, csf, csb, kkf, kkb, hi, hog, norm_g, ctx_len):
    b, t, hgw = hq.shape
    nh = HG_HEADS_PER_STEP
    wid = nh * HG_DK

    def seq():
        return pl.BlockSpec((1, t, wid), lambda bi, h: (bi, 0, h))

    return pl.pallas_call(
        functools.partial(_hgrn_kernel, n_chunks=t // HG_CHUNK, n_ctx_chunks=ctx_len // HG_CHUNK, n_heads=nh),
        grid=(b, hgw // wid),
        in_specs=[seq(), seq(), seq(), seq(), seq(), seq(), seq(),
                  pl.BlockSpec((1, wid), lambda bi, h: (0, h))],
        out_specs=seq(),
        out_shape=jax.ShapeDtypeStruct((b, t, hgw), BF16),
        scratch_shapes=[pltpu.VMEM((t, wid), F32), pltpu.VMEM((t, wid), F32),
                        pltpu.VMEM((nh, HG_DK, HG_DK), F32), pltpu.VMEM((nh, HG_DK, HG_DK), F32)],
        compiler_params=_params(("parallel", "parallel")),
        name="hgrn2_bidirectional",
    )(hq, csf, csb, kkf, kkb, hi, hog, norm_g)


def _attn_kernel(q_ref, k_ref, v_ref, o_ref, *, n_g, group, hd, nct, ctx_len, t_all):
    i = pl.program_id(2)

    def run(n_keys):
        outs = []
        for g in range(n_g):
            k = k_ref[0, g, 0:n_keys, :]
            v = v_ref[0, g, 0:n_keys, :]
            for h in range(group):
                c0 = (g * group + h) * hd
                s = _dot_nt(q_ref[0, :, c0:c0 + hd], k)
                p = jnp.exp2(s - jnp.max(s, axis=1, keepdims=True))
                ov = _dot(p.astype(BF16), v)
                outs.append((ov[:, 0:hd] / ov[:, hd:hd + 1]).astype(BF16))
        o_ref[0] = jnp.concatenate(outs, axis=1)

    @pl.when(i < nct)
    def _():
        run(ctx_len)

    @pl.when(i >= nct)
    def _():
        run(t_all)


def _attention(aq, ak, av, ctx_len):
    b, t, aw = aq.shape
    n_groups, hd = ak.shape[1], ak.shape[3]
    group = aw // (n_groups * hd)
    nt = t // TOK_TILE
    n_g = ATTN_GROUPS_PER_STEP
    return pl.pallas_call(
        functools.partial(_attn_kernel, n_g=n_g, group=group, hd=hd, nct=ctx_len // TOK_TILE, ctx_len=ctx_len,
                          t_all=t),
        grid=(b, n_groups // n_g, nt),
        in_specs=[
            pl.BlockSpec((1, TOK_TILE, n_g * group * hd), lambda bi, g, i: (bi, i, g)),
            pl.BlockSpec((1, n_g, t, hd), lambda bi, g, i: (bi, g, 0, 0)),
            pl.BlockSpec((1, n_g, t, 2 * hd), lambda bi, g, i: (bi, g, 0, 0)),
        ],
        out_specs=pl.BlockSpec((1, TOK_TILE, n_g * group * hd), lambda bi, g, i: (bi, i, g)),
        out_shape=jax.ShapeDtypeStruct((b, t, aw), BF16),
        compiler_params=_params(("parallel", "parallel", "parallel")),
        name="gqa_attention",
    )(aq, ak, av)


def _merge_kernel(x_ref, mod_ref, a_ref, cb_ref, cu_ref, cup_ref, cun_ref, att_ref, ga_ref, gb_ref, gc_ref,
                  cw_ref, wa_ref, wb_ref, wc_ref, wo_ref, g2_ref, rwt_ref,
                  x1_ref, h2_ref, aff_ref, *, d, nct, nt):
    i = pl.program_id(1)
    ns, rows = x_ref.shape[0], x_ref.shape[1]
    n = ns * rows

    def stacked(ref):
        return ref[...].reshape(n, ref.shape[2])

    row = lax.broadcasted_iota(jnp.int32, (n, 1), 0)

    def per_sample(lo):
        out = mod_ref[ns - 1][:, lo:lo + d]
        for s in range(ns - 2, -1, -1):
            out = jnp.where(row < (s + 1) * rows, mod_ref[s][:, lo:lo + d], out)
        return out

    u = stacked(cu_ref).astype(F32)
    has_prev = jnp.logical_and(i != 0, i != nct)
    has_next = jnp.logical_and(i != nct - 1, i != nt - 1)
    u_prev = pltpu.roll(u, 1, axis=0)
    u_next = pltpu.roll(u, n - 1, axis=0)
    for s in range(ns):
        prev_row = jnp.where(has_prev, cup_ref[s, HALO - 1:HALO, :].astype(F32), 0.0)
        next_row = jnp.where(has_next, cun_ref[s, 0:1, :].astype(F32), 0.0)
        u_prev = jnp.where(row == s * rows, prev_row, u_prev)
        u_next = jnp.where(row == (s + 1) * rows - 1, next_row, u_next)
    cw = cw_ref[...]
    conv = cw[0:1, :] * u_prev + cw[1:2, :] * u + cw[2:3, :] * u_next
    bb = (stacked(cb_ref).astype(F32) * conv).astype(BF16)
    y = (stacked(ga_ref).astype(F32) * _dot(stacked(a_ref), wa_ref[...])
         + stacked(gb_ref).astype(F32) * _dot(bb, wb_ref[...])
         + stacked(gc_ref).astype(F32) * _dot(stacked(att_ref), wc_ref[...]))
    x1 = stacked(x_ref) + per_sample(2 * d) * _dot(y.astype(BF16), wo_ref[...])
    x1_ref[...] = x1.reshape(ns, rows, d)
    h2 = _modulated_norm(x1, g2_ref[...], per_sample(3 * d), per_sample(4 * d))
    h2_ref[...] = h2.astype(BF16).reshape(ns, rows, d)
    logits = _dot_nt(rwt_ref[...], h2, precision=HIGHEST)
    ex = jnp.exp(logits - jnp.max(logits, axis=0, keepdims=True))
    aff = ex / jnp.sum(ex, axis=0, keepdims=True)
    for s in range(ns):
        aff_ref[s] = aff[:, s * rows:(s + 1) * rows]


def _merge(xc, mod_l, a, cb, cu, att, ga, gb, gc, conv_w, wa, wb, wc, wo, norm2_g, rwt, nct):
    b, t, d = xc.shape
    nt = t // TOK_TILE
    ne = rwt.shape[0]
    ns = MERGE_SAMPLES if b % MERGE_SAMPLES == 0 else 1
    mod_m = jnp.concatenate([mod_l[:b]] + [mod_l[b:b + 1]] * ns, axis=0)
    ctx_blk = b // ns
    sub = TOK_TILE // HALO
    n8 = t // HALO

    def tok(width):
        return pl.BlockSpec((ns, TOK_TILE, width), lambda bi, i: (bi, i, 0))

    def const(shape):
        return pl.BlockSpec(shape, lambda bi, i: (0,) * len(shape))

    cwid = cu.shape[2]
    return pl.pallas_call(
        functools.partial(_merge_kernel, d=d, nct=nct, nt=nt),
        grid=(b // ns, nt),
        in_specs=[
            tok(d),
            pl.BlockSpec((ns, 1, N_MOD * d), lambda bi, i: (jnp.where(i < nct, ctx_blk, bi), 0, 0)),
            tok(a.shape[2]), tok(cwid), tok(cwid),
            pl.BlockSpec((ns, HALO, cwid), lambda bi, i: (bi, jnp.maximum(i * sub - 1, 0), 0)),
            pl.BlockSpec((ns, HALO, cwid), lambda bi, i: (bi, jnp.minimum((i + 1) * sub, n8 - 1), 0)),
            tok(att.shape[2]), tok(d), tok(d), tok(d),
            const(conv_w.shape), const(wa.shape), const(wb.shape), const(wc.shape), const(wo.shape),
            const((1, d)), const(rwt.shape),
        ],
        out_specs=[tok(d), tok(d), pl.BlockSpec((ns, ne, TOK_TILE), lambda bi, i: (bi, 0, i))],
        out_shape=[jax.ShapeDtypeStruct((b, t, d), F32), jax.ShapeDtypeStruct((b, t, d), BF16),
                   jax.ShapeDtypeStruct((b, ne, t), F32)],
        compiler_params=_params(("parallel", "parallel")),
        name="merge_residual_router",
    )(xc, mod_m, a, cb, cu, cu, cu, att, ga, gb, gc, conv_w, wa, wb, wc, wo, norm2_g, rwt)


def _prefix_count(mask, tri):
    e, n = mask.shape
    carry = jnp.zeros((e, 1), F32)
    outs = []
    for blk in range(n // LANES):
        xb = jnp.where(mask[:, blk * LANES:(blk + 1) * LANES], 1.0, 0.0).astype(BF16)
        pre = _dot(xb, tri) + carry
        carry = pre[:, LANES - 1:LANES]
        outs.append(pre)
    return jnp.concatenate(outs, axis=1) if len(outs) > 1 else outs[0]


def _topk_kernel(aff_ref, tri_ref, tind_ref, sut_ref, pos_ref, before_ref, *, off, n, cap):
    a = aff_ref[0][:, off:off + n]
    bits = pltpu.bitcast(a, jnp.int32)
    thr = jnp.zeros((a.shape[0], 1), jnp.int32)
    for bit in range(30, -1, -1):
        cand = thr | (1 << bit)
        cnt = jnp.sum(jnp.where(bits >= cand, 1.0, 0.0), axis=1, keepdims=True)
        thr = jnp.where(cnt >= cap, cand, thr)
    gt = bits > thr
    eq = bits == thr
    need = cap - jnp.sum(jnp.where(gt, 1.0, 0.0), axis=1, keepdims=True)
    tri = tri_ref[...]
    eq_rank = _prefix_count(eq, tri)
    sel = jnp.logical_or(gt, jnp.logical_and(eq, eq_rank <= need))
    slot = _prefix_count(sel, tri) - 1.0
    pos_ref[0] = jnp.where(sel, slot, -1.0).astype(jnp.int32)
    tile_cnt = _dot(jnp.where(sel, 1.0, 0.0).astype(BF16), tind_ref[...])
    before_ref[0] = _dot(tile_cnt.astype(BF16), sut_ref[...]).astype(jnp.int32)


def _topk_positions(aff_t, tri, off, n, cap):
    b, ne, t = aff_t.shape
    assert n // TOK_TILE < LANES and TOK_TILE <= 256
    lane = jnp.arange(LANES)
    tind = (jnp.arange(n)[:, None] // TOK_TILE == lane[None, :]).astype(BF16)
    sut = (lane[:, None] < lane[None, :]).astype(BF16)

    def const(shape):
        return pl.BlockSpec(shape, lambda bi: (0,) * len(shape))

    return pl.pallas_call(
        functools.partial(_topk_kernel, off=off, n=n, cap=cap),
        grid=(b,),
        in_specs=[pl.BlockSpec((1, ne, t), lambda bi: (bi, 0, 0)), const((LANES, LANES)), const(tind.shape),
                  const(sut.shape)],
        out_specs=[pl.BlockSpec((1, ne, n), lambda bi: (bi, 0, 0)), pl.BlockSpec((1, ne, LANES), lambda bi: (bi, 0, 0))],
        out_shape=[jax.ShapeDtypeStruct((b, ne, n), jnp.int32), jax.ShapeDtypeStruct((b, ne, LANES), jnp.int32)],
        compiler_params=_params(("parallel",)),
        name="expert_choice_topk",
    )(aff_t, tri, tind, sut)


def _gather_kernel(before_ref, h_ref, pos_ref, xs_ref, *, ne, cap, win):
    bi = pl.program_id(0)
    tt = pl.program_id(1)

    @pl.when(tt == 0)
    def _():
        xs_ref[...] = jnp.zeros_like(xs_ref)

    h = h_ref[0]
    pos = pos_ref[0]
    rows = h.shape[0]
    sub = lax.broadcasted_iota(jnp.int32, (win, rows), 0)
    hots, spans = [], []
    overflow = False
    for e in range(ne):
        base = (bi * ne + e) * LANES
        c0 = before_ref[base + tt]
        c1 = before_ref[base + tt + 1]
        w0 = pl.multiple_of(jnp.minimum((c0 // SLOT_ALIGN) * SLOT_ALIGN, cap - win), SLOT_ALIGN)
        hots.append(jnp.where(pos[e:e + 1, :] - w0 == sub, 1.0, 0.0).astype(BF16))
        spans.append((c1, w0))
        overflow = jnp.logical_or(overflow, c1 > w0 + win)
    picked = _dot(jnp.concatenate(hots, axis=0), h).astype(BF16)
    for e, (c1, w0) in enumerate(spans):
        xs_ref[e, pl.ds(w0, win), :] += picked[e * win:(e + 1) * win, :]

    @pl.when(overflow)
    def _():
        for e, (c1, w0) in enumerate(spans):
            def more(k, carry, e=e, w0=w0):
                ws = w0 + k * win
                wc = pl.multiple_of(jnp.minimum(ws, cap - win), SLOT_ALIGN)
                hit = jnp.logical_and(pos_ref[0][e:e + 1, :] - wc == sub, sub >= ws - wc)
                xs_ref[e, pl.ds(wc, win), :] += _dot(jnp.where(hit, 1.0, 0.0).astype(BF16), h_ref[0]).astype(BF16)
                return carry

            lax.fori_loop(1, (c1 - w0 + win - 1) // win, more, 0)


def _gather_tokens(h2, pos, before, off, n, cap):
    b, t, d = h2.shape
    ne = pos.shape[1]
    ot = off // TOK_TILE
    win = min(cap, SLOT_WINDOW)
    grid_spec = pltpu.PrefetchScalarGridSpec(
        num_scalar_prefetch=1,
        grid=(b, n // TOK_TILE),
        in_specs=[pl.BlockSpec((1, TOK_TILE, d), lambda bi, tt, bf: (bi, tt + ot, 0)),
                  pl.BlockSpec((1, ne, TOK_TILE), lambda bi, tt, bf: (bi, 0, tt))],
        out_specs=pl.BlockSpec((ne, cap, d), lambda bi, tt, bf: (0, bi, 0)),
    )
    return pl.pallas_call(
        functools.partial(_gather_kernel, ne=ne, cap=cap, win=win),
        grid_spec=grid_spec,
        out_shape=jax.ShapeDtypeStruct((ne, b * cap, d), BF16),
        compiler_params=_params(("parallel", "arbitrary")),
        name="expert_gather",
    )(before.reshape(-1), h2, pos)


def _ffn_kernel(xs_ref, wg_ref, wu_ref, wd_ref, ys_ref, wgb_ref, wub_ref, wdb_ref):
    @pl.when(pl.program_id(1) == 0)
    def _():
        wgb_ref[...] = wg_ref[0, 0].astype(BF16)
        wub_ref[...] = wu_ref[0, 0].astype(BF16)
        wdb_ref[...] = wd_ref[0, 0].astype(BF16)

    x = xs_ref[0]
    g = _dot(x, wgb_ref[...])
    hid = (g * _sigmoid(g)) * _dot(x, wub_ref[...])
    ys_ref[0] = _dot(hid.astype(BF16), wdb_ref[...]).astype(BF16)


def _expert_ffn(xs, wg, wu, wd, layer):
    ne, m, d = xs.shape
    ff = wg.shape[3]
    tm = min(m, 512)
    return pl.pallas_call(
        _ffn_kernel,
        grid=(ne, m // tm),
        in_specs=[pl.BlockSpec((1, tm, d), lambda e, j: (e, j, 0)),
                  pl.BlockSpec((1, 1, d, ff), lambda e, j: (layer, e, 0, 0)),
                  pl.BlockSpec((1, 1, d, ff), lambda e, j: (layer, e, 0, 0)),
                  pl.BlockSpec((1, 1, ff, d), lambda e, j: (layer, e, 0, 0))],
        out_specs=pl.BlockSpec((1, tm, d), lambda e, j: (e, j, 0)),
        out_shape=jax.ShapeDtypeStruct((ne, m, d), BF16),
        scratch_shapes=[pltpu.VMEM((d, ff), BF16), pltpu.VMEM((d, ff), BF16), pltpu.VMEM((ff, d), BF16)],
        compiler_params=_params(("parallel", "arbitrary")),
        name="expert_ffn",
    )(xs, wg, wu, wd)


def _scatter_kernel(before_ref, x_ref, mod_ref, ys_ref, pos_ref, aff_ref, fg_ref, o_ref, acc_ref, *,
                    d, ne, cap, win, final):
    bi = pl.program_id(0)
    i = pl.program_id(1)
    m = mod_ref[0]
    pos = pos_ref[0]
    aff = aff_ref[0]
    rows = pos.shape[0]
    lane = lax.broadcasted_iota(jnp.int32, (rows, win), 1)
    acc = jnp.zeros((rows, d), F32)
    spans = []
    overflow = False
    for e in range(ne):
        base = (bi * ne + e) * LANES
        c0 = before_ref[base + i]
        c1 = before_ref[base + i + 1]
        w0 = pl.multiple_of(jnp.minimum((c0 // SLOT_ALIGN) * SLOT_ALIGN, cap - win), SLOT_ALIGN)
        onehot = jnp.where(pos[:, e:e + 1] - w0 == lane, 1.0, 0.0).astype(BF16)
        acc = acc + aff[:, e:e + 1] * _dot(onehot, ys_ref[e, pl.ds(w0, win), :])
        spans.append((c1, w0))
        overflow = jnp.logical_or(overflow, c1 > w0 + win)
    acc_ref[...] = acc

    @pl.when(overflow)
    def _():
        for e, (c1, w0) in enumerate(spans):
            def more(k, carry, e=e, w0=w0):
                ws = w0 + k * win
                wc = pl.multiple_of(jnp.minimum(ws, cap - win), SLOT_ALIGN)
                hit = jnp.logical_and(pos_ref[0][:, e:e + 1] - wc == lane, lane >= ws - wc)
                acc_ref[...] += aff_ref[0][:, e:e + 1] * _dot(jnp.where(hit, 1.0, 0.0).astype(BF16),
                                                              ys_ref[e, pl.ds(wc, win), :])
                return carry

            lax.fori_loop(1, (c1 - w0 + win - 1) // win, more, 0)

    x2 = x_ref[0] + m[:, 5 * d:6 * d] * acc_ref[...]
    if final:
        ms = jnp.mean(x2 * x2, axis=-1, keepdims=True)
        x2 = x2 * lax.rsqrt(ms + EPS) * fg_ref[...]
    o_ref[0] = x2


def _scatter_residual(x1, mod_l, ys, pos_n, aff_n, before, final_g, off, n, cap, mod_row_ctx, final):
    b, t, d = x1.shape
    ne = ys.shape[0]
    nb = mod_l.shape[0] - 1
    ot = off // TOK_TILE
    win = min(cap, SLOT_WINDOW)
    if final:
        out_shape = jax.ShapeDtypeStruct((b, n, d), F32)
        out_spec = pl.BlockSpec((1, TOK_TILE, d), lambda bi, i, bf: (bi, i, 0))
        aliases = {}
    else:
        out_shape = jax.ShapeDtypeStruct((b, t, d), F32)
        out_spec = pl.BlockSpec((1, TOK_TILE, d), lambda bi, i, bf: (bi, i + ot, 0))
        aliases = {1: 0}
    grid_spec = pltpu.PrefetchScalarGridSpec(
        num_scalar_prefetch=1,
        grid=(b, n // TOK_TILE),
        in_specs=[
            pl.BlockSpec((1, TOK_TILE, d), lambda bi, i, bf: (bi, i + ot, 0)),
            pl.BlockSpec((1, 1, N_MOD * d), lambda bi, i, bf: (nb if mod_row_ctx else bi, 0, 0)),
            pl.BlockSpec((ne, cap, d), lambda bi, i, bf: (0, bi, 0)),
            pl.BlockSpec((1, TOK_TILE, ne), lambda bi, i, bf: (bi, i, 0)),
            pl.BlockSpec((1, TOK_TILE, ne), lambda bi, i, bf: (bi, i + ot, 0)),
            pl.BlockSpec((1, d), lambda bi, i, bf: (0, 0)),
        ],
        out_specs=out_spec,
        scratch_shapes=[pltpu.VMEM((TOK_TILE, d), F32)],
    )
    return pl.pallas_call(
        functools.partial(_scatter_kernel, d=d, ne=ne, cap=cap, win=win, final=final),
        grid_spec=grid_spec,
        out_shape=out_shape,
        input_output_aliases=aliases,
        compiler_params=_params(("parallel", "parallel")),
        name="expert_scatter_residual",
    )(before.reshape(-1), x1, mod_l, ys, pos_n, aff_n, final_g)


def _rope_tables(ctx_len, seq, hd):
    rows = seq // GRID_W
    row = jnp.repeat(jnp.arange(rows), GRID_W).astype(F32)
    col = jnp.tile(jnp.arange(GRID_W), rows).astype(F32)
    inv = ROPE_THETA ** (-jnp.arange(0, hd // 2, 2, dtype=F32) / (hd // 2))
    ang = jnp.concatenate([row[:, None] * inv, col[:, None] * inv], axis=-1)
    cos = jnp.repeat(jnp.cos(ang), 2, axis=-1)
    sin = jnp.repeat(jnp.sin(ang), 2, axis=-1) * jnp.tile(jnp.array([-1.0, 1.0], F32), hd // 2)
    cos = jnp.concatenate([jnp.ones((ctx_len, hd), F32), cos], axis=0)
    sin = jnp.concatenate([jnp.zeros((ctx_len, hd), F32), sin], axis=0)
    rep = LANES // hd
    return jnp.tile(cos, (1, rep)), jnp.tile(sin, (1, rep))


def kernel(x, c, ctx, c_ctx, ada_w, ada_b, norm1_g, norm2_g, w_in, hg_lb_logits, hg_norm_g, conv_w, q_norm_g, k_norm_g, w_proj_a, w_proj_b, w_proj_c, w_out, router_w, w_gate, w_up, w_down, final_norm_g):
    b, s, d = x.shape
    ctx_len = ctx.shape[1]
    t = ctx_len + s
    depth = w_in.shape[0]
    hgw = hg_norm_g.shape[1]
    cw = conv_w.shape[2]
    aw = w_proj_c.shape[1]
    hd = q_norm_g.shape[1]
    kvw = (w_in.shape[2] - 5 * hgw - 3 * cw - aw - 3 * d) // 2
    ne = router_w.shape[2]
    nct = ctx_len // TOK_TILE
    assert ctx_len % TOK_TILE == 0 and s % TOK_TILE == 0 and LANES % hd == 0 and cw == d
    dims = dict(hgw=hgw, cw=cw, aw=aw, kvw=kvw, hd=hd, nct=nct)

    xc = jnp.concatenate([ctx, x], axis=1)
    n_rows = -(-(b + 1) // 8) * 8
    cvec = jnp.concatenate([c, c_ctx[None, :], jnp.zeros((n_rows - b - 1, d), F32)], axis=0)
    mod = _modulation(cvec, ada_w, ada_b)
    mod = mod[:, :b + 1].reshape(depth, b + 1, 1, N_MOD * d)

    cos_t, sin_t = _rope_tables(ctx_len, s, hd)
    lane = jnp.arange(LANES)
    gmat = (lane[:, None] // hd == lane[None, :] // hd).astype(BF16)
    tri = (lane[:, None] <= lane[None, :]).astype(BF16)
    rep = LANES // hd
    lb_logits = hg_lb_logits.astype(F32)

    out = None
    for l in range(depth):
        last = l == depth - 1
        mod_l = mod[l]
        (hq, lff, lfb, kkf, kkb, hi, hog, cb, cu, aq, ak, av, ga, gb, gc) = _in_projection(
            xc, mod_l, norm1_g[l][None, :], w_in[l].astype(BF16), cos_t, sin_t,
            jnp.tile(q_norm_g[l], rep)[None, :], jnp.tile(k_norm_g[l], rep)[None, :], gmat, lb_logits, l, dims)
        a = _hgrn(hq, lff, lfb, kkf, kkb, hi, hog, hg_norm_g[l][None, :], ctx_len)
        att = _attention(aq, ak, av, ctx_len)
        x1, h2, aff_t = _merge(xc, mod_l, a, cb, cu, att, ga, gb, gc, conv_w[l],
                               w_proj_a[l].astype(BF16), w_proj_b[l].astype(BF16), w_proj_c[l].astype(BF16),
                               w_out[l].astype(BF16), norm2_g[l][None, :], router_w[l].T, nct)
        aff_n = jnp.swapaxes(aff_t, 1, 2)

        def moe(stream, off, n, mod_row_ctx, final):
            cap = CAPACITY_FACTOR * n // ne
            pos, before = _topk_positions(aff_t, tri, off, n, cap)
            xs = _gather_tokens(h2, pos, before, off, n, cap)
            ys = _expert_ffn(xs, w_gate, w_up, w_down, l)
            return _scatter_residual(stream, mod_l, ys, jnp.swapaxes(pos, 1, 2), aff_n, before,
                                     final_norm_g[None, :], off, n, cap, mod_row_ctx, final)

        if last:
            out = moe(x1, ctx_len, s, False, True)
        else:
            xc = moe(x1, ctx_len, s, False, False)
            xc = moe(xc, 0, ctx_len, True, False)
    return out
```

```python
import functools

import jax
import jax.numpy as jnp
from jax import lax
from jax.experimental import pallas as pl
from jax.experimental.pallas import tpu as pltpu

F32 = jnp.float32
BF16 = jnp.bfloat16

EPS = 1e-6
N_MOD = 6
HG_DK = 128
GRID_W = 64
ROPE_THETA = 10000.0
CAPACITY_FACTOR = 2

LANES = 128
TOK_TILE = 256
HG_CHUNK = 64
HG_SUB = 32
HG_HEADS_PER_STEP = 2
HG_UNROLL = 4
KV_TILE = 256
HALO = 16
ATTN_GROUPS_PER_STEP = 4
SLOT_WINDOW = 128
SLOT_ALIGN = 16
MERGE_SAMPLES = 2
LOG2E = 1.4426950408889634
VMEM_LIMIT = 56 * 1024 * 1024

HIGHEST = lax.Precision.HIGHEST


def _dot(a, b, precision=None):
    return jnp.dot(a, b, preferred_element_type=F32, precision=precision)


def _dot_nt(a, b, precision=None):
    return lax.dot_general(a, b, (((1,), (1,)), ((), ())), preferred_element_type=F32, precision=precision)


def _dot_tn(a, b):
    return lax.dot_general(a, b, (((0,), (0,)), ((), ())), preferred_element_type=F32)


def _sigmoid(x):
    return 1.0 / (1.0 + jnp.exp(-x))


def _params(sem, vmem=VMEM_LIMIT):
    return pltpu.CompilerParams(dimension_semantics=sem, vmem_limit_bytes=vmem)


def _mod_kernel(c_ref, w_ref, b_ref, o_ref):
    c = c_ref[...]
    sc = c * _sigmoid(c)
    o_ref[0] = _dot(sc, w_ref[0], precision=HIGHEST) + b_ref[0]


def _modulation(cvec, ada_w, ada_b):
    depth, d, n = ada_w.shape
    r = cvec.shape[0]
    tn = d
    return pl.pallas_call(
        _mod_kernel,
        grid=(depth, n // tn),
        in_specs=[
            pl.BlockSpec((r, d), lambda l, j: (0, 0)),
            pl.BlockSpec((1, d, tn), lambda l, j: (l, 0, j)),
            pl.BlockSpec((1, 1, tn), lambda l, j: (l, 0, j)),
        ],
        out_specs=pl.BlockSpec((1, r, tn), lambda l, j: (l, 0, j)),
        out_shape=jax.ShapeDtypeStruct((depth, r, n), F32),
        compiler_params=_params(("parallel", "parallel")),
        name="adaln_modulation",
    )(cvec, ada_w, ada_b.reshape(depth, 1, n))


def _modulated_norm(x, g, shift, scale):
    ms = jnp.mean(x * x, axis=-1, keepdims=True)
    return (x * lax.rsqrt(ms + EPS) * g) * (1.0 + scale) + shift


def _headnorm_rope(p, gain, gmat, cos, sin, hd, post_scale):
    rows, width = p.shape
    lane = lax.broadcasted_iota(jnp.int32, (rows, LANES), 1)
    even = (lane % 2) == 0
    outs = []
    for cb in range(width // LANES):
        xb = p[:, cb * LANES:(cb + 1) * LANES]
        ss = _dot((xb * xb).astype(BF16), gmat)
        y = xb * lax.rsqrt(ss * (1.0 / hd) + EPS) * gain
        y_next = pltpu.roll(y, LANES - 1, axis=1)
        y_prev = pltpu.roll(y, 1, axis=1)
        ysw = jnp.where(even, y_next, y_prev)
        outs.append((y * cos + ysw * sin) * post_scale)
    return jnp.concatenate(outs, axis=1) if len(outs) > 1 else outs[0]


def _lower_bounds(lg, layer):
    e = jnp.exp(lg - jnp.max(lg, axis=0, keepdims=True))
    sm = e / jnp.sum(e, axis=0, keepdims=True)
    lb = jnp.zeros(lg.shape[1:], F32)
    for j in range(1, layer + 1):
        lb = lb + sm[j]
    return lb


def _forget_gate(z, lbd, lb_is_zero):
    zs = z * LOG2E
    sp = jnp.log2(1.0 + jnp.exp2(-jnp.abs(zs)))
    ls = jnp.minimum(zs, 0.0) - sp
    sneg = jnp.exp2(jnp.minimum(-zs, 0.0) - sp)
    if lb_is_zero:
        return ls, sneg
    a = jnp.log2(lbd)
    t = jnp.log2(1.0 - lbd) + ls
    mx = jnp.maximum(a, t)
    mn = jnp.minimum(a, t)
    return mx + jnp.log2(1.0 + jnp.exp2(mn - mx)), (1.0 - lbd) * sneg


def _stream_specs(stream, nct, block_rows, ns=1):
    _, _, lat_off = stream
    d = stream[0].shape[2]
    return [pl.BlockSpec((ns, block_rows, d), lambda bi, i: (bi, jnp.minimum(i, nct - 1), 0)),
            pl.BlockSpec((ns, block_rows, d), lambda bi, i: (bi, jnp.maximum(i, nct) - lat_off, 0))]


def _inproj_kernel(xa_ref, xb_ref, mod_ref, g_ref, w_ref, cos_ref, sin_ref, qg_ref, kg_ref, gmat_ref, lbl_ref,
                   hq_ref, lff_ref, lfb_ref, kkf_ref, kkb_ref, hi_ref, hog_ref, cb_ref, cu_ref, aq_ref, ak_ref,
                   av_ref, ga_ref, gb_ref, gc_ref, *, d, hgw, cw, aw, kvw, hd, layer, nct):
    x = jnp.where(pl.program_id(1) < nct, xa_ref[0], xb_ref[0])
    m = mod_ref[0]
    h = _modulated_norm(x, g_ref[...], m[:, 0:d], m[:, d:2 * d]).astype(BF16)

    def proj(lo, width):
        return _dot(h, w_ref[:, lo:lo + width])

    lb = _lower_bounds(lbl_ref[...], layer)
    o = 0
    hq_ref[0] = proj(o, hgw).astype(BF16); o += hgw
    lf, kk = _forget_gate(proj(o, hgw), lb[0:1, :], layer == 0); o += hgw
    lff_ref[0] = _chunk_cumsum(lf, False)
    kkf_ref[0] = kk.astype(BF16)
    lf, kk = _forget_gate(proj(o, hgw), lb[1:2, :], layer == 0); o += hgw
    lfb_ref[0] = _chunk_cumsum(lf, True)
    kkb_ref[0] = kk.astype(BF16)
    hi_ref[0] = proj(o, hgw).astype(BF16); o += hgw
    g = proj(o, hgw); o += hgw
    hog_ref[0] = (g * _sigmoid(g)).astype(BF16)
    cb_ref[0] = proj(o, cw).astype(BF16); o += cw
    cc = proj(o, cw); o += cw
    cx = proj(o, cw); o += cw
    cu_ref[0] = (cc * cx).astype(BF16)
    cos = cos_ref[...]
    sin = sin_ref[...]
    gmat = gmat_ref[...]
    q = proj(o, aw); o += aw
    aq_ref[0] = _headnorm_rope(q, qg_ref[...], gmat, cos, sin, hd, hd ** -0.5 * LOG2E).astype(BF16)
    k = proj(o, kvw); o += kvw
    kn = _headnorm_rope(k, kg_ref[...], gmat, cos, sin, hd, 1.0).astype(BF16)
    vv = proj(o, kvw).astype(BF16); o += kvw
    ones_col = jnp.where(lax.broadcasted_iota(jnp.int32, (vv.shape[0], hd), 1) == 0, 1.0, 0.0).astype(BF16)
    for g in range(kvw // hd):
        ak_ref[0, g] = kn[:, g * hd:(g + 1) * hd]
        av_ref[0, g] = jnp.concatenate([vv[:, g * hd:(g + 1) * hd], ones_col], axis=1)
    ga_ref[0] = _sigmoid(proj(o, d)).astype(BF16); o += d
    gb_ref[0] = _sigmoid(proj(o, d)).astype(BF16); o += d
    gc_ref[0] = _sigmoid(proj(o, d)).astype(BF16); o += d


def _in_projection(stream, mod_l, norm_g, w_bf, cos_t, sin_t, qg, kg, gmat, lb_logits, layer, dims):
    b, _, d = stream[0].shape
    t = dims["t"]
    hgw, cw, aw, kvw, hd, nct = dims["hgw"], dims["cw"], dims["aw"], dims["kvw"], dims["hd"], dims["nct"]
    nt = t // TOK_TILE
    in_w = w_bf.shape[1]
    nb = mod_l.shape[0] - 1

    def tok(width):
        return pl.BlockSpec((1, TOK_TILE, width), lambda bi, i: (bi, i, 0))

    def const(shape):
        return pl.BlockSpec(shape, lambda bi, i: (0,) * len(shape))

    widths = [(hgw, BF16), (hgw, F32), (hgw, F32), (hgw, BF16), (hgw, BF16), (hgw, BF16), (hgw, BF16),
              (cw, BF16), (cw, BF16), (aw, BF16), (kvw, BF16), (kvw, BF16), (d, BF16), (d, BF16), (d, BF16)]
    ng = kvw // hd
    out_specs = [tok(w) for w, _ in widths]
    out_shape = [jax.ShapeDtypeStruct((b, t, w), dt) for w, dt in widths]
    for idx, wid in ((10, hd), (11, 2 * hd)):
        out_specs[idx] = pl.BlockSpec((1, ng, TOK_TILE, wid), lambda bi, i: (bi, 0, i, 0))
        out_shape[idx] = jax.ShapeDtypeStruct((b, ng, t, wid), BF16)
    return pl.pallas_call(
        functools.partial(_inproj_kernel, d=d, hgw=hgw, cw=cw, aw=aw, kvw=kvw, hd=hd, layer=layer, nct=nct),
        grid=(b, nt),
        in_specs=_stream_specs(stream, nct, TOK_TILE) + [
            pl.BlockSpec((1, 1, N_MOD * d), lambda bi, i: (jnp.where(i < nct, nb, bi), 0, 0)),
            const((1, d)),
            pl.BlockSpec((d, in_w), lambda bi, i: (0, 0), pipeline_mode=pl.Buffered(1)),
            pl.BlockSpec((TOK_TILE, LANES), lambda bi, i: (i, 0)),
            pl.BlockSpec((TOK_TILE, LANES), lambda bi, i: (i, 0)),
            const((1, LANES)),
            const((1, LANES)),
            const((LANES, LANES)),
            const(lb_logits.shape),
        ],
        out_specs=out_specs,
        out_shape=out_shape,
        compiler_params=_params(("parallel", "parallel")),
        name="in_projection",
    )(stream[0], stream[1], mod_l, norm_g, w_bf, cos_t, sin_t, qg, kg, gmat, lb_logits)


def _chunk_cumsum(x, reverse):
    n = x.shape[0]
    pos = lax.broadcasted_iota(jnp.int32, x.shape, 0) % HG_CHUNK
    s = 1
    while s < HG_CHUNK:
        if reverse:
            x = x + jnp.where(pos < HG_CHUNK - s, pltpu.roll(x, n - s, axis=0), 0.0)
        else:
            x = x + jnp.where(pos >= s, pltpu.roll(x, s, axis=0), 0.0)
        s *= 2
    return x


def _hgrn_chunk(q, cs, kk, v, st_ref, reverse):
    c = q.shape[0]
    nsb = c // HG_SUB
    anchors = []
    for i in range(nsb):
        r = i * HG_SUB + (HG_SUB // 2 if reverse else HG_SUB // 2 - 1)
        anchors.append(cs[r:r + 1, :])
    c_anchor = jnp.concatenate([jnp.broadcast_to(a, (HG_SUB, a.shape[1])) for a in anchors], axis=0)
    c_end = cs[0:1, :] if reverse else cs[c - 1:c, :]
    qh = q * jnp.exp2(cs - c_anchor)

    def seg_rows(i, rows_of):
        return jnp.concatenate([rows_of(j) if keep_j else jnp.zeros((HG_SUB, q.shape[1]), F32)
                                for j, keep_j in enumerate(i)], axis=0)

    q_ext = jnp.concatenate(
        [seg_rows([j == i for j in range(nsb)], lambda j: qh[j * HG_SUB:(j + 1) * HG_SUB]) for i in range(nsb)],
        axis=1)
    k_ext = jnp.concatenate(
        [seg_rows([(j >= i) if reverse else (j <= i) for j in range(nsb)],
                  lambda j, i=i: kk[j * HG_SUB:(j + 1) * HG_SUB]
                  * jnp.exp2(anchors[i] - cs[j * HG_SUB:(j + 1) * HG_SUB])) for i in range(nsb)],
        axis=1)
    a = _dot_nt(q_ext.astype(BF16), k_ext.astype(BF16))
    row = lax.broadcasted_iota(jnp.int32, (c, c), 0)
    col = lax.broadcasted_iota(jnp.int32, (c, c), 1)
    keep = (col >= row) if reverse else (col <= row)
    st = st_ref[...]
    o = (_dot(jnp.where(keep, a, 0.0).astype(BF16), v)
         + _dot_nt((qh * jnp.exp2(c_anchor)).astype(BF16), st.astype(BF16)))
    kh = (kk * jnp.exp2(c_end - cs)).astype(BF16)
    st_ref[...] = st * jnp.exp2(c_end) + _dot_tn(v, kh)
    return o


def _hgrn_kernel(hq_ref, csf_ref, csb_ref, kkf_ref, kkb_ref, hi_ref, hog_ref, ng_ref, a_ref,
                 of_ref, ob_ref, stf_ref, stb_ref, *, n_chunks, n_ctx_chunks, n_heads):
    stf_ref[...] = jnp.zeros_like(stf_ref)
    stb_ref[...] = jnp.zeros_like(stb_ref)
    c = HG_CHUNK

    def step(s, carry):
        rf = pl.multiple_of(s * c, c)
        cbk = jnp.where(s < n_ctx_chunks, n_ctx_chunks - 1 - s, n_chunks - 1 - (s - n_ctx_chunks))
        rb = pl.multiple_of(cbk * c, c)
        for h in range(n_heads):
            ln = slice(h * HG_DK, (h + 1) * HG_DK)
            of_ref[pl.ds(rf, c), ln] = _hgrn_chunk(
                hq_ref[0, pl.ds(rf, c), ln].astype(F32), csf_ref[0, pl.ds(rf, c), ln],
                kkf_ref[0, pl.ds(rf, c), ln].astype(F32), hi_ref[0, pl.ds(rf, c), ln], stf_ref.at[h], False)
            ob_ref[pl.ds(rb, c), ln] = _hgrn_chunk(
                hq_ref[0, pl.ds(rb, c), ln].astype(F32), csb_ref[0, pl.ds(rb, c), ln],
                kkb_ref[0, pl.ds(rb, c), ln].astype(F32), hi_ref[0, pl.ds(rb, c), ln], stb_ref.at[h], True)
        return carry

    lax.fori_loop(0, n_chunks, step, 0, unroll=HG_UNROLL)

    def readout(i, carry):
        r0 = pl.multiple_of(i * TOK_TILE, TOK_TILE)
        for h in range(n_heads):
            ln = slice(h * HG_DK, (h + 1) * HG_DK)
            o = of_ref[pl.ds(r0, TOK_TILE), ln] + ob_ref[pl.ds(r0, TOK_TILE), ln]
            ms = jnp.mean(o * o, axis=-1, keepdims=True)
            y = o * lax.rsqrt(ms + EPS) * ng_ref[:, ln]
            a_ref[0, pl.ds(r0, TOK_TILE), ln] = (y * hog_ref[0, pl.ds(r0, TOK_TILE), ln].astype(F32)).astype(BF16)
        return carry

    lax.fori_loop(0, (n_chunks * c) // TOK_TILE, readout, 0)


def _hgrn(hq, csf, csb, kkf, kkb, hi, hog, norm_g, ctx_len):
    b, t, hgw = hq.shape
    nh = HG_HEADS_PER_STEP
    wid = nh * HG_DK

    def seq():
        return pl.BlockSpec((1, t, wid), lambda bi, h: (bi, 0, h))

    return pl.pallas_call(
        functools.partial(_hgrn_kernel, n_chunks=t // HG_CHUNK, n_ctx_chunks=ctx_len // HG_CHUNK, n_heads=nh),
        grid=(b, hgw // wid),
        in_specs=[seq(), seq(), seq(), seq(), seq(), seq(), seq(),
                  pl.BlockSpec((1, wid), lambda bi, h: (0, h))],
        out_specs=seq(),
        out_shape=jax.ShapeDtypeStruct((b, t, hgw), BF16),
        scratch_shapes=[pltpu.VMEM((t, wid), F32), pltpu.VMEM((t, wid), F32),
                        pltpu.VMEM((nh, HG_DK, HG_DK), F32), pltpu.VMEM((nh, HG_DK, HG_DK), F32)],
        compiler_params=_params(("parallel", "parallel")),
        name="hgrn2_bidirectional",
    )(hq, csf, csb, kkf, kkb, hi, hog, norm_g)


def _attn_kernel(q_ref, k_ref, v_ref, o_ref, *, n_g, group, hd, nct, ctx_len, t_all):
    i = pl.program_id(2)

    def run(n_keys):
        outs = []
        for g in range(n_g):
            k = k_ref[0, g, 0:n_keys, :]
            v = v_ref[0, g, 0:n_keys, :]
            for h in range(group):
                c0 = (g * group + h) * hd
                s = _dot_nt(q_ref[0, :, c0:c0 + hd], k)
                p = jnp.exp2(s - jnp.max(s, axis=1, keepdims=True))
                ov = _dot(p.astype(BF16), v)
                outs.append((ov[:, 0:hd] / ov[:, hd:hd + 1]).astype(BF16))
        o_ref[0] = jnp.concatenate(outs, axis=1)

    @pl.when(i < nct)
    def _():
        run(ctx_len)

    @pl.when(i >= nct)
    def _():
        run(t_all)


def _attention(aq, ak, av, ctx_len):
    b, t, aw = aq.shape
    n_groups, hd = ak.shape[1], ak.shape[3]
    group = aw // (n_groups * hd)
    nt = t // TOK_TILE
    n_g = ATTN_GROUPS_PER_STEP if n_groups % ATTN_GROUPS_PER_STEP == 0 else 1
    return pl.pallas_call(
        functools.partial(_attn_kernel, n_g=n_g, group=group, hd=hd, nct=ctx_len // TOK_TILE, ctx_len=ctx_len,
                          t_all=t),
        grid=(b, n_groups // n_g, nt),
        in_specs=[
            pl.BlockSpec((1, TOK_TILE, n_g * group * hd), lambda bi, g, i: (bi, i, g)),
            pl.BlockSpec((1, n_g, t, hd), lambda bi, g, i: (bi, g, 0, 0)),
            pl.BlockSpec((1, n_g, t, 2 * hd), lambda bi, g, i: (bi, g, 0, 0)),
        ],
        out_specs=pl.BlockSpec((1, TOK_TILE, n_g * group * hd), lambda bi, g, i: (bi, i, g)),
        out_shape=jax.ShapeDtypeStruct((b, t, aw), BF16),
        compiler_params=_params(("parallel", "parallel", "parallel")),
        name="gqa_attention",
    )(aq, ak, av)


def _merge_kernel(xa_ref, xb_ref, mod_ref, a_ref, cb_ref, cu_ref, cup_ref, cun_ref, att_ref, ga_ref, gb_ref, gc_ref,
                  cw_ref, wa_ref, wb_ref, wc_ref, wo_ref, g2_ref, rwt_ref,
                  x1_ref, h2_ref, aff_ref, *, d, nct, nt):
    i = pl.program_id(1)
    ns, rows = xa_ref.shape[0], xa_ref.shape[1]
    n = ns * rows

    def stacked(ref):
        return ref[...].reshape(n, ref.shape[2])

    row = lax.broadcasted_iota(jnp.int32, (n, 1), 0)

    def per_sample(lo):
        out = mod_ref[ns - 1][:, lo:lo + d]
        for s in range(ns - 2, -1, -1):
            out = jnp.where(row < (s + 1) * rows, mod_ref[s][:, lo:lo + d], out)
        return out

    u = stacked(cu_ref).astype(F32)
    has_prev = jnp.logical_and(i != 0, i != nct)
    has_next = jnp.logical_and(i != nct - 1, i != nt - 1)
    u_prev = pltpu.roll(u, 1, axis=0)
    u_next = pltpu.roll(u, n - 1, axis=0)
    for s in range(ns):
        prev_row = jnp.where(has_prev, cup_ref[s, HALO - 1:HALO, :].astype(F32), 0.0)
        next_row = jnp.where(has_next, cun_ref[s, 0:1, :].astype(F32), 0.0)
        u_prev = jnp.where(row == s * rows, prev_row, u_prev)
        u_next = jnp.where(row == (s + 1) * rows - 1, next_row, u_next)
    cw = cw_ref[...]
    conv = cw[0:1, :] * u_prev + cw[1:2, :] * u + cw[2:3, :] * u_next
    bb = (stacked(cb_ref).astype(F32) * conv).astype(BF16)
    y = (stacked(ga_ref).astype(F32) * _dot(stacked(a_ref), wa_ref[...])
         + stacked(gb_ref).astype(F32) * _dot(bb, wb_ref[...])
         + stacked(gc_ref).astype(F32) * _dot(stacked(att_ref), wc_ref[...]))
    x = jnp.where(i < nct, stacked(xa_ref), stacked(xb_ref))
    x1 = x + per_sample(2 * d) * _dot(y.astype(BF16), wo_ref[...])
    x1_ref[...] = x1.reshape(ns, rows, d)
    h2 = _modulated_norm(x1, g2_ref[...], per_sample(3 * d), per_sample(4 * d))
    h2_ref[...] = h2.astype(BF16).reshape(ns, rows, d)
    logits = _dot_nt(rwt_ref[...], h2, precision=HIGHEST)
    ex = jnp.exp(logits - jnp.max(logits, axis=0, keepdims=True))
    aff = ex / jnp.sum(ex, axis=0, keepdims=True)
    for s in range(ns):
        aff_ref[s] = aff[:, s * rows:(s + 1) * rows]


def _merge(stream, mod_l, a, cb, cu, att, ga, gb, gc, conv_w, wa, wb, wc, wo, norm2_g, rwt, nct):
    b, t, d = a.shape[0], a.shape[1], stream[0].shape[2]
    nt = t // TOK_TILE
    ne = rwt.shape[0]
    ns = MERGE_SAMPLES if b % MERGE_SAMPLES == 0 else 1
    mod_m = jnp.concatenate([mod_l[:b]] + [mod_l[b:b + 1]] * ns, axis=0)
    ctx_blk = b // ns
    sub = TOK_TILE // HALO
    n8 = t // HALO

    def tok(width):
        return pl.BlockSpec((ns, TOK_TILE, width), lambda bi, i: (bi, i, 0))

    def const(shape):
        return pl.BlockSpec(shape, lambda bi, i: (0,) * len(shape))

    cwid = cu.shape[2]
    return pl.pallas_call(
        functools.partial(_merge_kernel, d=d, nct=nct, nt=nt),
        grid=(b // ns, nt),
        in_specs=_stream_specs(stream, nct, TOK_TILE, ns) + [
            pl.BlockSpec((ns, 1, N_MOD * d), lambda bi, i: (jnp.where(i < nct, ctx_blk, bi), 0, 0)),
            tok(a.shape[2]), tok(cwid), tok(cwid),
            pl.BlockSpec((ns, HALO, cwid), lambda bi, i: (bi, jnp.maximum(i * sub - 1, 0), 0)),
            pl.BlockSpec((ns, HALO, cwid), lambda bi, i: (bi, jnp.minimum((i + 1) * sub, n8 - 1), 0)),
            tok(att.shape[2]), tok(d), tok(d), tok(d),
            const(conv_w.shape), const(wa.shape), const(wb.shape), const(wc.shape), const(wo.shape),
            const((1, d)), const(rwt.shape),
        ],
        out_specs=[tok(d), tok(d), pl.BlockSpec((ns, ne, TOK_TILE), lambda bi, i: (bi, 0, i))],
        out_shape=[jax.ShapeDtypeStruct((b, t, d), F32), jax.ShapeDtypeStruct((b, t, d), BF16),
                   jax.ShapeDtypeStruct((b, ne, t), F32)],
        compiler_params=_params(("parallel", "parallel")),
        name="merge_residual_router",
    )(stream[0], stream[1], mod_m, a, cb, cu, cu, cu, att, ga, gb, gc, conv_w, wa, wb, wc, wo, norm2_g, rwt)


def _prefix_count(mask, tri):
    e, n = mask.shape
    carry = jnp.zeros((e, 1), F32)
    outs = []
    for blk in range(n // LANES):
        xb = jnp.where(mask[:, blk * LANES:(blk + 1) * LANES], 1.0, 0.0).astype(BF16)
        pre = _dot(xb, tri) + carry
        carry = pre[:, LANES - 1:LANES]
        outs.append(pre)
    return jnp.concatenate(outs, axis=1) if len(outs) > 1 else outs[0]


def _topk_kernel(aff_ref, tri_ref, tind_ref, sut_ref, pos_ref, before_ref, *, off, n, cap):
    a = aff_ref[0][:, off:off + n]
    bits = pltpu.bitcast(a, jnp.int32)
    thr = jnp.zeros((a.shape[0], 1), jnp.int32)
    for bit in range(30, -1, -1):
        cand = thr | (1 << bit)
        cnt = jnp.sum(jnp.where(bits >= cand, 1.0, 0.0), axis=1, keepdims=True)
        thr = jnp.where(cnt >= cap, cand, thr)
    gt = bits > thr
    eq = bits == thr
    need = cap - jnp.sum(jnp.where(gt, 1.0, 0.0), axis=1, keepdims=True)
    tri = tri_ref[...]
    eq_rank = _prefix_count(eq, tri)
    sel = jnp.logical_or(gt, jnp.logical_and(eq, eq_rank <= need))
    slot = _prefix_count(sel, tri) - 1.0
    pos_ref[0] = jnp.where(sel, slot, -1.0).astype(jnp.int32)
    tile_cnt = _dot(jnp.where(sel, 1.0, 0.0).astype(BF16), tind_ref[...])
    before_ref[0] = _dot(tile_cnt.astype(BF16), sut_ref[...]).astype(jnp.int32)


def _topk_positions(aff_t, tri, off, n, cap):
    b, ne, t = aff_t.shape
    assert n // TOK_TILE < LANES and TOK_TILE <= 256
    lane = jnp.arange(LANES)
    tind = (jnp.arange(n)[:, None] // TOK_TILE == lane[None, :]).astype(BF16)
    sut = (lane[:, None] < lane[None, :]).astype(BF16)

    def const(shape):
        return pl.BlockSpec(shape, lambda bi: (0,) * len(shape))

    return pl.pallas_call(
        functools.partial(_topk_kernel, off=off, n=n, cap=cap),
        grid=(b,),
        in_specs=[pl.BlockSpec((1, ne, t), lambda bi: (bi, 0, 0)), const((LANES, LANES)), const(tind.shape),
                  const(sut.shape)],
        out_specs=[pl.BlockSpec((1, ne, n), lambda bi: (bi, 0, 0)), pl.BlockSpec((1, ne, LANES), lambda bi: (bi, 0, 0))],
        out_shape=[jax.ShapeDtypeStruct((b, ne, n), jnp.int32), jax.ShapeDtypeStruct((b, ne, LANES), jnp.int32)],
        compiler_params=_params(("parallel",)),
        name="expert_choice_topk",
    )(aff_t, tri, tind, sut)


def _gather_kernel(before_ref, h_ref, pos_ref, xs_ref, *, ne, cap, win):
    bi = pl.program_id(0)
    tt = pl.program_id(1)

    @pl.when(tt == 0)
    def _():
        xs_ref[...] = jnp.zeros_like(xs_ref)

    h = h_ref[0]
    pos = pos_ref[0]
    rows = h.shape[0]
    sub = lax.broadcasted_iota(jnp.int32, (win, rows), 0)
    hots, spans = [], []
    overflow = False
    for e in range(ne):
        base = (bi * ne + e) * LANES
        c0 = before_ref[base + tt]
        c1 = before_ref[base + tt + 1]
        w0 = pl.multiple_of(jnp.minimum((c0 // SLOT_ALIGN) * SLOT_ALIGN, cap - win), SLOT_ALIGN)
        hots.append(jnp.where(pos[e:e + 1, :] - w0 == sub, 1.0, 0.0).astype(BF16))
        spans.append((c1, w0))
        overflow = jnp.logical_or(overflow, c1 > w0 + win)
    picked = _dot(jnp.concatenate(hots, axis=0), h).astype(BF16)
    for e, (c1, w0) in enumerate(spans):
        xs_ref[e, pl.ds(w0, win), :] += picked[e * win:(e + 1) * win, :]

    @pl.when(overflow)
    def _():
        for e, (c1, w0) in enumerate(spans):
            def more(k, carry, e=e, w0=w0):
                ws = w0 + k * win
                wc = pl.multiple_of(jnp.minimum(ws, cap - win), SLOT_ALIGN)
                hit = jnp.logical_and(pos_ref[0][e:e + 1, :] - wc == sub, sub >= ws - wc)
                xs_ref[e, pl.ds(wc, win), :] += _dot(jnp.where(hit, 1.0, 0.0).astype(BF16), h_ref[0]).astype(BF16)
                return carry

            lax.fori_loop(1, (c1 - w0 + win - 1) // win, more, 0)


def _gather_tokens(h2, pos, before, off, n, cap):
    b, t, d = h2.shape
    ne = pos.shape[1]
    ot = off // TOK_TILE
    win = min(cap, SLOT_WINDOW)
    grid_spec = pltpu.PrefetchScalarGridSpec(
        num_scalar_prefetch=1,
        grid=(b, n // TOK_TILE),
        in_specs=[pl.BlockSpec((1, TOK_TILE, d), lambda bi, tt, bf: (bi, tt + ot, 0)),
                  pl.BlockSpec((1, ne, TOK_TILE), lambda bi, tt, bf: (bi, 0, tt))],
        out_specs=pl.BlockSpec((ne, cap, d), lambda bi, tt, bf: (0, bi, 0)),
    )
    return pl.pallas_call(
        functools.partial(_gather_kernel, ne=ne, cap=cap, win=win),
        grid_spec=grid_spec,
        out_shape=jax.ShapeDtypeStruct((ne, b * cap, d), BF16),
        compiler_params=_params(("parallel", "arbitrary")),
        name="expert_gather",
    )(before.reshape(-1), h2, pos)


def _ffn_kernel(xs_ref, wg_ref, wu_ref, wd_ref, ys_ref, wgb_ref, wub_ref, wdb_ref):
    @pl.when(pl.program_id(1) == 0)
    def _():
        wgb_ref[...] = wg_ref[0, 0].astype(BF16)
        wub_ref[...] = wu_ref[0, 0].astype(BF16)
        wdb_ref[...] = wd_ref[0, 0].astype(BF16)

    x = xs_ref[0]
    g = _dot(x, wgb_ref[...])
    hid = (g * _sigmoid(g)) * _dot(x, wub_ref[...])
    ys_ref[0] = _dot(hid.astype(BF16), wdb_ref[...]).astype(BF16)


def _expert_ffn(xs, wg, wu, wd, layer):
    ne, m, d = xs.shape
    ff = wg.shape[3]
    tm = min(m, 512)
    return pl.pallas_call(
        _ffn_kernel,
        grid=(ne, m // tm),
        in_specs=[pl.BlockSpec((1, tm, d), lambda e, j: (e, j, 0)),
                  pl.BlockSpec((1, 1, d, ff), lambda e, j: (layer, e, 0, 0)),
                  pl.BlockSpec((1, 1, d, ff), lambda e, j: (layer, e, 0, 0)),
                  pl.BlockSpec((1, 1, ff, d), lambda e, j: (layer, e, 0, 0))],
        out_specs=pl.BlockSpec((1, tm, d), lambda e, j: (e, j, 0)),
        out_shape=jax.ShapeDtypeStruct((ne, m, d), BF16),
        scratch_shapes=[pltpu.VMEM((d, ff), BF16), pltpu.VMEM((d, ff), BF16), pltpu.VMEM((ff, d), BF16)],
        compiler_params=_params(("parallel", "arbitrary")),
        name="expert_ffn",
    )(xs, wg, wu, wd)


def _scatter_kernel(before_ref, x_ref, mod_ref, ys_ref, pos_ref, aff_ref, fg_ref, o_ref, acc_ref, *,
                    d, ne, cap, win, final):
    bi = pl.program_id(0)
    i = pl.program_id(1)
    m = mod_ref[0]
    pos = pos_ref[0]
    aff = aff_ref[0]
    rows = pos.shape[0]
    lane = lax.broadcasted_iota(jnp.int32, (rows, win), 1)
    acc = jnp.zeros((rows, d), F32)
    spans = []
    overflow = False
    for e in range(ne):
        base = (bi * ne + e) * LANES
        c0 = before_ref[base + i]
        c1 = before_ref[base + i + 1]
        w0 = pl.multiple_of(jnp.minimum((c0 // SLOT_ALIGN) * SLOT_ALIGN, cap - win), SLOT_ALIGN)
        onehot = jnp.where(pos[:, e:e + 1] - w0 == lane, 1.0, 0.0).astype(BF16)
        acc = acc + aff[:, e:e + 1] * _dot(onehot, ys_ref[e, pl.ds(w0, win), :])
        spans.append((c1, w0))
        overflow = jnp.logical_or(overflow, c1 > w0 + win)
    acc_ref[...] = acc

    @pl.when(overflow)
    def _():
        for e, (c1, w0) in enumerate(spans):
            def more(k, carry, e=e, w0=w0):
                ws = w0 + k * win
                wc = pl.multiple_of(jnp.minimum(ws, cap - win), SLOT_ALIGN)
                hit = jnp.logical_and(pos_ref[0][:, e:e + 1] - wc == lane, lane >= ws - wc)
                acc_ref[...] += aff_ref[0][:, e:e + 1] * _dot(jnp.where(hit, 1.0, 0.0).astype(BF16),
                                                              ys_ref[e, pl.ds(wc, win), :])
                return carry

            lax.fori_loop(1, (c1 - w0 + win - 1) // win, more, 0)

    x2 = x_ref[0] + m[:, 5 * d:6 * d] * acc_ref[...]
    if final:
        ms = jnp.mean(x2 * x2, axis=-1, keepdims=True)
        x2 = x2 * lax.rsqrt(ms + EPS) * fg_ref[...]
    o_ref[0] = x2


def _scatter_residual(x1, mod_l, ys, pos_n, aff_n, before, final_g, off, n, cap, mod_row_ctx, final):
    b, t, d = x1.shape
    ne = ys.shape[0]
    nb = mod_l.shape[0] - 1
    ot = off // TOK_TILE
    win = min(cap, SLOT_WINDOW)
    if final:
        out_shape = jax.ShapeDtypeStruct((b, n, d), F32)
        out_spec = pl.BlockSpec((1, TOK_TILE, d), lambda bi, i, bf: (bi, i, 0))
        aliases = {}
    else:
        out_shape = jax.ShapeDtypeStruct((b, t, d), F32)
        out_spec = pl.BlockSpec((1, TOK_TILE, d), lambda bi, i, bf: (bi, i + ot, 0))
        aliases = {1: 0}
    grid_spec = pltpu.PrefetchScalarGridSpec(
        num_scalar_prefetch=1,
        grid=(b, n // TOK_TILE),
        in_specs=[
            pl.BlockSpec((1, TOK_TILE, d), lambda bi, i, bf: (bi, i + ot, 0)),
            pl.BlockSpec((1, 1, N_MOD * d), lambda bi, i, bf: (nb if mod_row_ctx else bi, 0, 0)),
            pl.BlockSpec((ne, cap, d), lambda bi, i, bf: (0, bi, 0)),
            pl.BlockSpec((1, TOK_TILE, ne), lambda bi, i, bf: (bi, i, 0)),
            pl.BlockSpec((1, TOK_TILE, ne), lambda bi, i, bf: (bi, i + ot, 0)),
            pl.BlockSpec((1, d), lambda bi, i, bf: (0, 0)),
        ],
        out_specs=out_spec,
        scratch_shapes=[pltpu.VMEM((TOK_TILE, d), F32)],
    )
    return pl.pallas_call(
        functools.partial(_scatter_kernel, d=d, ne=ne, cap=cap, win=win, final=final),
        grid_spec=grid_spec,
        out_shape=out_shape,
        input_output_aliases=aliases,
        compiler_params=_params(("parallel", "parallel")),
        name="expert_scatter_residual",
    )(before.reshape(-1), x1, mod_l, ys, pos_n, aff_n, final_g)


def _rope_tables(ctx_len, seq, hd):
    rows = seq // GRID_W
    row = jnp.repeat(jnp.arange(rows), GRID_W).astype(F32)
    col = jnp.tile(jnp.arange(GRID_W), rows).astype(F32)
    inv = ROPE_THETA ** (-jnp.arange(0, hd // 2, 2, dtype=F32) / (hd // 2))
    ang = jnp.concatenate([row[:, None] * inv, col[:, None] * inv], axis=-1)
    cos = jnp.repeat(jnp.cos(ang), 2, axis=-1)
    sin = jnp.repeat(jnp.sin(ang), 2, axis=-1) * jnp.tile(jnp.array([-1.0, 1.0], F32), hd // 2)
    cos = jnp.concatenate([jnp.ones((ctx_len, hd), F32), cos], axis=0)
    sin = jnp.concatenate([jnp.zeros((ctx_len, hd), F32), sin], axis=0)
    rep = LANES // hd
    return jnp.tile(cos, (1, rep)), jnp.tile(sin, (1, rep))


def kernel(x, c, ctx, c_ctx, ada_w, ada_b, norm1_g, norm2_g, w_in, hg_lb_logits, hg_norm_g, conv_w, q_norm_g, k_norm_g, w_proj_a, w_proj_b, w_proj_c, w_out, router_w, w_gate, w_up, w_down, final_norm_g):
    b, s, d = x.shape
    ctx_len = ctx.shape[1]
    t = ctx_len + s
    depth = w_in.shape[0]
    hgw = hg_norm_g.shape[1]
    cw = conv_w.shape[2]
    aw = w_proj_c.shape[1]
    hd = q_norm_g.shape[1]
    kvw = (w_in.shape[2] - 5 * hgw - 3 * cw - aw - 3 * d) // 2
    ne = router_w.shape[2]
    nct = ctx_len // TOK_TILE
    assert ctx_len % TOK_TILE == 0 and s % TOK_TILE == 0 and LANES % hd == 0 and cw == d
    dims = dict(hgw=hgw, cw=cw, aw=aw, kvw=kvw, hd=hd, nct=nct, t=t)

    stream = (ctx, x, nct)
    n_rows = -(-(b + 1) // 8) * 8
    cvec = jnp.concatenate([c, c_ctx[None, :], jnp.zeros((n_rows - b - 1, d), F32)], axis=0)
    mod = _modulation(cvec, ada_w, ada_b)
    mod = mod[:, :b + 1].reshape(depth, b + 1, 1, N_MOD * d)

    cos_t, sin_t = _rope_tables(ctx_len, s, hd)
    lane = jnp.arange(LANES)
    gmat = (lane[:, None] // hd == lane[None, :] // hd).astype(BF16)
    tri = (lane[:, None] <= lane[None, :]).astype(BF16)
    rep = LANES // hd
    lb_logits = hg_lb_logits.astype(F32)

    out = None
    for l in range(depth):
        last = l == depth - 1
        mod_l = mod[l]
        (hq, lff, lfb, kkf, kkb, hi, hog, cb, cu, aq, ak, av, ga, gb, gc) = _in_projection(
            stream, mod_l, norm1_g[l][None, :], w_in[l].astype(BF16), cos_t, sin_t,
            jnp.tile(q_norm_g[l], rep)[None, :], jnp.tile(k_norm_g[l], rep)[None, :], gmat, lb_logits, l, dims)
        a = _hgrn(hq, lff, lfb, kkf, kkb, hi, hog, hg_norm_g[l][None, :], ctx_len)
        att = _attention(aq, ak, av, ctx_len)
        x1, h2, aff_t = _merge(stream, mod_l, a, cb, cu, att, ga, gb, gc, conv_w[l],
                               w_proj_a[l].astype(BF16), w_proj_b[l].astype(BF16), w_proj_c[l].astype(BF16),
                               w_out[l].astype(BF16), norm2_g[l][None, :], router_w[l].T, nct)
        aff_n = jnp.swapaxes(aff_t, 1, 2)

        def moe(xs_in, off, n, mod_row_ctx, final):
            cap = CAPACITY_FACTOR * n // ne
            pos, before = _topk_positions(aff_t, tri, off, n, cap)
            xs = _gather_tokens(h2, pos, before, off, n, cap)
            ys = _expert_ffn(xs, w_gate, w_up, w_down, l)
            return _scatter_residual(xs_in, mod_l, ys, jnp.swapaxes(pos, 1, 2), aff_n, before,
                                     final_norm_g[None, :], off, n, cap, mod_row_ctx, final)

        if last:
            out = moe(x1, ctx_len, s, False, True)
        else:
            xc = moe(x1, ctx_len, s, False, False)
            xc = moe(xc, 0, ctx_len, True, False)
            stream = (xc, xc, 0)
    return out
```

```python
import functools

import jax
import jax.numpy as jnp
from jax import lax
from jax.experimental import pallas as pl
from jax.experimental.pallas import tpu as pltpu

F32 = jnp.float32
BF16 = jnp.bfloat16

EPS = 1e-6
N_MOD = 6
HG_DK = 128
GRID_W = 64
ROPE_THETA = 10000.0
CAPACITY_FACTOR = 2

LANES = 128
TOK_TILE = 256
HG_CHUNK = 64
HG_SUB = 32
HG_HEADS_PER_STEP = 2
HG_UNROLL = 4
KV_TILE = 256
HALO = 16
ATTN_GROUPS_PER_STEP = 4
SLOT_WINDOW = 128
SLOT_ALIGN = 16
MERGE_SAMPLES = 2
LOG2E = 1.4426950408889634
VMEM_LIMIT = 56 * 1024 * 1024

HIGHEST = lax.Precision.HIGHEST


def _dot(a, b, precision=None):
    return jnp.dot(a, b, preferred_element_type=F32, precision=precision)


def _dot_nt(a, b, precision=None):
    return lax.dot_general(a, b, (((1,), (1,)), ((), ())), preferred_element_type=F32, precision=precision)


def _dot_tn(a, b):
    return lax.dot_general(a, b, (((0,), (0,)), ((), ())), preferred_element_type=F32)


def _sigmoid(x):
    return 1.0 / (1.0 + jnp.exp(-x))


def _params(sem, vmem=VMEM_LIMIT):
    return pltpu.CompilerParams(dimension_semantics=sem, vmem_limit_bytes=vmem)


def _mod_kernel(c_ref, w_ref, b_ref, o_ref):
    c = c_ref[...]
    sc = c * _sigmoid(c)
    o_ref[0] = _dot(sc, w_ref[0], precision=HIGHEST) + b_ref[0]


def _modulation(cvec, ada_w, ada_b):
    depth, d, n = ada_w.shape
    r = cvec.shape[0]
    tn = d
    return pl.pallas_call(
        _mod_kernel,
        grid=(depth, n // tn),
        in_specs=[
            pl.BlockSpec((r, d), lambda l, j: (0, 0)),
            pl.BlockSpec((1, d, tn), lambda l, j: (l, 0, j)),
            pl.BlockSpec((1, 1, tn), lambda l, j: (l, 0, j)),
        ],
        out_specs=pl.BlockSpec((1, r, tn), lambda l, j: (l, 0, j)),
        out_shape=jax.ShapeDtypeStruct((depth, r, n), F32),
        compiler_params=_params(("parallel", "parallel")),
        name="adaln_modulation",
    )(cvec, ada_w, ada_b.reshape(depth, 1, n))


def _modulated_norm(x, g, shift, scale):
    ms = jnp.mean(x * x, axis=-1, keepdims=True)
    return (x * lax.rsqrt(ms + EPS) * g) * (1.0 + scale) + shift


def _headnorm_rope(p, gain, gmat, cos, sin, hd, post_scale):
    rows, width = p.shape
    lane = lax.broadcasted_iota(jnp.int32, (rows, LANES), 1)
    even = (lane % 2) == 0
    outs = []
    for cb in range(width // LANES):
        xb = p[:, cb * LANES:(cb + 1) * LANES]
        ss = _dot((xb * xb).astype(BF16), gmat)
        y = xb * lax.rsqrt(ss * (1.0 / hd) + EPS) * gain
        y_next = pltpu.roll(y, LANES - 1, axis=1)
        y_prev = pltpu.roll(y, 1, axis=1)
        ysw = jnp.where(even, y_next, y_prev)
        outs.append((y * cos + ysw * sin) * post_scale)
    return jnp.concatenate(outs, axis=1) if len(outs) > 1 else outs[0]


def _lower_bounds(lg, layer):
    e = jnp.exp(lg - jnp.max(lg, axis=0, keepdims=True))
    sm = e / jnp.sum(e, axis=0, keepdims=True)
    lb = jnp.zeros(lg.shape[1:], F32)
    for j in range(1, layer + 1):
        lb = lb + sm[j]
    return lb


def _forget_gate(z, lbd, lb_is_zero):
    zs = z * LOG2E
    sp = jnp.log2(1.0 + jnp.exp2(-jnp.abs(zs)))
    ls = jnp.minimum(zs, 0.0) - sp
    sneg = jnp.exp2(jnp.minimum(-zs, 0.0) - sp)
    if lb_is_zero:
        return ls, sneg
    a = jnp.log2(lbd)
    t = jnp.log2(1.0 - lbd) + ls
    mx = jnp.maximum(a, t)
    mn = jnp.minimum(a, t)
    return mx + jnp.log2(1.0 + jnp.exp2(mn - mx)), (1.0 - lbd) * sneg


def _stream_specs(stream, nct, block_rows, ns=1):
    _, _, lat_off = stream
    d = stream[0].shape[2]
    return [pl.BlockSpec((ns, block_rows, d), lambda bi, i: (bi, jnp.minimum(i, nct - 1), 0)),
            pl.BlockSpec((ns, block_rows, d), lambda bi, i: (bi, jnp.maximum(i, nct) - lat_off, 0))]


def _inproj_kernel(xa_ref, xb_ref, mod_ref, g_ref, w_ref, cos_ref, sin_ref, qg_ref, kg_ref, gmat_ref, lbl_ref,
                   hq_ref, lff_ref, lfb_ref, kkf_ref, kkb_ref, hi_ref, hog_ref, cb_ref, cu_ref, aq_ref, ak_ref,
                   av_ref, ga_ref, gb_ref, gc_ref, *, d, hgw, cw, aw, kvw, hd, layer, nct):
    x = jnp.where(pl.program_id(1) < nct, xa_ref[0], xb_ref[0])
    m = mod_ref[0]
    h = _modulated_norm(x, g_ref[...], m[:, 0:d], m[:, d:2 * d]).astype(BF16)

    def proj(lo, width):
        return _dot(h, w_ref[:, lo:lo + width])

    lb = _lower_bounds(lbl_ref[...], layer)
    o_hq, o_zf, o_zb, o_hi, o_hog = (j * hgw for j in range(5))
    o_cb, o_cc, o_cx = (5 * hgw + j * cw for j in range(3))
    o_q = 5 * hgw + 3 * cw
    o_k, o_v = o_q + aw, o_q + aw + kvw
    o_ga, o_gb, o_gc = (o_v + kvw + j * d for j in range(3))
    cos = cos_ref[...]
    sin = sin_ref[...]
    gmat = gmat_ref[...]

    def forget(o, dirn, lf_ref, kk_ref):
        lf, kk = _forget_gate(proj(o, hgw), lb[dirn:dirn + 1, :], layer == 0)
        lf_ref[0] = _chunk_cumsum(lf, dirn == 1)
        kk_ref[0] = kk.astype(BF16)

    def gate(o, ref):
        ref[0] = _sigmoid(proj(o, d)).astype(BF16)

    hq_ref[0] = proj(o_hq, hgw).astype(BF16)
    forget(o_zf, 0, lff_ref, kkf_ref)
    forget(o_zb, 1, lfb_ref, kkb_ref)
    hi_ref[0] = proj(o_hi, hgw).astype(BF16)
    g = proj(o_hog, hgw)
    hog_ref[0] = (g * _sigmoid(g)).astype(BF16)
    cb_ref[0] = proj(o_cb, cw).astype(BF16)
    cu_ref[0] = (proj(o_cc, cw) * proj(o_cx, cw)).astype(BF16)
    aq_ref[0] = _headnorm_rope(proj(o_q, aw), qg_ref[...], gmat, cos, sin, hd, hd ** -0.5 * LOG2E).astype(BF16)
    kn_t = _headnorm_rope(proj(o_k, kvw), kg_ref[...], gmat, cos, sin, hd, 1.0).T.astype(BF16)
    vv = proj(o_v, kvw).astype(BF16)
    ones_col = jnp.where(lax.broadcasted_iota(jnp.int32, (vv.shape[0], hd), 1) == 0, 1.0, 0.0).astype(BF16)
    for g in range(kvw // hd):
        ak_ref[0, g] = kn_t[g * hd:(g + 1) * hd, :]
        av_ref[0, g] = jnp.concatenate([vv[:, g * hd:(g + 1) * hd], ones_col], axis=1)
    gate(o_ga, ga_ref)
    gate(o_gb, gb_ref)
    gate(o_gc, gc_ref)


def _in_projection(stream, mod_l, norm_g, w_bf, cos_t, sin_t, qg, kg, gmat, lb_logits, layer, dims):
    b, _, d = stream[0].shape
    t = dims["t"]
    hgw, cw, aw, kvw, hd, nct = dims["hgw"], dims["cw"], dims["aw"], dims["kvw"], dims["hd"], dims["nct"]
    nt = t // TOK_TILE
    in_w = w_bf.shape[1]
    nb = mod_l.shape[0] - 1

    def tok(width):
        return pl.BlockSpec((1, TOK_TILE, width), lambda bi, i: (bi, i, 0))

    def const(shape):
        return pl.BlockSpec(shape, lambda bi, i: (0,) * len(shape))

    widths = [(hgw, BF16), (hgw, F32), (hgw, F32), (hgw, BF16), (hgw, BF16), (hgw, BF16), (hgw, BF16),
              (cw, BF16), (cw, BF16), (aw, BF16), (kvw, BF16), (kvw, BF16), (d, BF16), (d, BF16), (d, BF16)]
    ng = kvw // hd
    out_specs = [tok(w) for w, _ in widths]
    out_shape = [jax.ShapeDtypeStruct((b, t, w), dt) for w, dt in widths]
    out_specs[10] = pl.BlockSpec((1, ng, hd, TOK_TILE), lambda bi, i: (bi, 0, 0, i))
    out_shape[10] = jax.ShapeDtypeStruct((b, ng, hd, t), BF16)
    out_specs[11] = pl.BlockSpec((1, ng, TOK_TILE, 2 * hd), lambda bi, i: (bi, 0, i, 0))
    out_shape[11] = jax.ShapeDtypeStruct((b, ng, t, 2 * hd), BF16)
    return pl.pallas_call(
        functools.partial(_inproj_kernel, d=d, hgw=hgw, cw=cw, aw=aw, kvw=kvw, hd=hd, layer=layer, nct=nct),
        grid=(b, nt),
        in_specs=_stream_specs(stream, nct, TOK_TILE) + [
            pl.BlockSpec((1, 1, N_MOD * d), lambda bi, i: (jnp.where(i < nct, nb, bi), 0, 0)),
            const((1, d)),
            pl.BlockSpec((d, in_w), lambda bi, i: (0, 0), pipeline_mode=pl.Buffered(1)),
            pl.BlockSpec((TOK_TILE, LANES), lambda bi, i: (i, 0)),
            pl.BlockSpec((TOK_TILE, LANES), lambda bi, i: (i, 0)),
            const((1, LANES)),
            const((1, LANES)),
            const((LANES, LANES)),
            const(lb_logits.shape),
        ],
        out_specs=out_specs,
        out_shape=out_shape,
        compiler_params=_params(("parallel", "parallel")),
        name="in_projection",
    )(stream[0], stream[1], mod_l, norm_g, w_bf, cos_t, sin_t, qg, kg, gmat, lb_logits)


def _chunk_cumsum(x, reverse):
    n = x.shape[0]
    pos = lax.broadcasted_iota(jnp.int32, x.shape, 0) % HG_CHUNK
    s = 1
    while s < HG_CHUNK:
        if reverse:
            x = x + jnp.where(pos < HG_CHUNK - s, pltpu.roll(x, n - s, axis=0), 0.0)
        else:
            x = x + jnp.where(pos >= s, pltpu.roll(x, s, axis=0), 0.0)
        s *= 2
    return x


def _hgrn_chunk(q, cs, kk, v, st_ref, reverse):
    c = q.shape[0]
    nsb = c // HG_SUB
    anchors = []
    for i in range(nsb):
        r = i * HG_SUB + (HG_SUB // 2 if reverse else HG_SUB // 2 - 1)
        anchors.append(cs[r:r + 1, :])
    c_anchor = jnp.concatenate([jnp.broadcast_to(a, (HG_SUB, a.shape[1])) for a in anchors], axis=0)
    c_end = cs[0:1, :] if reverse else cs[c - 1:c, :]
    qh = q * jnp.exp2(cs - c_anchor)

    def seg_rows(i, rows_of):
        return jnp.concatenate([rows_of(j) if keep_j else jnp.zeros((HG_SUB, q.shape[1]), F32)
                                for j, keep_j in enumerate(i)], axis=0)

    q_ext = jnp.concatenate(
        [seg_rows([j == i for j in range(nsb)], lambda j: qh[j * HG_SUB:(j + 1) * HG_SUB]) for i in range(nsb)],
        axis=1)
    k_ext = jnp.concatenate(
        [seg_rows([(j >= i) if reverse else (j <= i) for j in range(nsb)],
                  lambda j, i=i: kk[j * HG_SUB:(j + 1) * HG_SUB]
                  * jnp.exp2(anchors[i] - cs[j * HG_SUB:(j + 1) * HG_SUB])) for i in range(nsb)],
        axis=1)
    a = _dot_nt(q_ext.astype(BF16), k_ext.astype(BF16))
    row = lax.broadcasted_iota(jnp.int32, (c, c), 0)
    col = lax.broadcasted_iota(jnp.int32, (c, c), 1)
    keep = (col >= row) if reverse else (col <= row)
    st = st_ref[...]
    o = (_dot(jnp.where(keep, a, 0.0).astype(BF16), v)
         + _dot_nt((qh * jnp.exp2(c_anchor)).astype(BF16), st.astype(BF16)))
    kh = (kk * jnp.exp2(c_end - cs)).astype(BF16)
    st_ref[...] = st * jnp.exp2(c_end) + _dot_tn(v, kh)
    return o


def _hgrn_kernel(hq_ref, csf_ref, csb_ref, kkf_ref, kkb_ref, hi_ref, hog_ref, ng_ref, a_ref,
                 of_ref, ob_ref, stf_ref, stb_ref, *, n_chunks, n_ctx_chunks, n_heads):
    stf_ref[...] = jnp.zeros_like(stf_ref)
    stb_ref[...] = jnp.zeros_like(stb_ref)
    c = HG_CHUNK

    def step(s, carry):
        rf = pl.multiple_of(s * c, c)
        cbk = jnp.where(s < n_ctx_chunks, n_ctx_chunks - 1 - s, n_chunks - 1 - (s - n_ctx_chunks))
        rb = pl.multiple_of(cbk * c, c)
        for h in range(n_heads):
            ln = slice(h * HG_DK, (h + 1) * HG_DK)
            of_ref[pl.ds(rf, c), ln] = _hgrn_chunk(
                hq_ref[0, pl.ds(rf, c), ln].astype(F32), csf_ref[0, pl.ds(rf, c), ln],
                kkf_ref[0, pl.ds(rf, c), ln].astype(F32), hi_ref[0, pl.ds(rf, c), ln], stf_ref.at[h], False)
            ob_ref[pl.ds(rb, c), ln] = _hgrn_chunk(
                hq_ref[0, pl.ds(rb, c), ln].astype(F32), csb_ref[0, pl.ds(rb, c), ln],
                kkb_ref[0, pl.ds(rb, c), ln].astype(F32), hi_ref[0, pl.ds(rb, c), ln], stb_ref.at[h], True)
        return carry

    lax.fori_loop(0, n_chunks, step, 0, unroll=HG_UNROLL)

    def readout(i, carry):
        r0 = pl.multiple_of(i * TOK_TILE, TOK_TILE)
        for h in range(n_heads):
            ln = slice(h * HG_DK, (h + 1) * HG_DK)
            o = of_ref[pl.ds(r0, TOK_TILE), ln] + ob_ref[pl.ds(r0, TOK_TILE), ln]
            ms = jnp.mean(o * o, axis=-1, keepdims=True)
            y = o * lax.rsqrt(ms + EPS) * ng_ref[:, ln]
            a_ref[0, pl.ds(r0, TOK_TILE), ln] = (y * hog_ref[0, pl.ds(r0, TOK_TILE), ln].astype(F32)).astype(BF16)
        return carry

    lax.fori_loop(0, (n_chunks * c) // TOK_TILE, readout, 0)


def _hgrn(hq, csf, csb, kkf, kkb, hi, hog, norm_g, ctx_len):
    b, t, hgw = hq.shape
    nh = HG_HEADS_PER_STEP
    wid = nh * HG_DK

    def seq():
        return pl.BlockSpec((1, t, wid), lambda bi, h: (bi, 0, h))

    return pl.pallas_call(
        functools.partial(_hgrn_kernel, n_chunks=t // HG_CHUNK, n_ctx_chunks=ctx_len // HG_CHUNK, n_heads=nh),
        grid=(b, hgw // wid),
        in_specs=[seq(), seq(), seq(), seq(), seq(), seq(), seq(),
                  pl.BlockSpec((1, wid), lambda bi, h: (0, h))],
        out_specs=seq(),
        out_shape=jax.ShapeDtypeStruct((b, t, hgw), BF16),
        scratch_shapes=[pltpu.VMEM((t, wid), F32), pltpu.VMEM((t, wid), F32),
                        pltpu.VMEM((nh, HG_DK, HG_DK), F32), pltpu.VMEM((nh, HG_DK, HG_DK), F32)],
        compiler_params=_params(("parallel", "parallel")),
        name="hgrn2_bidirectional",
    )(hq, csf, csb, kkf, kkb, hi, hog, norm_g)


def _attn_kernel(q_ref, k_ref, v_ref, o_ref, *, n_g, group, hd, nct, ctx_len, t_all):
    i = pl.program_id(2)

    def run(n_keys):
        outs = []
        for g in range(n_g):
            kt = k_ref[0, g, :, 0:n_keys]
            v = v_ref[0, g, 0:n_keys, :]
            for h in range(group):
                c0 = (g * group + h) * hd
                s = _dot(q_ref[0, :, c0:c0 + hd], kt)
                p = jnp.exp2(s - jnp.max(s, axis=1, keepdims=True))
                ov = _dot(p.astype(BF16), v)
                outs.append((ov[:, 0:hd] / ov[:, hd:hd + 1]).astype(BF16))
        o_ref[0] = jnp.concatenate(outs, axis=1)

    @pl.when(i < nct)
    def _():
        run(ctx_len)

    @pl.when(i >= nct)
    def _():
        run(t_all)


def _attention(aq, ak, av, ctx_len):
    b, t, aw = aq.shape
    n_groups, hd = ak.shape[1], ak.shape[2]
    group = aw // (n_groups * hd)
    nt = t // TOK_TILE
    n_g = ATTN_GROUPS_PER_STEP if n_groups % ATTN_GROUPS_PER_STEP == 0 else 1
    return pl.pallas_call(
        functools.partial(_attn_kernel, n_g=n_g, group=group, hd=hd, nct=ctx_len // TOK_TILE, ctx_len=ctx_len,
                          t_all=t),
        grid=(b, n_groups // n_g, nt),
        in_specs=[
            pl.BlockSpec((1, TOK_TILE, n_g * group * hd), lambda bi, g, i: (bi, i, g)),
            pl.BlockSpec((1, n_g, hd, t), lambda bi, g, i: (bi, g, 0, 0)),
            pl.BlockSpec((1, n_g, t, 2 * hd), lambda bi, g, i: (bi, g, 0, 0)),
        ],
        out_specs=pl.BlockSpec((1, TOK_TILE, n_g * group * hd), lambda bi, g, i: (bi, i, g)),
        out_shape=jax.ShapeDtypeStruct((b, t, aw), BF16),
        compiler_params=_params(("parallel", "parallel", "parallel")),
        name="gqa_attention",
    )(aq, ak, av)


def _merge_kernel(xa_ref, xb_ref, mod_ref, a_ref, cb_ref, cu_ref, cup_ref, cun_ref, att_ref, ga_ref, gb_ref, gc_ref,
                  cw_ref, wa_ref, wb_ref, wc_ref, wo_ref, g2_ref, rwt_ref,
                  x1_ref, h2_ref, aff_ref, *, d, nct, nt):
    i = pl.program_id(1)
    ns, rows = xa_ref.shape[0], xa_ref.shape[1]
    n = ns * rows

    def stacked(ref):
        return ref[...].reshape(n, ref.shape[2])

    row = lax.broadcasted_iota(jnp.int32, (n, 1), 0)

    def per_sample(lo):
        out = mod_ref[ns - 1][:, lo:lo + d]
        for s in range(ns - 2, -1, -1):
            out = jnp.where(row < (s + 1) * rows, mod_ref[s][:, lo:lo + d], out)
        return out

    u = stacked(cu_ref).astype(F32)
    has_prev = jnp.logical_and(i != 0, i != nct)
    has_next = jnp.logical_and(i != nct - 1, i != nt - 1)
    u_prev = pltpu.roll(u, 1, axis=0)
    u_next = pltpu.roll(u, n - 1, axis=0)
    for s in range(ns):
        prev_row = jnp.where(has_prev, cup_ref[s, HALO - 1:HALO, :].astype(F32), 0.0)
        next_row = jnp.where(has_next, cun_ref[s, 0:1, :].astype(F32), 0.0)
        u_prev = jnp.where(row == s * rows, prev_row, u_prev)
        u_next = jnp.where(row == (s + 1) * rows - 1, next_row, u_next)
    cw = cw_ref[...]
    conv = cw[0:1, :] * u_prev + cw[1:2, :] * u + cw[2:3, :] * u_next
    bb = (stacked(cb_ref).astype(F32) * conv).astype(BF16)
    y = (stacked(ga_ref).astype(F32) * _dot(stacked(a_ref), wa_ref[...])
         + stacked(gb_ref).astype(F32) * _dot(bb, wb_ref[...])
         + stacked(gc_ref).astype(F32) * _dot(stacked(att_ref), wc_ref[...]))
    x = jnp.where(i < nct, stacked(xa_ref), stacked(xb_ref))
    x1 = x + per_sample(2 * d) * _dot(y.astype(BF16), wo_ref[...])
    x1_ref[...] = x1.reshape(ns, rows, d)
    h2 = _modulated_norm(x1, g2_ref[...], per_sample(3 * d), per_sample(4 * d))
    h2_ref[...] = h2.astype(BF16).reshape(ns, rows, d)
    logits = _dot_nt(rwt_ref[...], h2, precision=HIGHEST)
    ex = jnp.exp(logits - jnp.max(logits, axis=0, keepdims=True))
    aff = ex / jnp.sum(ex, axis=0, keepdims=True)
    for s in range(ns):
        aff_ref[s] = aff[:, s * rows:(s + 1) * rows]


def _merge(stream, mod_l, a, cb, cu, att, ga, gb, gc, conv_w, wa, wb, wc, wo, norm2_g, rwt, nct):
    b, t, d = a.shape[0], a.shape[1], stream[0].shape[2]
    nt = t // TOK_TILE
    ne = rwt.shape[0]
    ns = MERGE_SAMPLES if b % MERGE_SAMPLES == 0 else 1
    mod_m = jnp.concatenate([mod_l[:b]] + [mod_l[b:b + 1]] * ns, axis=0)
    ctx_blk = b // ns
    sub = TOK_TILE // HALO
    n8 = t // HALO

    def tok(width):
        return pl.BlockSpec((ns, TOK_TILE, width), lambda bi, i: (bi, i, 0))

    def const(shape):
        return pl.BlockSpec(shape, lambda bi, i: (0,) * len(shape))

    cwid = cu.shape[2]
    return pl.pallas_call(
        functools.partial(_merge_kernel, d=d, nct=nct, nt=nt),
        grid=(b // ns, nt),
        in_specs=_stream_specs(stream, nct, TOK_TILE, ns) + [
            pl.BlockSpec((ns, 1, N_MOD * d), lambda bi, i: (jnp.where(i < nct, ctx_blk, bi), 0, 0)),
            tok(a.shape[2]), tok(cwid), tok(cwid),
            pl.BlockSpec((ns, HALO, cwid), lambda bi, i: (bi, jnp.maximum(i * sub - 1, 0), 0)),
            pl.BlockSpec((ns, HALO, cwid), lambda bi, i: (bi, jnp.minimum((i + 1) * sub, n8 - 1), 0)),
            tok(att.shape[2]), tok(d), tok(d), tok(d),
            const(conv_w.shape), const(wa.shape), const(wb.shape), const(wc.shape), const(wo.shape),
            const((1, d)), const(rwt.shape),
        ],
        out_specs=[tok(d), tok(d), pl.BlockSpec((ns, ne, TOK_TILE), lambda bi, i: (bi, 0, i))],
        out_shape=[jax.ShapeDtypeStruct((b, t, d), F32), jax.ShapeDtypeStruct((b, t, d), BF16),
                   jax.ShapeDtypeStruct((b, ne, t), F32)],
        compiler_params=_params(("parallel", "parallel")),
        name="merge_residual_router",
    )(stream[0], stream[1], mod_m, a, cb, cu, cu, cu, att, ga, gb, gc, conv_w, wa, wb, wc, wo, norm2_g, rwt)


def _prefix_count(mask, tri):
    e, n = mask.shape
    carry = jnp.zeros((e, 1), F32)
    outs = []
    for blk in range(n // LANES):
        xb = jnp.where(mask[:, blk * LANES:(blk + 1) * LANES], 1.0, 0.0).astype(BF16)
        pre = _dot(xb, tri) + carry
        carry = pre[:, LANES - 1:LANES]
        outs.append(pre)
    return jnp.concatenate(outs, axis=1) if len(outs) > 1 else outs[0]


def _topk_kernel(aff_ref, tri_ref, tind_ref, sut_ref, pos_ref, before_ref, *, off, n, cap):
    a = aff_ref[0][:, off:off + n]
    bits = pltpu.bitcast(a, jnp.int32)
    thr = jnp.zeros((a.shape[0], 1), jnp.int32)
    for bit in range(30, -1, -1):
        cand = thr | (1 << bit)
        cnt = jnp.sum(jnp.where(bits >= cand, 1.0, 0.0), axis=1, keepdims=True)
        thr = jnp.where(cnt >= cap, cand, thr)
    gt = bits > thr
    eq = bits == thr
    need = cap - jnp.sum(jnp.where(gt, 1.0, 0.0), axis=1, keepdims=True)
    tri = tri_ref[...]
    eq_rank = _prefix_count(eq, tri)
    sel = jnp.logical_or(gt, jnp.logical_and(eq, eq_rank <= need))
    slot = _prefix_count(sel, tri) - 1.0
    pos_ref[0] = jnp.where(sel, slot, -1.0).astype(jnp.int32)
    tile_cnt = _dot(jnp.where(sel, 1.0, 0.0).astype(BF16), tind_ref[...])
    before_ref[0] = _dot(tile_cnt.astype(BF16), sut_ref[...]).astype(jnp.int32)


def _topk_positions(aff_t, tri, off, n, cap):
    b, ne, t = aff_t.shape
    assert n // TOK_TILE < LANES and TOK_TILE <= 256
    lane = jnp.arange(LANES)
    tind = (jnp.arange(n)[:, None] // TOK_TILE == lane[None, :]).astype(BF16)
    sut = (lane[:, None] < lane[None, :]).astype(BF16)

    def const(shape):
        return pl.BlockSpec(shape, lambda bi: (0,) * len(shape))

    return pl.pallas_call(
        functools.partial(_topk_kernel, off=off, n=n, cap=cap),
        grid=(b,),
        in_specs=[pl.BlockSpec((1, ne, t), lambda bi: (bi, 0, 0)), const((LANES, LANES)), const(tind.shape),
                  const(sut.shape)],
        out_specs=[pl.BlockSpec((1, ne, n), lambda bi: (bi, 0, 0)), pl.BlockSpec((1, ne, LANES), lambda bi: (bi, 0, 0))],
        out_shape=[jax.ShapeDtypeStruct((b, ne, n), jnp.int32), jax.ShapeDtypeStruct((b, ne, LANES), jnp.int32)],
        compiler_params=_params(("parallel",)),
        name="expert_choice_topk",
    )(aff_t, tri, tind, sut)


def _gather_kernel(before_ref, h_ref, pos_ref, xs_ref, *, ne, cap, win):
    bi = pl.program_id(0)
    tt = pl.program_id(1)

    @pl.when(tt == 0)
    def _():
        xs_ref[...] = jnp.zeros_like(xs_ref)

    h = h_ref[0]
    pos = pos_ref[0]
    rows = h.shape[0]
    sub = lax.broadcasted_iota(jnp.int32, (win, rows), 0)
    hots, spans = [], []
    overflow = False
    for e in range(ne):
        base = (bi * ne + e) * LANES
        c0 = before_ref[base + tt]
        c1 = before_ref[base + tt + 1]
        w0 = pl.multiple_of(jnp.minimum((c0 // SLOT_ALIGN) * SLOT_ALIGN, cap - win), SLOT_ALIGN)
        hots.append(jnp.where(pos[e:e + 1, :] - w0 == sub, 1.0, 0.0).astype(BF16))
        spans.append((c1, w0))
        overflow = jnp.logical_or(overflow, c1 > w0 + win)
    picked = _dot(jnp.concatenate(hots, axis=0), h).astype(BF16)
    for e, (c1, w0) in enumerate(spans):
        xs_ref[e, pl.ds(w0, win), :] += picked[e * win:(e + 1) * win, :]

    @pl.when(overflow)
    def _():
        for e, (c1, w0) in enumerate(spans):
            def more(k, carry, e=e, w0=w0):
                ws = w0 + k * win
                wc = pl.multiple_of(jnp.minimum(ws, cap - win), SLOT_ALIGN)
                hit = jnp.logical_and(pos_ref[0][e:e + 1, :] - wc == sub, sub >= ws - wc)
                xs_ref[e, pl.ds(wc, win), :] += _dot(jnp.where(hit, 1.0, 0.0).astype(BF16), h_ref[0]).astype(BF16)
                return carry

            lax.fori_loop(1, (c1 - w0 + win - 1) // win, more, 0)


def _gather_tokens(h2, pos, before, off, n, cap):
    b, t, d = h2.shape
    ne = pos.shape[1]
    ot = off // TOK_TILE
    win = min(cap, SLOT_WINDOW)
    grid_spec = pltpu.PrefetchScalarGridSpec(
        num_scalar_prefetch=1,
        grid=(b, n // TOK_TILE),
        in_specs=[pl.BlockSpec((1, TOK_TILE, d), lambda bi, tt, bf: (bi, tt + ot, 0)),
                  pl.BlockSpec((1, ne, TOK_TILE), lambda bi, tt, bf: (bi, 0, tt))],
        out_specs=pl.BlockSpec((ne, cap, d), lambda bi, tt, bf: (0, bi, 0)),
    )
    return pl.pallas_call(
        functools.partial(_gather_kernel, ne=ne, cap=cap, win=win),
        grid_spec=grid_spec,
        out_shape=jax.ShapeDtypeStruct((ne, b * cap, d), BF16),
        compiler_params=_params(("parallel", "arbitrary")),
        name="expert_gather",
    )(before.reshape(-1), h2, pos)


def _ffn_kernel(xs_ref, wg_ref, wu_ref, wd_ref, ys_ref, wgb_ref, wub_ref, wdb_ref):
    @pl.when(pl.program_id(1) == 0)
    def _():
        wgb_ref[...] = wg_ref[0, 0].astype(BF16)
        wub_ref[...] = wu_ref[0, 0].astype(BF16)
        wdb_ref[...] = wd_ref[0, 0].astype(BF16)

    x = xs_ref[0]
    g = _dot(x, wgb_ref[...])
    hid = (g * _sigmoid(g)) * _dot(x, wub_ref[...])
    ys_ref[0] = _dot(hid.astype(BF16), wdb_ref[...]).astype(BF16)


def _expert_ffn(xs, wg, wu, wd, layer):
    ne, m, d = xs.shape
    ff = wg.shape[3]
    tm = min(m, 512)
    return pl.pallas_call(
        _ffn_kernel,
        grid=(ne, m // tm),
        in_specs=[pl.BlockSpec((1, tm, d), lambda e, j: (e, j, 0)),
                  pl.BlockSpec((1, 1, d, ff), lambda e, j: (layer, e, 0, 0)),
                  pl.BlockSpec((1, 1, d, ff), lambda e, j: (layer, e, 0, 0)),
                  pl.BlockSpec((1, 1, ff, d), lambda e, j: (layer, e, 0, 0))],
        out_specs=pl.BlockSpec((1, tm, d), lambda e, j: (e, j, 0)),
        out_shape=jax.ShapeDtypeStruct((ne, m, d), BF16),
        scratch_shapes=[pltpu.VMEM((d, ff), BF16), pltpu.VMEM((d, ff), BF16), pltpu.VMEM((ff, d), BF16)],
        compiler_params=_params(("parallel", "arbitrary")),
        name="expert_ffn",
    )(xs, wg, wu, wd)


def _scatter_kernel(before_ref, x_ref, mod_ref, ys_ref, pos_ref, aff_ref, fg_ref, o_ref, acc_ref, *,
                    d, ne, cap, win, final):
    bi = pl.program_id(0)
    i = pl.program_id(1)
    m = mod_ref[0]
    pos = pos_ref[0]
    aff = aff_ref[0]
    rows = pos.shape[0]
    lane = lax.broadcasted_iota(jnp.int32, (rows, win), 1)
    acc = jnp.zeros((rows, d), F32)
    spans = []
    overflow = False
    for e in range(ne):
        base = (bi * ne + e) * LANES
        c0 = before_ref[base + i]
        c1 = before_ref[base + i + 1]
        w0 = pl.multiple_of(jnp.minimum((c0 // SLOT_ALIGN) * SLOT_ALIGN, cap - win), SLOT_ALIGN)
        onehot = jnp.where(pos[:, e:e + 1] - w0 == lane, 1.0, 0.0).astype(BF16)
        acc = acc + aff[:, e:e + 1] * _dot(onehot, ys_ref[e, pl.ds(w0, win), :])
        spans.append((c1, w0))
        overflow = jnp.logical_or(overflow, c1 > w0 + win)
    acc_ref[...] = acc

    @pl.when(overflow)
    def _():
        for e, (c1, w0) in enumerate(spans):
            def more(k, carry, e=e, w0=w0):
                ws = w0 + k * win
                wc = pl.multiple_of(jnp.minimum(ws, cap - win), SLOT_ALIGN)
                hit = jnp.logical_and(pos_ref[0][:, e:e + 1] - wc == lane, lane >= ws - wc)
                acc_ref[...] += aff_ref[0][:, e:e + 1] * _dot(jnp.where(hit, 1.0, 0.0).astype(BF16),
                                                              ys_ref[e, pl.ds(wc, win), :])
                return carry

            lax.fori_loop(1, (c1 - w0 + win - 1) // win, more, 0)

    x2 = x_ref[0] + m[:, 5 * d:6 * d] * acc_ref[...]
    if final:
        ms = jnp.mean(x2 * x2, axis=-1, keepdims=True)
        x2 = x2 * lax.rsqrt(ms + EPS) * fg_ref[...]
    o_ref[0] = x2


def _scatter_residual(x1, mod_l, ys, pos_n, aff_n, before, final_g, off, n, cap, mod_row_ctx, final):
    b, t, d = x1.shape
    ne = ys.shape[0]
    nb = mod_l.shape[0] - 1
    ot = off // TOK_TILE
    win = min(cap, SLOT_WINDOW)
    if final:
        out_shape = jax.ShapeDtypeStruct((b, n, d), F32)
        out_spec = pl.BlockSpec((1, TOK_TILE, d), lambda bi, i, bf: (bi, i, 0))
        aliases = {}
    else:
        out_shape = jax.ShapeDtypeStruct((b, t, d), F32)
        out_spec = pl.BlockSpec((1, TOK_TILE, d), lambda bi, i, bf: (bi, i + ot, 0))
        aliases = {1: 0}
    grid_spec = pltpu.PrefetchScalarGridSpec(
        num_scalar_prefetch=1,
        grid=(b, n // TOK_TILE),
        in_specs=[
            pl.BlockSpec((1, TOK_TILE, d), lambda bi, i, bf: (bi, i + ot, 0)),
            pl.BlockSpec((1, 1, N_MOD * d), lambda bi, i, bf: (nb if mod_row_ctx else bi, 0, 0)),
            pl.BlockSpec((ne, cap, d), lambda bi, i, bf: (0, bi, 0)),
            pl.BlockSpec((1, TOK_TILE, ne), lambda bi, i, bf: (bi, i, 0)),
            pl.BlockSpec((1, TOK_TILE, ne), lambda bi, i, bf: (bi, i + ot, 0)),
            pl.BlockSpec((1, d), lambda bi, i, bf: (0, 0)),
        ],
        out_specs=out_spec,
        scratch_shapes=[pltpu.VMEM((TOK_TILE, d), F32)],
    )
    return pl.pallas_call(
        functools.partial(_scatter_kernel, d=d, ne=ne, cap=cap, win=win, final=final),
        grid_spec=grid_spec,
        out_shape=out_shape,
        input_output_aliases=aliases,
        compiler_params=_params(("parallel", "parallel")),
        name="expert_scatter_residual",
    )(before.reshape(-1), x1, mod_l, ys, pos_n, aff_n, final_g)


def _rope_tables(ctx_len, seq, hd):
    rows = seq // GRID_W
    row = jnp.repeat(jnp.arange(rows), GRID_W).astype(F32)
    col = jnp.tile(jnp.arange(GRID_W), rows).astype(F32)
    inv = ROPE_THETA ** (-jnp.arange(0, hd // 2, 2, dtype=F32) / (hd // 2))
    ang = jnp.concatenate([row[:, None] * inv, col[:, None] * inv], axis=-1)
    cos = jnp.repeat(jnp.cos(ang), 2, axis=-1)
    sin = jnp.repeat(jnp.sin(ang), 2, axis=-1) * jnp.tile(jnp.array([-1.0, 1.0], F32), hd // 2)
    cos = jnp.concatenate([jnp.ones((ctx_len, hd), F32), cos], axis=0)
    sin = jnp.concatenate([jnp.zeros((ctx_len, hd), F32), sin], axis=0)
    rep = LANES // hd
    return jnp.tile(cos, (1, rep)), jnp.tile(sin, (1, rep))


def kernel(x, c, ctx, c_ctx, ada_w, ada_b, norm1_g, norm2_g, w_in, hg_lb_logits, hg_norm_g, conv_w, q_norm_g, k_norm_g, w_proj_a, w_proj_b, w_proj_c, w_out, router_w, w_gate, w_up, w_down, final_norm_g):
    b, s, d = x.shape
    ctx_len = ctx.shape[1]
    t = ctx_len + s
    depth = w_in.shape[0]
    hgw = hg_norm_g.shape[1]
    cw = conv_w.shape[2]
    aw = w_proj_c.shape[1]
    hd = q_norm_g.shape[1]
    kvw = (w_in.shape[2] - 5 * hgw - 3 * cw - aw - 3 * d) // 2
    ne = router_w.shape[2]
    nct = ctx_len // TOK_TILE
    assert ctx_len % TOK_TILE == 0 and s % TOK_TILE == 0 and LANES % hd == 0 and cw == d
    dims = dict(hgw=hgw, cw=cw, aw=aw, kvw=kvw, hd=hd, nct=nct, t=t)

    stream = (ctx, x, nct)
    n_rows = -(-(b + 1) // 8) * 8
    cvec = jnp.concatenate([c, c_ctx[None, :], jnp.zeros((n_rows - b - 1, d), F32)], axis=0)
    mod = _modulation(cvec, ada_w, ada_b)
    mod = mod[:, :b + 1].reshape(depth, b + 1, 1, N_MOD * d)

    cos_t, sin_t = _rope_tables(ctx_len, s, hd)
    lane = jnp.arange(LANES)
    gmat = (lane[:, None] // hd == lane[None, :] // hd).astype(BF16)
    tri = (lane[:, None] <= lane[None, :]).astype(BF16)
    rep = LANES // hd
    lb_logits = hg_lb_logits.astype(F32)

    out = None
    for l in range(depth):
        last = l == depth - 1
        mod_l = mod[l]
        (hq, lff, lfb, kkf, kkb, hi, hog, cb, cu, aq, ak, av, ga, gb, gc) = _in_projection(
            stream, mod_l, norm1_g[l][None, :], w_in[l].astype(BF16), cos_t, sin_t,
            jnp.tile(q_norm_g[l], rep)[None, :], jnp.tile(k_norm_g[l], rep)[None, :], gmat, lb_logits, l, dims)
        a = _hgrn(hq, lff, lfb, kkf, kkb, hi, hog, hg_norm_g[l][None, :], ctx_len)
        att = _attention(aq, ak, av, ctx_len)
        x1, h2, aff_t = _merge(stream, mod_l, a, cb, cu, att, ga, gb, gc, conv_w[l],
                               w_proj_a[l].astype(BF16), w_proj_b[l].astype(BF16), w_proj_c[l].astype(BF16),
                               w_out[l].astype(BF16), norm2_g[l][None, :], router_w[l].T, nct)
        aff_n = jnp.swapaxes(aff_t, 1, 2)

        def moe(xs_in, off, n, mod_row_ctx, final):
            cap = CAPACITY_FACTOR * n // ne
            pos, before = _topk_positions(aff_t, tri, off, n, cap)
            xs = _gather_tokens(h2, pos, before, off, n, cap)
            ys = _expert_ffn(xs, w_gate, w_up, w_down, l)
            return _scatter_residual(xs_in, mod_l, ys, jnp.swapaxes(pos, 1, 2), aff_n, before,
                                     final_norm_g[None, :], off, n, cap, mod_row_ctx, final)

        if last:
            out = moe(x1, ctx_len, s, False, True)
        else:
            xc = moe(x1, ctx_len, s, False, False)
            xc = moe(xc, 0, ctx_len, True, False)
            stream = (xc, xc, 0)
    return out
```

```python
import functools

import jax
import jax.numpy as jnp
from jax import lax
from jax.experimental import pallas as pl
from jax.experimental.pallas import tpu as pltpu

F32 = jnp.float32
BF16 = jnp.bfloat16

EPS = 1e-6
N_MOD = 6
HG_DK = 128
GRID_W = 64
ROPE_THETA = 10000.0
CAPACITY_FACTOR = 2

LANES = 128
TOK_TILE = 256
HG_CHUNK = 64
HG_SUB = 32
HG_HEADS_PER_STEP = 2
HG_UNROLL = 4
KV_TILE = 256
HALO = 16
ATTN_GROUPS_PER_STEP = 4
SLOT_WINDOW = 128
SLOT_ALIGN = 16
MERGE_SAMPLES = 2
LOG2E = 1.4426950408889634
VMEM_LIMIT = 56 * 1024 * 1024

HIGHEST = lax.Precision.HIGHEST


def _dot(a, b, precision=None):
    return jnp.dot(a, b, preferred_element_type=F32, precision=precision)


def _dot_nt(a, b, precision=None):
    return lax.dot_general(a, b, (((1,), (1,)), ((), ())), preferred_element_type=F32, precision=precision)


def _dot_tn(a, b):
    return lax.dot_general(a, b, (((0,), (0,)), ((), ())), preferred_element_type=F32)


def _sigmoid(x):
    return 1.0 / (1.0 + jnp.exp(-x))


def _params(sem, vmem=VMEM_LIMIT):
    return pltpu.CompilerParams(dimension_semantics=sem, vmem_limit_bytes=vmem)


def _mod_kernel(c_ref, w_ref, b_ref, o_ref):
    c = c_ref[...]
    sc = c * _sigmoid(c)
    o_ref[0] = _dot(sc, w_ref[0], precision=HIGHEST) + b_ref[0]


def _modulation(cvec, ada_w, ada_b):
    depth, d, n = ada_w.shape
    r = cvec.shape[0]
    tn = d
    return pl.pallas_call(
        _mod_kernel,
        grid=(depth, n // tn),
        in_specs=[
            pl.BlockSpec((r, d), lambda l, j: (0, 0)),
            pl.BlockSpec((1, d, tn), lambda l, j: (l, 0, j)),
            pl.BlockSpec((1, 1, tn), lambda l, j: (l, 0, j)),
        ],
        out_specs=pl.BlockSpec((1, r, tn), lambda l, j: (l, 0, j)),
        out_shape=jax.ShapeDtypeStruct((depth, r, n), F32),
        compiler_params=_params(("parallel", "parallel")),
        name="adaln_modulation",
    )(cvec, ada_w, ada_b.reshape(depth, 1, n))


def _modulated_norm(x, g, shift, scale):
    ms = jnp.mean(x * x, axis=-1, keepdims=True)
    return (x * lax.rsqrt(ms + EPS) * g) * (1.0 + scale) + shift


def _headnorm_rope(p, gain, gmat, cos, sin, hd, post_scale):
    rows, width = p.shape
    lane = lax.broadcasted_iota(jnp.int32, (rows, LANES), 1)
    even = (lane % 2) == 0
    outs = []
    for cb in range(width // LANES):
        xb = p[:, cb * LANES:(cb + 1) * LANES]
        ss = _dot((xb * xb).astype(BF16), gmat)
        y = xb * lax.rsqrt(ss * (1.0 / hd) + EPS) * gain
        y_next = pltpu.roll(y, LANES - 1, axis=1)
        y_prev = pltpu.roll(y, 1, axis=1)
        ysw = jnp.where(even, y_next, y_prev)
        outs.append((y * cos + ysw * sin) * post_scale)
    return jnp.concatenate(outs, axis=1) if len(outs) > 1 else outs[0]


def _lower_bounds(lg, layer):
    e = jnp.exp(lg - jnp.max(lg, axis=0, keepdims=True))
    sm = e / jnp.sum(e, axis=0, keepdims=True)
    lb = jnp.zeros(lg.shape[1:], F32)
    for j in range(1, layer + 1):
        lb = lb + sm[j]
    return lb


def _forget_gate(z, lbd, lb_is_zero):
    zs = z * LOG2E
    sp = jnp.log2(1.0 + jnp.exp2(-jnp.abs(zs)))
    ls = jnp.minimum(zs, 0.0) - sp
    sneg = jnp.exp2(jnp.minimum(-zs, 0.0) - sp)
    if lb_is_zero:
        return ls, sneg
    a = jnp.log2(lbd)
    t = jnp.log2(1.0 - lbd) + ls
    mx = jnp.maximum(a, t)
    mn = jnp.minimum(a, t)
    return mx + jnp.log2(1.0 + jnp.exp2(mn - mx)), (1.0 - lbd) * sneg


def _stream_specs(stream, nct, block_rows, ns=1):
    _, _, lat_off = stream
    d = stream[0].shape[2]
    return [pl.BlockSpec((ns, block_rows, d), lambda bi, i: (bi, jnp.minimum(i, nct - 1), 0)),
            pl.BlockSpec((ns, block_rows, d), lambda bi, i: (bi, jnp.maximum(i, nct) - lat_off, 0))]


def _inproj_kernel(xa_ref, xb_ref, mod_ref, g_ref, w_ref, cos_ref, sin_ref, qg_ref, kg_ref, gmat_ref,
                   hq_ref, zf_ref, zb_ref, hi_ref, hog_ref, cb_ref, cu_ref, aq_ref, ak_ref,
                   av_ref, ga_ref, gb_ref, gc_ref, *, d, hgw, cw, aw, kvw, hd, nct):
    x = jnp.where(pl.program_id(1) < nct, xa_ref[0], xb_ref[0])
    m = mod_ref[0]
    h = _modulated_norm(x, g_ref[...], m[:, 0:d], m[:, d:2 * d]).astype(BF16)

    def proj(lo, width):
        return _dot(h, w_ref[:, lo:lo + width])

    o_hq, o_zf, o_zb, o_hi, o_hog = (j * hgw for j in range(5))
    o_cb, o_cc, o_cx = (5 * hgw + j * cw for j in range(3))
    o_q = 5 * hgw + 3 * cw
    o_k, o_v = o_q + aw, o_q + aw + kvw
    o_ga, o_gb, o_gc = (o_v + kvw + j * d for j in range(3))
    cos = cos_ref[...]
    sin = sin_ref[...]
    gmat = gmat_ref[...]

    def gate(o, ref):
        ref[0] = _sigmoid(proj(o, d)).astype(BF16)

    hq_ref[0] = proj(o_hq, hgw).astype(BF16)
    zf_ref[0] = proj(o_zf, hgw)
    zb_ref[0] = proj(o_zb, hgw)
    hi_ref[0] = proj(o_hi, hgw).astype(BF16)
    g = proj(o_hog, hgw)
    hog_ref[0] = (g * _sigmoid(g)).astype(BF16)
    cb_ref[0] = proj(o_cb, cw).astype(BF16)
    cu_ref[0] = (proj(o_cc, cw) * proj(o_cx, cw)).astype(BF16)
    aq_ref[0] = _headnorm_rope(proj(o_q, aw), qg_ref[...], gmat, cos, sin, hd, hd ** -0.5 * LOG2E).astype(BF16)
    kn_t = _headnorm_rope(proj(o_k, kvw), kg_ref[...], gmat, cos, sin, hd, 1.0).T.astype(BF16)
    vv = proj(o_v, kvw).astype(BF16)
    ones_col = jnp.where(lax.broadcasted_iota(jnp.int32, (vv.shape[0], hd), 1) == 0, 1.0, 0.0).astype(BF16)
    for g in range(kvw // hd):
        ak_ref[0, g] = kn_t[g * hd:(g + 1) * hd, :]
        av_ref[0, g] = jnp.concatenate([vv[:, g * hd:(g + 1) * hd], ones_col], axis=1)
    gate(o_ga, ga_ref)
    gate(o_gb, gb_ref)
    gate(o_gc, gc_ref)


def _in_projection(stream, mod_l, norm_g, w_bf, cos_t, sin_t, qg, kg, gmat, dims):
    b, _, d = stream[0].shape
    t = dims["t"]
    hgw, cw, aw, kvw, hd, nct = dims["hgw"], dims["cw"], dims["aw"], dims["kvw"], dims["hd"], dims["nct"]
    nt = t // TOK_TILE
    in_w = w_bf.shape[1]
    nb = mod_l.shape[0] - 1

    def tok(width):
        return pl.BlockSpec((1, TOK_TILE, width), lambda bi, i: (bi, i, 0))

    def const(shape):
        return pl.BlockSpec(shape, lambda bi, i: (0,) * len(shape))

    widths = [(hgw, BF16), (hgw, F32), (hgw, F32), (hgw, BF16), (hgw, BF16),
              (cw, BF16), (cw, BF16), (aw, BF16), (kvw, BF16), (kvw, BF16), (d, BF16), (d, BF16), (d, BF16)]
    ng = kvw // hd
    out_specs = [tok(w) for w, _ in widths]
    out_shape = [jax.ShapeDtypeStruct((b, t, w), dt) for w, dt in widths]
    out_specs[8] = pl.BlockSpec((1, ng, hd, TOK_TILE), lambda bi, i: (bi, 0, 0, i))
    out_shape[8] = jax.ShapeDtypeStruct((b, ng, hd, t), BF16)
    out_specs[9] = pl.BlockSpec((1, ng, TOK_TILE, 2 * hd), lambda bi, i: (bi, 0, i, 0))
    out_shape[9] = jax.ShapeDtypeStruct((b, ng, t, 2 * hd), BF16)
    return pl.pallas_call(
        functools.partial(_inproj_kernel, d=d, hgw=hgw, cw=cw, aw=aw, kvw=kvw, hd=hd, nct=nct),
        grid=(b, nt),
        in_specs=_stream_specs(stream, nct, TOK_TILE) + [
            pl.BlockSpec((1, 1, N_MOD * d), lambda bi, i: (jnp.where(i < nct, nb, bi), 0, 0)),
            const((1, d)),
            pl.BlockSpec((d, in_w), lambda bi, i: (0, 0), pipeline_mode=pl.Buffered(1)),
            pl.BlockSpec((TOK_TILE, LANES), lambda bi, i: (i, 0)),
            pl.BlockSpec((TOK_TILE, LANES), lambda bi, i: (i, 0)),
            const((1, LANES)),
            const((1, LANES)),
            const((LANES, LANES)),
        ],
        out_specs=out_specs,
        out_shape=out_shape,
        compiler_params=_params(("parallel", "parallel")),
        name="in_projection",
    )(stream[0], stream[1], mod_l, norm_g, w_bf, cos_t, sin_t, qg, kg, gmat)


def _chunk_cumsum(x, reverse):
    n = x.shape[0]
    pos = lax.broadcasted_iota(jnp.int32, x.shape, 0) % HG_CHUNK
    s = 1
    while s < HG_CHUNK:
        if reverse:
            x = x + jnp.where(pos < HG_CHUNK - s, pltpu.roll(x, n - s, axis=0), 0.0)
        else:
            x = x + jnp.where(pos >= s, pltpu.roll(x, s, axis=0), 0.0)
        s *= 2
    return x


def _hgrn_chunk(q, z, lbd, lb_is_zero, v, st_ref, reverse):
    c = q.shape[0]
    lf, kk = _forget_gate(z, lbd, lb_is_zero)
    cs = _chunk_cumsum(lf, reverse)
    nsb = c // HG_SUB
    anchors = []
    for i in range(nsb):
        r = i * HG_SUB + (HG_SUB // 2 if reverse else HG_SUB // 2 - 1)
        anchors.append(cs[r:r + 1, :])
    c_anchor = jnp.concatenate([jnp.broadcast_to(a, (HG_SUB, a.shape[1])) for a in anchors], axis=0)
    c_end = cs[0:1, :] if reverse else cs[c - 1:c, :]
    qh = q * jnp.exp2(cs - c_anchor)

    def seg_rows(i, rows_of):
        return jnp.concatenate([rows_of(j) if keep_j else jnp.zeros((HG_SUB, q.shape[1]), F32)
                                for j, keep_j in enumerate(i)], axis=0)

    q_ext = jnp.concatenate(
        [seg_rows([j == i for j in range(nsb)], lambda j: qh[j * HG_SUB:(j + 1) * HG_SUB]) for i in range(nsb)],
        axis=1)
    k_ext = jnp.concatenate(
        [seg_rows([(j >= i) if reverse else (j <= i) for j in range(nsb)],
                  lambda j, i=i: kk[j * HG_SUB:(j + 1) * HG_SUB]
                  * jnp.exp2(anchors[i] - cs[j * HG_SUB:(j + 1) * HG_SUB])) for i in range(nsb)],
        axis=1)
    a = _dot_nt(q_ext.astype(BF16), k_ext.astype(BF16))
    row = lax.broadcasted_iota(jnp.int32, (c, c), 0)
    col = lax.broadcasted_iota(jnp.int32, (c, c), 1)
    keep = (col >= row) if reverse else (col <= row)
    st = st_ref[...]
    o = (_dot(jnp.where(keep, a, 0.0).astype(BF16), v)
         + _dot_nt((qh * jnp.exp2(c_anchor)).astype(BF16), st.astype(BF16)))
    kh = (kk * jnp.exp2(c_end - cs)).astype(BF16)
    st_ref[...] = st * jnp.exp2(c_end) + _dot_tn(v, kh)
    return o


def _hgrn_kernel(lbl_ref, hq_ref, zf_ref, zb_ref, hi_ref, hog_ref, ng_ref, a_ref,
                 of_ref, ob_ref, stf_ref, stb_ref, *, layer, n_chunks, n_ctx_chunks, n_heads):
    stf_ref[...] = jnp.zeros_like(stf_ref)
    stb_ref[...] = jnp.zeros_like(stb_ref)
    c = HG_CHUNK
    lb = _lower_bounds(lbl_ref[...], layer)

    def step(s, carry):
        rf = pl.multiple_of(s * c, c)
        cbk = jnp.where(s < n_ctx_chunks, n_ctx_chunks - 1 - s, n_chunks - 1 - (s - n_ctx_chunks))
        rb = pl.multiple_of(cbk * c, c)
        for h in range(n_heads):
            ln = slice(h * HG_DK, (h + 1) * HG_DK)
            of_ref[pl.ds(rf, c), ln] = _hgrn_chunk(
                hq_ref[0, pl.ds(rf, c), ln].astype(F32), zf_ref[0, pl.ds(rf, c), ln], lb[0:1, ln], layer == 0,
                hi_ref[0, pl.ds(rf, c), ln], stf_ref.at[h], False)
            ob_ref[pl.ds(rb, c), ln] = _hgrn_chunk(
                hq_ref[0, pl.ds(rb, c), ln].astype(F32), zb_ref[0, pl.ds(rb, c), ln], lb[1:2, ln], layer == 0,
                hi_ref[0, pl.ds(rb, c), ln], stb_ref.at[h], True)
        return carry

    lax.fori_loop(0, n_chunks, step, 0, unroll=HG_UNROLL)

    def readout(i, carry):
        r0 = pl.multiple_of(i * TOK_TILE, TOK_TILE)
        for h in range(n_heads):
            ln = slice(h * HG_DK, (h + 1) * HG_DK)
            o = of_ref[pl.ds(r0, TOK_TILE), ln] + ob_ref[pl.ds(r0, TOK_TILE), ln]
            ms = jnp.mean(o * o, axis=-1, keepdims=True)
            y = o * lax.rsqrt(ms + EPS) * ng_ref[:, ln]
            a_ref[0, pl.ds(r0, TOK_TILE), ln] = (y * hog_ref[0, pl.ds(r0, TOK_TILE), ln].astype(F32)).astype(BF16)
        return carry

    lax.fori_loop(0, (n_chunks * c) // TOK_TILE, readout, 0)


def _hgrn(hq, zf, zb, hi, hog, lb_logits, norm_g, layer, ctx_len):
    b, t, hgw = hq.shape
    nh = HG_HEADS_PER_STEP
    wid = nh * HG_DK
    depth = lb_logits.shape[0]

    def seq():
        return pl.BlockSpec((1, t, wid), lambda bi, h: (bi, 0, h))

    return pl.pallas_call(
        functools.partial(_hgrn_kernel, layer=layer, n_chunks=t // HG_CHUNK, n_ctx_chunks=ctx_len // HG_CHUNK,
                          n_heads=nh),
        grid=(b, hgw // wid),
        in_specs=[pl.BlockSpec((depth, 2, wid), lambda bi, h: (0, 0, h)),
                  seq(), seq(), seq(), seq(), seq(),
                  pl.BlockSpec((1, wid), lambda bi, h: (0, h))],
        out_specs=seq(),
        out_shape=jax.ShapeDtypeStruct((b, t, hgw), BF16),
        scratch_shapes=[pltpu.VMEM((t, wid), F32), pltpu.VMEM((t, wid), F32),
                        pltpu.VMEM((nh, HG_DK, HG_DK), F32), pltpu.VMEM((nh, HG_DK, HG_DK), F32)],
        compiler_params=_params(("parallel", "parallel")),
        name="hgrn2_bidirectional",
    )(lb_logits, hq, zf, zb, hi, hog, norm_g)


def _attn_kernel(q_ref, k_ref, v_ref, o_ref, *, n_g, group, hd, nct, ctx_len, t_all):
    i = pl.program_id(2)

    def run(n_keys):
        outs = []
        for g in range(n_g):
            kt = k_ref[0, g, :, 0:n_keys]
            v = v_ref[0, g, 0:n_keys, :]
            for h in range(group):
                c0 = (g * group + h) * hd
                s = _dot(q_ref[0, :, c0:c0 + hd], kt)
                p = jnp.exp2(s - jnp.max(s, axis=1, keepdims=True))
                ov = _dot(p.astype(BF16), v)
                outs.append((ov[:, 0:hd] / ov[:, hd:hd + 1]).astype(BF16))
        o_ref[0] = jnp.concatenate(outs, axis=1)

    @pl.when(i < nct)
    def _():
        run(ctx_len)

    @pl.when(i >= nct)
    def _():
        run(t_all)


def _attention(aq, ak, av, ctx_len):
    b, t, aw = aq.shape
    n_groups, hd = ak.shape[1], ak.shape[2]
    group = aw // (n_groups * hd)
    nt = t // TOK_TILE
    n_g = ATTN_GROUPS_PER_STEP if n_groups % ATTN_GROUPS_PER_STEP == 0 else 1
    return pl.pallas_call(
        functools.partial(_attn_kernel, n_g=n_g, group=group, hd=hd, nct=ctx_len // TOK_TILE, ctx_len=ctx_len,
                          t_all=t),
        grid=(b, n_groups // n_g, nt),
        in_specs=[
            pl.BlockSpec((1, TOK_TILE, n_g * group * hd), lambda bi, g, i: (bi, i, g)),
            pl.BlockSpec((1, n_g, hd, t), lambda bi, g, i: (bi, g, 0, 0)),
            pl.BlockSpec((1, n_g, t, 2 * hd), lambda bi, g, i: (bi, g, 0, 0)),
        ],
        out_specs=pl.BlockSpec((1, TOK_TILE, n_g * group * hd), lambda bi, g, i: (bi, i, g)),
        out_shape=jax.ShapeDtypeStruct((b, t, aw), BF16),
        compiler_params=_params(("parallel", "parallel", "parallel")),
        name="gqa_attention",
    )(aq, ak, av)


def _merge_kernel(xa_ref, xb_ref, mod_ref, a_ref, cb_ref, cu_ref, cup_ref, cun_ref, att_ref, ga_ref, gb_ref, gc_ref,
                  cw_ref, wa_ref, wb_ref, wc_ref, wo_ref, g2_ref, rwt_ref,
                  x1_ref, h2_ref, aff_ref, *, d, nct, nt):
    i = pl.program_id(1)
    ns, rows = xa_ref.shape[0], xa_ref.shape[1]
    n = ns * rows

    def stacked(ref):
        return ref[...].reshape(n, ref.shape[2])

    row = lax.broadcasted_iota(jnp.int32, (n, 1), 0)

    def per_sample(lo):
        out = mod_ref[ns - 1][:, lo:lo + d]
        for s in range(ns - 2, -1, -1):
            out = jnp.where(row < (s + 1) * rows, mod_ref[s][:, lo:lo + d], out)
        return out

    u = stacked(cu_ref).astype(F32)
    has_prev = jnp.logical_and(i != 0, i != nct)
    has_next = jnp.logical_and(i != nct - 1, i != nt - 1)
    u_prev = pltpu.roll(u, 1, axis=0)
    u_next = pltpu.roll(u, n - 1, axis=0)
    for s in range(ns):
        prev_row = jnp.where(has_prev, cup_ref[s, HALO - 1:HALO, :].astype(F32), 0.0)
        next_row = jnp.where(has_next, cun_ref[s, 0:1, :].astype(F32), 0.0)
        u_prev = jnp.where(row == s * rows, prev_row, u_prev)
        u_next = jnp.where(row == (s + 1) * rows - 1, next_row, u_next)
    cw = cw_ref[...]
    conv = cw[0:1, :] * u_prev + cw[1:2, :] * u + cw[2:3, :] * u_next
    bb = (stacked(cb_ref).astype(F32) * conv).astype(BF16)
    y = (stacked(ga_ref).astype(F32) * _dot(stacked(a_ref), wa_ref[...])
         + stacked(gb_ref).astype(F32) * _dot(bb, wb_ref[...])
         + stacked(gc_ref).astype(F32) * _dot(stacked(att_ref), wc_ref[...]))
    x = jnp.where(i < nct, stacked(xa_ref), stacked(xb_ref))
    x1 = x + per_sample(2 * d) * _dot(y.astype(BF16), wo_ref[...])
    x1_ref[...] = x1.reshape(ns, rows, d)
    h2 = _modulated_norm(x1, g2_ref[...], per_sample(3 * d), per_sample(4 * d))
    h2_ref[...] = h2.astype(BF16).reshape(ns, rows, d)
    logits = _dot_nt(rwt_ref[...], h2, precision=HIGHEST)
    ex = jnp.exp(logits - jnp.max(logits, axis=0, keepdims=True))
    aff = ex / jnp.sum(ex, axis=0, keepdims=True)
    for s in range(ns):
        aff_ref[s] = aff[:, s * rows:(s + 1) * rows]


def _merge(stream, mod_l, a, cb, cu, att, ga, gb, gc, conv_w, wa, wb, wc, wo, norm2_g, rwt, nct):
    b, t, d = a.shape[0], a.shape[1], stream[0].shape[2]
    nt = t // TOK_TILE
    ne = rwt.shape[0]
    ns = MERGE_SAMPLES if b % MERGE_SAMPLES == 0 else 1
    mod_m = jnp.concatenate([mod_l[:b]] + [mod_l[b:b + 1]] * ns, axis=0)
    ctx_blk = b // ns
    sub = TOK_TILE // HALO
    n8 = t // HALO

    def tok(width):
        return pl.BlockSpec((ns, TOK_TILE, width), lambda bi, i: (bi, i, 0))

    def const(shape):
        return pl.BlockSpec(shape, lambda bi, i: (0,) * len(shape))

    cwid = cu.shape[2]
    return pl.pallas_call(
        functools.partial(_merge_kernel, d=d, nct=nct, nt=nt),
        grid=(b // ns, nt),
        in_specs=_stream_specs(stream, nct, TOK_TILE, ns) + [
            pl.BlockSpec((ns, 1, N_MOD * d), lambda bi, i: (jnp.where(i < nct, ctx_blk, bi), 0, 0)),
            tok(a.shape[2]), tok(cwid), tok(cwid),
            pl.BlockSpec((ns, HALO, cwid), lambda bi, i: (bi, jnp.maximum(i * sub - 1, 0), 0)),
            pl.BlockSpec((ns, HALO, cwid), lambda bi, i: (bi, jnp.minimum((i + 1) * sub, n8 - 1), 0)),
            tok(att.shape[2]), tok(d), tok(d), tok(d),
            const(conv_w.shape), const(wa.shape), const(wb.shape), const(wc.shape), const(wo.shape),
            const((1, d)), const(rwt.shape),
        ],
        out_specs=[tok(d), tok(d), pl.BlockSpec((ns, ne, TOK_TILE), lambda bi, i: (bi, 0, i))],
        out_shape=[jax.ShapeDtypeStruct((b, t, d), F32), jax.ShapeDtypeStruct((b, t, d), BF16),
                   jax.ShapeDtypeStruct((b, ne, t), F32)],
        compiler_params=_params(("parallel", "parallel")),
        name="merge_residual_router",
    )(stream[0], stream[1], mod_m, a, cb, cu, cu, cu, att, ga, gb, gc, conv_w, wa, wb, wc, wo, norm2_g, rwt)


def _prefix_count(mask, tri):
    e, n = mask.shape
    carry = jnp.zeros((e, 1), F32)
    outs = []
    for blk in range(n // LANES):
        xb = jnp.where(mask[:, blk * LANES:(blk + 1) * LANES], 1.0, 0.0).astype(BF16)
        pre = _dot(xb, tri) + carry
        carry = pre[:, LANES - 1:LANES]
        outs.append(pre)
    return jnp.concatenate(outs, axis=1) if len(outs) > 1 else outs[0]


def _topk_kernel(aff_ref, tri_ref, tind_ref, sut_ref, pos_ref, before_ref, *, off, n, cap):
    a = aff_ref[0][:, off:off + n]
    bits = pltpu.bitcast(a, jnp.int32)
    thr = jnp.zeros((a.shape[0], 1), jnp.int32)
    for bit in range(30, -1, -1):
        cand = thr | (1 << bit)
        cnt = jnp.sum(jnp.where(bits >= cand, 1.0, 0.0), axis=1, keepdims=True)
        thr = jnp.where(cnt >= cap, cand, thr)
    gt = bits > thr
    eq = bits == thr
    need = cap - jnp.sum(jnp.where(gt, 1.0, 0.0), axis=1, keepdims=True)
    tri = tri_ref[...]
    eq_rank = _prefix_count(eq, tri)
    sel = jnp.logical_or(gt, jnp.logical_and(eq, eq_rank <= need))
    slot = _prefix_count(sel, tri) - 1.0
    pos_ref[0] = jnp.where(sel, slot, -1.0).astype(jnp.int32)
    tile_cnt = _dot(jnp.where(sel, 1.0, 0.0).astype(BF16), tind_ref[...])
    before_ref[0] = _dot(tile_cnt.astype(BF16), sut_ref[...]).astype(jnp.int32)


def _topk_positions(aff_t, tri, off, n, cap):
    b, ne, t = aff_t.shape
    assert n // TOK_TILE < LANES and TOK_TILE <= 256
    lane = jnp.arange(LANES)
    tind = (jnp.arange(n)[:, None] // TOK_TILE == lane[None, :]).astype(BF16)
    sut = (lane[:, None] < lane[None, :]).astype(BF16)

    def const(shape):
        return pl.BlockSpec(shape, lambda bi: (0,) * len(shape))

    return pl.pallas_call(
        functools.partial(_topk_kernel, off=off, n=n, cap=cap),
        grid=(b,),
        in_specs=[pl.BlockSpec((1, ne, t), lambda bi: (bi, 0, 0)), const((LANES, LANES)), const(tind.shape),
                  const(sut.shape)],
        out_specs=[pl.BlockSpec((1, ne, n), lambda bi: (bi, 0, 0)), pl.BlockSpec((1, ne, LANES), lambda bi: (bi, 0, 0))],
        out_shape=[jax.ShapeDtypeStruct((b, ne, n), jnp.int32), jax.ShapeDtypeStruct((b, ne, LANES), jnp.int32)],
        compiler_params=_params(("parallel",)),
        name="expert_choice_topk",
    )(aff_t, tri, tind, sut)


def _gather_kernel(before_ref, h_ref, pos_ref, xs_ref, *, ne, cap, win):
    bi = pl.program_id(0)
    tt = pl.program_id(1)

    @pl.when(tt == 0)
    def _():
        xs_ref[...] = jnp.zeros_like(xs_ref)

    h = h_ref[0]
    pos = pos_ref[0]
    rows = h.shape[0]
    sub = lax.broadcasted_iota(jnp.int32, (win, rows), 0)
    hots, spans = [], []
    overflow = False
    for e in range(ne):
        base = (bi * ne + e) * LANES
        c0 = before_ref[base + tt]
        c1 = before_ref[base + tt + 1]
        w0 = pl.multiple_of(jnp.minimum((c0 // SLOT_ALIGN) * SLOT_ALIGN, cap - win), SLOT_ALIGN)
        hots.append(jnp.where(pos[e:e + 1, :] - w0 == sub, 1.0, 0.0).astype(BF16))
        spans.append((c1, w0))
        overflow = jnp.logical_or(overflow, c1 > w0 + win)
    picked = _dot(jnp.concatenate(hots, axis=0), h).astype(BF16)
    for e, (c1, w0) in enumerate(spans):
        xs_ref[e, pl.ds(w0, win), :] += picked[e * win:(e + 1) * win, :]

    @pl.when(overflow)
    def _():
        for e, (c1, w0) in enumerate(spans):
            def more(k, carry, e=e, w0=w0):
                ws = w0 + k * win
                wc = pl.multiple_of(jnp.minimum(ws, cap - win), SLOT_ALIGN)
                hit = jnp.logical_and(pos_ref[0][e:e + 1, :] - wc == sub, sub >= ws - wc)
                xs_ref[e, pl.ds(wc, win), :] += _dot(jnp.where(hit, 1.0, 0.0).astype(BF16), h_ref[0]).astype(BF16)
                return carry

            lax.fori_loop(1, (c1 - w0 + win - 1) // win, more, 0)


def _gather_tokens(h2, pos, before, off, n, cap):
    b, t, d = h2.shape
    ne = pos.shape[1]
    ot = off // TOK_TILE
    win = min(cap, SLOT_WINDOW)
    grid_spec = pltpu.PrefetchScalarGridSpec(
        num_scalar_prefetch=1,
        grid=(b, n // TOK_TILE),
        in_specs=[pl.BlockSpec((1, TOK_TILE, d), lambda bi, tt, bf: (bi, tt + ot, 0)),
                  pl.BlockSpec((1, ne, TOK_TILE), lambda bi, tt, bf: (bi, 0, tt))],
        out_specs=pl.BlockSpec((ne, cap, d), lambda bi, tt, bf: (0, bi, 0)),
    )
    return pl.pallas_call(
        functools.partial(_gather_kernel, ne=ne, cap=cap, win=win),
        grid_spec=grid_spec,
        out_shape=jax.ShapeDtypeStruct((ne, b * cap, d), BF16),
        compiler_params=_params(("parallel", "arbitrary")),
        name="expert_gather",
    )(before.reshape(-1), h2, pos)


def _ffn_kernel(xs_ref, wg_ref, wu_ref, wd_ref, ys_ref, wgb_ref, wub_ref, wdb_ref):
    @pl.when(pl.program_id(1) == 0)
    def _():
        wgb_ref[...] = wg_ref[0, 0].astype(BF16)
        wub_ref[...] = wu_ref[0, 0].astype(BF16)
        wdb_ref[...] = wd_ref[0, 0].astype(BF16)

    x = xs_ref[0]
    g = _dot(x, wgb_ref[...])
    hid = (g * _sigmoid(g)) * _dot(x, wub_ref[...])
    ys_ref[0] = _dot(hid.astype(BF16), wdb_ref[...]).astype(BF16)


def _expert_ffn(xs, wg, wu, wd, layer):
    ne, m, d = xs.shape
    ff = wg.shape[3]
    tm = min(m, 512)
    return pl.pallas_call(
        _ffn_kernel,
        grid=(ne, m // tm),
        in_specs=[pl.BlockSpec((1, tm, d), lambda e, j: (e, j, 0)),
                  pl.BlockSpec((1, 1, d, ff), lambda e, j: (layer, e, 0, 0)),
                  pl.BlockSpec((1, 1, d, ff), lambda e, j: (layer, e, 0, 0)),
                  pl.BlockSpec((1, 1, ff, d), lambda e, j: (layer, e, 0, 0))],
        out_specs=pl.BlockSpec((1, tm, d), lambda e, j: (e, j, 0)),
        out_shape=jax.ShapeDtypeStruct((ne, m, d), BF16),
        scratch_shapes=[pltpu.VMEM((d, ff), BF16), pltpu.VMEM((d, ff), BF16), pltpu.VMEM((ff, d), BF16)],
        compiler_params=_params(("parallel", "arbitrary")),
        name="expert_ffn",
    )(xs, wg, wu, wd)


def _scatter_kernel(before_ref, x_ref, mod_ref, ys_ref, pos_ref, aff_ref, fg_ref, o_ref, acc_ref, *,
                    d, ne, cap, win, final):
    bi = pl.program_id(0)
    i = pl.program_id(1)
    m = mod_ref[0]
    pos = pos_ref[0]
    aff = aff_ref[0]
    rows = pos.shape[0]
    lane = lax.broadcasted_iota(jnp.int32, (rows, win), 1)
    acc = jnp.zeros((rows, d), F32)
    spans = []
    overflow = False
    for e in range(ne):
        base = (bi * ne + e) * LANES
        c0 = before_ref[base + i]
        c1 = before_ref[base + i + 1]
        w0 = pl.multiple_of(jnp.minimum((c0 // SLOT_ALIGN) * SLOT_ALIGN, cap - win), SLOT_ALIGN)
        onehot = jnp.where(pos[:, e:e + 1] - w0 == lane, 1.0, 0.0).astype(BF16)
        acc = acc + aff[:, e:e + 1] * _dot(onehot, ys_ref[e, pl.ds(w0, win), :])
        spans.append((c1, w0))
        overflow = jnp.logical_or(overflow, c1 > w0 + win)
    acc_ref[...] = acc

    @pl.when(overflow)
    def _():
        for e, (c1, w0) in enumerate(spans):
            def more(k, carry, e=e, w0=w0):
                ws = w0 + k * win
                wc = pl.multiple_of(jnp.minimum(ws, cap - win), SLOT_ALIGN)
                hit = jnp.logical_and(pos_ref[0][:, e:e + 1] - wc == lane, lane >= ws - wc)
                acc_ref[...] += aff_ref[0][:, e:e + 1] * _dot(jnp.where(hit, 1.0, 0.0).astype(BF16),
                                                              ys_ref[e, pl.ds(wc, win), :])
                return carry

            lax.fori_loop(1, (c1 - w0 + win - 1) // win, more, 0)

    x2 = x_ref[0] + m[:, 5 * d:6 * d] * acc_ref[...]
    if final:
        ms = jnp.mean(x2 * x2, axis=-1, keepdims=True)
        x2 = x2 * lax.rsqrt(ms + EPS) * fg_ref[...]
    o_ref[0] = x2


def _scatter_residual(x1, mod_l, ys, pos_n, aff_n, before, final_g, off, n, cap, mod_row_ctx, final):
    b, t, d = x1.shape
    ne = ys.shape[0]
    nb = mod_l.shape[0] - 1
    ot = off // TOK_TILE
    win = min(cap, SLOT_WINDOW)
    if final:
        out_shape = jax.ShapeDtypeStruct((b, n, d), F32)
        out_spec = pl.BlockSpec((1, TOK_TILE, d), lambda bi, i, bf: (bi, i, 0))
        aliases = {}
    else:
        out_shape = jax.ShapeDtypeStruct((b, t, d), F32)
        out_spec = pl.BlockSpec((1, TOK_TILE, d), lambda bi, i, bf: (bi, i + ot, 0))
        aliases = {1: 0}
    grid_spec = pltpu.PrefetchScalarGridSpec(
        num_scalar_prefetch=1,
        grid=(b, n // TOK_TILE),
        in_specs=[
            pl.BlockSpec((1, TOK_TILE, d), lambda bi, i, bf: (bi, i + ot, 0)),
            pl.BlockSpec((1, 1, N_MOD * d), lambda bi, i, bf: (nb if mod_row_ctx else bi, 0, 0)),
            pl.BlockSpec((ne, cap, d), lambda bi, i, bf: (0, bi, 0)),
            pl.BlockSpec((1, TOK_TILE, ne), lambda bi, i, bf: (bi, i, 0)),
            pl.BlockSpec((1, TOK_TILE, ne), lambda bi, i, bf: (bi, i + ot, 0)),
            pl.BlockSpec((1, d), lambda bi, i, bf: (0, 0)),
        ],
        out_specs=out_spec,
        scratch_shapes=[pltpu.VMEM((TOK_TILE, d), F32)],
    )
    return pl.pallas_call(
        functools.partial(_scatter_kernel, d=d, ne=ne, cap=cap, win=win, final=final),
        grid_spec=grid_spec,
        out_shape=out_shape,
        input_output_aliases=aliases,
        compiler_params=_params(("parallel", "parallel")),
        name="expert_scatter_residual",
    )(before.reshape(-1), x1, mod_l, ys, pos_n, aff_n, final_g)


def _rope_tables(ctx_len, seq, hd):
    rows = seq // GRID_W
    row = jnp.repeat(jnp.arange(rows), GRID_W).astype(F32)
    col = jnp.tile(jnp.arange(GRID_W), rows).astype(F32)
    inv = ROPE_THETA ** (-jnp.arange(0, hd // 2, 2, dtype=F32) / (hd // 2))
    ang = jnp.concatenate([row[:, None] * inv, col[:, None] * inv], axis=-1)
    cos = jnp.repeat(jnp.cos(ang), 2, axis=-1)
    sin = jnp.repeat(jnp.sin(ang), 2, axis=-1) * jnp.tile(jnp.array([-1.0, 1.0], F32), hd // 2)
    cos = jnp.concatenate([jnp.ones((ctx_len, hd), F32), cos], axis=0)
    sin = jnp.concatenate([jnp.zeros((ctx_len, hd), F32), sin], axis=0)
    rep = LANES // hd
    return jnp.tile(cos, (1, rep)), jnp.tile(sin, (1, rep))


def kernel(x, c, ctx, c_ctx, ada_w, ada_b, norm1_g, norm2_g, w_in, hg_lb_logits, hg_norm_g, conv_w, q_norm_g, k_norm_g, w_proj_a, w_proj_b, w_proj_c, w_out, router_w, w_gate, w_up, w_down, final_norm_g):
    b, s, d = x.shape
    ctx_len = ctx.shape[1]
    t = ctx_len + s
    depth = w_in.shape[0]
    hgw = hg_norm_g.shape[1]
    cw = conv_w.shape[2]
    aw = w_proj_c.shape[1]
    hd = q_norm_g.shape[1]
    kvw = (w_in.shape[2] - 5 * hgw - 3 * cw - aw - 3 * d) // 2
    ne = router_w.shape[2]
    nct = ctx_len // TOK_TILE
    assert ctx_len % TOK_TILE == 0 and s % TOK_TILE == 0 and LANES % hd == 0 and cw == d
    dims = dict(hgw=hgw, cw=cw, aw=aw, kvw=kvw, hd=hd, nct=nct, t=t)

    stream = (ctx, x, nct)
    n_rows = -(-(b + 1) // 8) * 8
    cvec = jnp.concatenate([c, c_ctx[None, :], jnp.zeros((n_rows - b - 1, d), F32)], axis=0)
    mod = _modulation(cvec, ada_w, ada_b)
    mod = mod[:, :b + 1].reshape(depth, b + 1, 1, N_MOD * d)

    cos_t, sin_t = _rope_tables(ctx_len, s, hd)
    lane = jnp.arange(LANES)
    gmat = (lane[:, None] // hd == lane[None, :] // hd).astype(BF16)
    tri = (lane[:, None] <= lane[None, :]).astype(BF16)
    rep = LANES // hd
    lb_logits = hg_lb_logits.astype(F32)

    out = None
    for l in range(depth):
        last = l == depth - 1
        mod_l = mod[l]
        (hq, zf, zb, hi, hog, cb, cu, aq, ak, av, ga, gb, gc) = _in_projection(
            stream, mod_l, norm1_g[l][None, :], w_in[l].astype(BF16), cos_t, sin_t,
            jnp.tile(q_norm_g[l], rep)[None, :], jnp.tile(k_norm_g[l], rep)[None, :], gmat, dims)
        a = _hgrn(hq, zf, zb, hi, hog, lb_logits, hg_norm_g[l][None, :], l, ctx_len)
        att = _attention(aq, ak, av, ctx_len)
        x1, h2, aff_t = _merge(stream, mod_l, a, cb, cu, att, ga, gb, gc, conv_w[l],
                               w_proj_a[l].astype(BF16), w_proj_b[l].astype(BF16), w_proj_c[l].astype(BF16),
                               w_out[l].astype(BF16), norm2_g[l][None, :], router_w[l].T, nct)
        aff_n = jnp.swapaxes(aff_t, 1, 2)

        def moe(xs_in, off, n, mod_row_ctx, final):
            cap = CAPACITY_FACTOR * n // ne
            pos, before = _topk_positions(aff_t, tri, off, n, cap)
            xs = _gather_tokens(h2, pos, before, off, n, cap)
            ys = _expert_ffn(xs, w_gate, w_up, w_down, l)
            return _scatter_residual(xs_in, mod_l, ys, jnp.swapaxes(pos, 1, 2), aff_n, before,
                                     final_norm_g[None, :], off, n, cap, mod_row_ctx, final)

        if last:
            out = moe(x1, ctx_len, s, False, True)
        else:
            xc = moe(x1, ctx_len, s, False, False)
            xc = moe(xc, 0, ctx_len, True, False)
            stream = (xc, xc, 0)
    return out
```

```python
import functools

import jax
import jax.numpy as jnp
from jax import lax
from jax.experimental import pallas as pl
from jax.experimental.pallas import tpu as pltpu

F32 = jnp.float32
BF16 = jnp.bfloat16

EPS = 1e-6
N_MOD = 6
HG_DK = 128
GRID_W = 64
ROPE_THETA = 10000.0
CAPACITY_FACTOR = 2

LANES = 128
TOK_TILE = 256
HG_CHUNK = 64
HG_SUB = 32
HG_HEADS_PER_STEP = 2
HG_UNROLL = 4
HALO = 16
ATTN_GROUPS_PER_STEP = 4
SLOT_WINDOW = 128
SLOT_ALIGN = 16
MERGE_SAMPLES = 2
LOG2E = 1.4426950408889634
VMEM_LIMIT = 56 * 1024 * 1024

HIGHEST = lax.Precision.HIGHEST


def _dot(a, b, precision=None):
    return jnp.dot(a, b, preferred_element_type=F32, precision=precision)


def _dot_nt(a, b, precision=None):
    return lax.dot_general(a, b, (((1,), (1,)), ((), ())), preferred_element_type=F32, precision=precision)


def _dot_tn(a, b):
    return lax.dot_general(a, b, (((0,), (0,)), ((), ())), preferred_element_type=F32)


def _sigmoid(x):
    return 1.0 / (1.0 + jnp.exp(-x))


def _params(sem, vmem=VMEM_LIMIT):
    return pltpu.CompilerParams(dimension_semantics=sem, vmem_limit_bytes=vmem)


def _mod_kernel(c_ref, w_ref, b_ref, o_ref):
    c = c_ref[...]
    sc = c * _sigmoid(c)
    o_ref[0] = _dot(sc, w_ref[0], precision=HIGHEST) + b_ref[0]


def _modulation(cvec, ada_w, ada_b):
    depth, d, n = ada_w.shape
    r = cvec.shape[0]
    tn = d
    return pl.pallas_call(
        _mod_kernel,
        grid=(depth, n // tn),
        in_specs=[
            pl.BlockSpec((r, d), lambda l, j: (0, 0)),
            pl.BlockSpec((1, d, tn), lambda l, j: (l, 0, j)),
            pl.BlockSpec((1, 1, tn), lambda l, j: (l, 0, j)),
        ],
        out_specs=pl.BlockSpec((1, r, tn), lambda l, j: (l, 0, j)),
        out_shape=jax.ShapeDtypeStruct((depth, r, n), F32),
        compiler_params=_params(("parallel", "parallel")),
        name="adaln_modulation",
    )(cvec, ada_w, ada_b.reshape(depth, 1, n))


def _modulated_norm(x, g, shift, scale):
    ms = jnp.mean(x * x, axis=-1, keepdims=True)
    return (x * lax.rsqrt(ms + EPS) * g) * (1.0 + scale) + shift


def _headnorm_rope(p, gain, gmat, cos, sin, hd, post_scale):
    rows, width = p.shape
    lane = lax.broadcasted_iota(jnp.int32, (rows, LANES), 1)
    even = (lane % 2) == 0
    outs = []
    for cb in range(width // LANES):
        xb = p[:, cb * LANES:(cb + 1) * LANES]
        ss = _dot((xb * xb).astype(BF16), gmat)
        y = xb * lax.rsqrt(ss * (1.0 / hd) + EPS) * gain
        y_next = pltpu.roll(y, LANES - 1, axis=1)
        y_prev = pltpu.roll(y, 1, axis=1)
        ysw = jnp.where(even, y_next, y_prev)
        outs.append((y * cos + ysw * sin) * post_scale)
    return jnp.concatenate(outs, axis=1) if len(outs) > 1 else outs[0]


def _lower_bounds(lg, layer):
    e = jnp.exp(lg - jnp.max(lg, axis=0, keepdims=True))
    sm = e / jnp.sum(e, axis=0, keepdims=True)
    lb = jnp.zeros(lg.shape[1:], F32)
    for j in range(1, layer + 1):
        lb = lb + sm[j]
    return lb


def _forget_gate(z, lbd, lb_is_zero):
    zs = z * LOG2E
    sp = jnp.log2(1.0 + jnp.exp2(-jnp.abs(zs)))
    ls = jnp.minimum(zs, 0.0) - sp
    sneg = jnp.exp2(jnp.minimum(-zs, 0.0) - sp)
    if lb_is_zero:
        return ls, sneg
    a = jnp.log2(lbd)
    t = jnp.log2(1.0 - lbd) + ls
    mx = jnp.maximum(a, t)
    mn = jnp.minimum(a, t)
    return mx + jnp.log2(1.0 + jnp.exp2(mn - mx)), (1.0 - lbd) * sneg


def _stream_specs(stream, nct, block_rows, ns=1):
    _, _, lat_off = stream
    d = stream[0].shape[2]
    return [pl.BlockSpec((ns, block_rows, d), lambda bi, i: (bi, jnp.minimum(i, nct - 1), 0)),
            pl.BlockSpec((ns, block_rows, d), lambda bi, i: (bi, jnp.maximum(i, nct) - lat_off, 0))]


def _inproj_kernel(xa_ref, xb_ref, mod_ref, g_ref, w_ref, cos_ref, sin_ref, qg_ref, kg_ref, gmat_ref,
                   hq_ref, zf_ref, zb_ref, hi_ref, hog_ref, cb_ref, cu_ref, aq_ref, ak_ref,
                   av_ref, ga_ref, gb_ref, gc_ref, *, d, hgw, cw, aw, kvw, hd, nct):
    x = jnp.where(pl.program_id(1) < nct, xa_ref[0], xb_ref[0])
    m = mod_ref[0]
    h = _modulated_norm(x, g_ref[...], m[:, 0:d], m[:, d:2 * d]).astype(BF16)

    def proj(lo, width):
        return _dot(h, w_ref[:, lo:lo + width])

    o_hq, o_zf, o_zb, o_hi, o_hog = (j * hgw for j in range(5))
    o_cb, o_cc, o_cx = (5 * hgw + j * cw for j in range(3))
    o_q = 5 * hgw + 3 * cw
    o_k, o_v = o_q + aw, o_q + aw + kvw
    o_ga, o_gb, o_gc = (o_v + kvw + j * d for j in range(3))
    cos = cos_ref[...]
    sin = sin_ref[...]
    gmat = gmat_ref[...]

    def gate(o, ref):
        ref[0] = _sigmoid(proj(o, d)).astype(BF16)

    hq_ref[0] = proj(o_hq, hgw).astype(BF16)
    zf_ref[0] = proj(o_zf, hgw)
    zb_ref[0] = proj(o_zb, hgw)
    hi_ref[0] = proj(o_hi, hgw).astype(BF16)
    g = proj(o_hog, hgw)
    hog_ref[0] = (g * _sigmoid(g)).astype(BF16)
    cb_ref[0] = proj(o_cb, cw).astype(BF16)
    cu_ref[0] = (proj(o_cc, cw) * proj(o_cx, cw)).astype(BF16)
    aq_ref[0] = _headnorm_rope(proj(o_q, aw), qg_ref[...], gmat, cos, sin, hd, hd ** -0.5 * LOG2E).astype(BF16)
    kn_t = _headnorm_rope(proj(o_k, kvw), kg_ref[...], gmat, cos, sin, hd, 1.0).T.astype(BF16)
    vv = proj(o_v, kvw).astype(BF16)
    ones_col = jnp.where(lax.broadcasted_iota(jnp.int32, (vv.shape[0], hd), 1) == 0, 1.0, 0.0).astype(BF16)
    for g in range(kvw // hd):
        ak_ref[0, g] = kn_t[g * hd:(g + 1) * hd, :]
        av_ref[0, g] = jnp.concatenate([vv[:, g * hd:(g + 1) * hd], ones_col], axis=1)
    gate(o_ga, ga_ref)
    gate(o_gb, gb_ref)
    gate(o_gc, gc_ref)


def _in_projection(stream, mod_l, norm_g, w_bf, cos_t, sin_t, qg, kg, gmat, dims):
    b, _, d = stream[0].shape
    t = dims["t"]
    hgw, cw, aw, kvw, hd, nct = dims["hgw"], dims["cw"], dims["aw"], dims["kvw"], dims["hd"], dims["nct"]
    nt = t // TOK_TILE
    in_w = w_bf.shape[1]
    nb = mod_l.shape[0] - 1

    def tok(width):
        return pl.BlockSpec((1, TOK_TILE, width), lambda bi, i: (bi, i, 0))

    def const(shape):
        return pl.BlockSpec(shape, lambda bi, i: (0,) * len(shape))

    widths = [(hgw, BF16), (hgw, F32), (hgw, F32), (hgw, BF16), (hgw, BF16),
              (cw, BF16), (cw, BF16), (aw, BF16), (kvw, BF16), (kvw, BF16), (d, BF16), (d, BF16), (d, BF16)]
    ng = kvw // hd
    out_specs = [tok(w) for w, _ in widths]
    out_shape = [jax.ShapeDtypeStruct((b, t, w), dt) for w, dt in widths]
    out_specs[8] = pl.BlockSpec((1, ng, hd, TOK_TILE), lambda bi, i: (bi, 0, 0, i))
    out_shape[8] = jax.ShapeDtypeStruct((b, ng, hd, t), BF16)
    out_specs[9] = pl.BlockSpec((1, ng, TOK_TILE, 2 * hd), lambda bi, i: (bi, 0, i, 0))
    out_shape[9] = jax.ShapeDtypeStruct((b, ng, t, 2 * hd), BF16)
    return pl.pallas_call(
        functools.partial(_inproj_kernel, d=d, hgw=hgw, cw=cw, aw=aw, kvw=kvw, hd=hd, nct=nct),
        grid=(b, nt),
        in_specs=_stream_specs(stream, nct, TOK_TILE) + [
            pl.BlockSpec((1, 1, N_MOD * d), lambda bi, i: (jnp.where(i < nct, nb, bi), 0, 0)),
            const((1, d)),
            pl.BlockSpec((d, in_w), lambda bi, i: (0, 0), pipeline_mode=pl.Buffered(1)),
            pl.BlockSpec((TOK_TILE, LANES), lambda bi, i: (i, 0)),
            pl.BlockSpec((TOK_TILE, LANES), lambda bi, i: (i, 0)),
            const((1, LANES)),
            const((1, LANES)),
            const((LANES, LANES)),
        ],
        out_specs=out_specs,
        out_shape=out_shape,
        compiler_params=_params(("parallel", "parallel")),
        name="in_projection",
    )(stream[0], stream[1], mod_l, norm_g, w_bf, cos_t, sin_t, qg, kg, gmat)


def _chunk_cumsum(x, reverse):
    n = x.shape[0]
    pos = lax.broadcasted_iota(jnp.int32, x.shape, 0) % HG_CHUNK
    s = 1
    while s < HG_CHUNK:
        if reverse:
            x = x + jnp.where(pos < HG_CHUNK - s, pltpu.roll(x, n - s, axis=0), 0.0)
        else:
            x = x + jnp.where(pos >= s, pltpu.roll(x, s, axis=0), 0.0)
        s *= 2
    return x


def _hgrn_chunk(q, z, lbd, lb_is_zero, v, st_ref, reverse):
    c = q.shape[0]
    lf, kk = _forget_gate(z, lbd, lb_is_zero)
    cs = _chunk_cumsum(lf, reverse)
    nsb = c // HG_SUB
    anchors = []
    for i in range(nsb):
        r = i * HG_SUB + (HG_SUB // 2 if reverse else HG_SUB // 2 - 1)
        anchors.append(cs[r:r + 1, :])
    c_anchor = jnp.concatenate([jnp.broadcast_to(a, (HG_SUB, a.shape[1])) for a in anchors], axis=0)
    c_end = cs[0:1, :] if reverse else cs[c - 1:c, :]
    qh = q * jnp.exp2(cs - c_anchor)

    def seg_rows(i, rows_of):
        return jnp.concatenate([rows_of(j) if keep_j else jnp.zeros((HG_SUB, q.shape[1]), F32)
                                for j, keep_j in enumerate(i)], axis=0)

    q_ext = jnp.concatenate(
        [seg_rows([j == i for j in range(nsb)], lambda j: qh[j * HG_SUB:(j + 1) * HG_SUB]) for i in range(nsb)],
        axis=1)
    k_ext = jnp.concatenate(
        [seg_rows([(j >= i) if reverse else (j <= i) for j in range(nsb)],
                  lambda j, i=i: kk[j * HG_SUB:(j + 1) * HG_SUB]
                  * jnp.exp2(anchors[i] - cs[j * HG_SUB:(j + 1) * HG_SUB])) for i in range(nsb)],
        axis=1)
    a = _dot_nt(q_ext.astype(BF16), k_ext.astype(BF16))
    row = lax.broadcasted_iota(jnp.int32, (c, c), 0)
    col = lax.broadcasted_iota(jnp.int32, (c, c), 1)
    keep = (col >= row) if reverse else (col <= row)
    st = st_ref[...]
    o = (_dot(jnp.where(keep, a, 0.0).astype(BF16), v)
         + _dot_nt((qh * jnp.exp2(c_anchor)).astype(BF16), st.astype(BF16)))
    kh = (kk * jnp.exp2(c_end - cs)).astype(BF16)
    st_ref[...] = st * jnp.exp2(c_end) + _dot_tn(v, kh)
    return o


def _hgrn_kernel(lbl_ref, hq_ref, zf_ref, zb_ref, hi_ref, hog_ref, ng_ref, a_ref,
                 of_ref, ob_ref, stf_ref, stb_ref, *, layer, n_chunks, n_ctx_chunks, n_heads):
    stf_ref[...] = jnp.zeros_like(stf_ref)
    stb_ref[...] = jnp.zeros_like(stb_ref)
    c = HG_CHUNK
    lb = _lower_bounds(lbl_ref[...], layer)

    def step(s, carry):
        rf = pl.multiple_of(s * c, c)
        cbk = jnp.where(s < n_ctx_chunks, n_ctx_chunks - 1 - s, n_chunks - 1 - (s - n_ctx_chunks))
        rb = pl.multiple_of(cbk * c, c)
        for h in range(n_heads):
            ln = slice(h * HG_DK, (h + 1) * HG_DK)
            of_ref[pl.ds(rf, c), ln] = _hgrn_chunk(
                hq_ref[0, pl.ds(rf, c), ln].astype(F32), zf_ref[0, pl.ds(rf, c), ln], lb[0:1, ln], layer == 0,
                hi_ref[0, pl.ds(rf, c), ln], stf_ref.at[h], False)
            ob_ref[pl.ds(rb, c), ln] = _hgrn_chunk(
                hq_ref[0, pl.ds(rb, c), ln].astype(F32), zb_ref[0, pl.ds(rb, c), ln], lb[1:2, ln], layer == 0,
                hi_ref[0, pl.ds(rb, c), ln], stb_ref.at[h], True)
        return carry

    lax.fori_loop(0, n_chunks, step, 0, unroll=HG_UNROLL)

    def readout(i, carry):
        r0 = pl.multiple_of(i * TOK_TILE, TOK_TILE)
        for h in range(n_heads):
            ln = slice(h * HG_DK, (h + 1) * HG_DK)
            o = of_ref[pl.ds(r0, TOK_TILE), ln] + ob_ref[pl.ds(r0, TOK_TILE), ln]
            ms = jnp.mean(o * o, axis=-1, keepdims=True)
            y = o * lax.rsqrt(ms + EPS) * ng_ref[:, ln]
            a_ref[0, pl.ds(r0, TOK_TILE), ln] = (y * hog_ref[0, pl.ds(r0, TOK_TILE), ln].astype(F32)).astype(BF16)
        return carry

    lax.fori_loop(0, (n_chunks * c) // TOK_TILE, readout, 0)


def _hgrn(hq, zf, zb, hi, hog, lb_logits, norm_g, layer, ctx_len):
    b, t, hgw = hq.shape
    nh = HG_HEADS_PER_STEP
    wid = nh * HG_DK
    depth = lb_logits.shape[0]

    def seq():
        return pl.BlockSpec((1, t, wid), lambda bi, h: (bi, 0, h))

    return pl.pallas_call(
        functools.partial(_hgrn_kernel, layer=layer, n_chunks=t // HG_CHUNK, n_ctx_chunks=ctx_len // HG_CHUNK,
                          n_heads=nh),
        grid=(b, hgw // wid),
        in_specs=[pl.BlockSpec((depth, 2, wid), lambda bi, h: (0, 0, h)),
                  seq(), seq(), seq(), seq(), seq(),
                  pl.BlockSpec((1, wid), lambda bi, h: (0, h))],
        out_specs=seq(),
        out_shape=jax.ShapeDtypeStruct((b, t, hgw), BF16),
        scratch_shapes=[pltpu.VMEM((t, wid), F32), pltpu.VMEM((t, wid), F32),
                        pltpu.VMEM((nh, HG_DK, HG_DK), F32), pltpu.VMEM((nh, HG_DK, HG_DK), F32)],
        compiler_params=_params(("parallel", "parallel")),
        name="hgrn2_bidirectional",
    )(lb_logits, hq, zf, zb, hi, hog, norm_g)


def _attn_kernel(q_ref, k_ref, v_ref, o_ref, *, n_g, group, hd, nct, ctx_len, t_all):
    i = pl.program_id(2)

    def run(n_keys):
        outs = []
        for g in range(n_g):
            kt = k_ref[0, g, :, 0:n_keys]
            v = v_ref[0, g, 0:n_keys, :]
            for h in range(group):
                c0 = (g * group + h) * hd
                s = _dot(q_ref[0, :, c0:c0 + hd], kt)
                p = jnp.exp2(s - jnp.max(s, axis=1, keepdims=True))
                ov = _dot(p.astype(BF16), v)
                outs.append((ov[:, 0:hd] / ov[:, hd:hd + 1]).astype(BF16))
        o_ref[0] = jnp.concatenate(outs, axis=1)

    @pl.when(i < nct)
    def _():
        run(ctx_len)

    @pl.when(i >= nct)
    def _():
        run(t_all)


def _attention(aq, ak, av, ctx_len):
    b, t, aw = aq.shape
    n_groups, hd = ak.shape[1], ak.shape[2]
    group = aw // (n_groups * hd)
    nt = t // TOK_TILE
    n_g = ATTN_GROUPS_PER_STEP if n_groups % ATTN_GROUPS_PER_STEP == 0 else 1
    return pl.pallas_call(
        functools.partial(_attn_kernel, n_g=n_g, group=group, hd=hd, nct=ctx_len // TOK_TILE, ctx_len=ctx_len,
                          t_all=t),
        grid=(b, n_groups // n_g, nt),
        in_specs=[
            pl.BlockSpec((1, TOK_TILE, n_g * group * hd), lambda bi, g, i: (bi, i, g)),
            pl.BlockSpec((1, n_g, hd, t), lambda bi, g, i: (bi, g, 0, 0)),
            pl.BlockSpec((1, n_g, t, 2 * hd), lambda bi, g, i: (bi, g, 0, 0)),
        ],
        out_specs=pl.BlockSpec((1, TOK_TILE, n_g * group * hd), lambda bi, g, i: (bi, i, g)),
        out_shape=jax.ShapeDtypeStruct((b, t, aw), BF16),
        compiler_params=_params(("parallel", "parallel", "parallel")),
        name="gqa_attention",
    )(aq, ak, av)


def _merge_kernel(xa_ref, xb_ref, mod_ref, a_ref, cb_ref, cu_ref, cup_ref, cun_ref, att_ref, ga_ref, gb_ref, gc_ref,
                  cw_ref, wa_ref, wb_ref, wc_ref, wo_ref, g2_ref, rwt_ref,
                  x1_ref, h2_ref, aff_ref, *, d, nct, nt):
    i = pl.program_id(1)
    ns, rows = xa_ref.shape[0], xa_ref.shape[1]
    n = ns * rows

    def stacked(ref):
        return ref[...].reshape(n, ref.shape[2])

    row = lax.broadcasted_iota(jnp.int32, (n, 1), 0)

    def per_sample(lo):
        out = mod_ref[ns - 1][:, lo:lo + d]
        for s in range(ns - 2, -1, -1):
            out = jnp.where(row < (s + 1) * rows, mod_ref[s][:, lo:lo + d], out)
        return out

    u = stacked(cu_ref).astype(F32)
    has_prev = jnp.logical_and(i != 0, i != nct)
    has_next = jnp.logical_and(i != nct - 1, i != nt - 1)
    u_prev = pltpu.roll(u, 1, axis=0)
    u_next = pltpu.roll(u, n - 1, axis=0)
    for s in range(ns):
        prev_row = jnp.where(has_prev, cup_ref[s, HALO - 1:HALO, :].astype(F32), 0.0)
        next_row = jnp.where(has_next, cun_ref[s, 0:1, :].astype(F32), 0.0)
        u_prev = jnp.where(row == s * rows, prev_row, u_prev)
        u_next = jnp.where(row == (s + 1) * rows - 1, next_row, u_next)
    cw = cw_ref[...]
    conv = cw[0:1, :] * u_prev + cw[1:2, :] * u + cw[2:3, :] * u_next
    bb = (stacked(cb_ref).astype(F32) * conv).astype(BF16)
    y = (stacked(ga_ref).astype(F32) * _dot(stacked(a_ref), wa_ref[...])
         + stacked(gb_ref).astype(F32) * _dot(bb, wb_ref[...])
         + stacked(gc_ref).astype(F32) * _dot(stacked(att_ref), wc_ref[...]))
    x = jnp.where(i < nct, stacked(xa_ref), stacked(xb_ref))
    x1 = x + per_sample(2 * d) * _dot(y.astype(BF16), wo_ref[...])
    x1_ref[...] = x1.reshape(ns, rows, d)
    h2 = _modulated_norm(x1, g2_ref[...], per_sample(3 * d), per_sample(4 * d))
    h2_ref[...] = h2.astype(BF16).reshape(ns, rows, d)
    logits = _dot_nt(rwt_ref[...], h2, precision=HIGHEST)
    ex = jnp.exp(logits - jnp.max(logits, axis=0, keepdims=True))
    aff = ex / jnp.sum(ex, axis=0, keepdims=True)
    for s in range(ns):
        aff_ref[s] = aff[:, s * rows:(s + 1) * rows]


def _merge(stream, mod_l, a, cb, cu, att, ga, gb, gc, conv_w, wa, wb, wc, wo, norm2_g, rwt, nct):
    b, t, d = a.shape[0], a.shape[1], stream[0].shape[2]
    nt = t // TOK_TILE
    ne = rwt.shape[0]
    ns = MERGE_SAMPLES if b % MERGE_SAMPLES == 0 else 1
    mod_m = jnp.concatenate([mod_l[:b]] + [mod_l[b:b + 1]] * ns, axis=0)
    ctx_blk = b // ns
    sub = TOK_TILE // HALO
    n8 = t // HALO

    def tok(width):
        return pl.BlockSpec((ns, TOK_TILE, width), lambda bi, i: (bi, i, 0))

    def const(shape):
        return pl.BlockSpec(shape, lambda bi, i: (0,) * len(shape))

    cwid = cu.shape[2]
    return pl.pallas_call(
        functools.partial(_merge_kernel, d=d, nct=nct, nt=nt),
        grid=(b // ns, nt),
        in_specs=_stream_specs(stream, nct, TOK_TILE, ns) + [
            pl.BlockSpec((ns, 1, N_MOD * d), lambda bi, i: (jnp.where(i < nct, ctx_blk, bi), 0, 0)),
            tok(a.shape[2]), tok(cwid), tok(cwid),
            pl.BlockSpec((ns, HALO, cwid), lambda bi, i: (bi, jnp.maximum(i * sub - 1, 0), 0)),
            pl.BlockSpec((ns, HALO, cwid), lambda bi, i: (bi, jnp.minimum((i + 1) * sub, n8 - 1), 0)),
            tok(att.shape[2]), tok(d), tok(d), tok(d),
            const(conv_w.shape), const(wa.shape), const(wb.shape), const(wc.shape), const(wo.shape),
            const((1, d)), const(rwt.shape),
        ],
        out_specs=[tok(d), tok(d), pl.BlockSpec((ns, ne, TOK_TILE), lambda bi, i: (bi, 0, i))],
        out_shape=[jax.ShapeDtypeStruct((b, t, d), F32), jax.ShapeDtypeStruct((b, t, d), BF16),
                   jax.ShapeDtypeStruct((b, ne, t), F32)],
        compiler_params=_params(("parallel", "parallel")),
        name="merge_residual_router",
    )(stream[0], stream[1], mod_m, a, cb, cu, cu, cu, att, ga, gb, gc, conv_w, wa, wb, wc, wo, norm2_g, rwt)


def _prefix_count(mask, tri):
    e, n = mask.shape
    carry = jnp.zeros((e, 1), F32)
    outs = []
    for blk in range(n // LANES):
        xb = jnp.where(mask[:, blk * LANES:(blk + 1) * LANES], 1.0, 0.0).astype(BF16)
        pre = _dot(xb, tri) + carry
        carry = pre[:, LANES - 1:LANES]
        outs.append(pre)
    return jnp.concatenate(outs, axis=1) if len(outs) > 1 else outs[0]


def _topk_kernel(aff_ref, tri_ref, tind_ref, sut_ref, pos_ref, before_ref, *, off, n, cap):
    a = aff_ref[0][:, off:off + n]
    bits = pltpu.bitcast(a, jnp.int32)
    thr = jnp.zeros((a.shape[0], 1), jnp.int32)
    for bit in range(30, -1, -1):
        cand = thr | (1 << bit)
        cnt = jnp.sum(jnp.where(bits >= cand, 1.0, 0.0), axis=1, keepdims=True)
        thr = jnp.where(cnt >= cap, cand, thr)
    gt = bits > thr
    eq = bits == thr
    need = cap - jnp.sum(jnp.where(gt, 1.0, 0.0), axis=1, keepdims=True)
    tri = tri_ref[...]
    eq_rank = _prefix_count(eq, tri)
    sel = jnp.logical_or(gt, jnp.logical_and(eq, eq_rank <= need))
    slot = _prefix_count(sel, tri) - 1.0
    pos_ref[0] = jnp.where(sel, slot, -1.0).astype(jnp.int32)
    tile_cnt = _dot(jnp.where(sel, 1.0, 0.0).astype(BF16), tind_ref[...])
    before_ref[0] = _dot(tile_cnt.astype(BF16), sut_ref[...]).astype(jnp.int32)


def _topk_positions(aff_t, tri, off, n, cap):
    b, ne, t = aff_t.shape
    assert n // TOK_TILE < LANES and TOK_TILE <= 256
    lane = jnp.arange(LANES)
    tind = (jnp.arange(n)[:, None] // TOK_TILE == lane[None, :]).astype(BF16)
    sut = (lane[:, None] < lane[None, :]).astype(BF16)

    def const(shape):
        return pl.BlockSpec(shape, lambda bi: (0,) * len(shape))

    return pl.pallas_call(
        functools.partial(_topk_kernel, off=off, n=n, cap=cap),
        grid=(b,),
        in_specs=[pl.BlockSpec((1, ne, t), lambda bi: (bi, 0, 0)), const((LANES, LANES)), const(tind.shape),
                  const(sut.shape)],
        out_specs=[pl.BlockSpec((1, ne, n), lambda bi: (bi, 0, 0)), pl.BlockSpec((1, ne, LANES), lambda bi: (bi, 0, 0))],
        out_shape=[jax.ShapeDtypeStruct((b, ne, n), jnp.int32), jax.ShapeDtypeStruct((b, ne, LANES), jnp.int32)],
        compiler_params=_params(("parallel",)),
        name="expert_choice_topk",
    )(aff_t, tri, tind, sut)


def _gather_kernel(before_ref, h_ref, pos_ref, xs_ref, *, ne, cap, win):
    bi = pl.program_id(0)
    tt = pl.program_id(1)

    @pl.when(tt == 0)
    def _():
        xs_ref[...] = jnp.zeros_like(xs_ref)

    h = h_ref[0]
    pos = pos_ref[0]
    rows = h.shape[0]
    sub = lax.broadcasted_iota(jnp.int32, (win, rows), 0)
    hots, spans = [], []
    overflow = False
    for e in range(ne):
        base = (bi * ne + e) * LANES
        c0 = before_ref[base + tt]
        c1 = before_ref[base + tt + 1]
        w0 = pl.multiple_of(jnp.minimum((c0 // SLOT_ALIGN) * SLOT_ALIGN, cap - win), SLOT_ALIGN)
        hots.append(jnp.where(pos[e:e + 1, :] - w0 == sub, 1.0, 0.0).astype(BF16))
        spans.append((c1, w0))
        overflow = jnp.logical_or(overflow, c1 > w0 + win)
    picked = _dot(jnp.concatenate(hots, axis=0), h).astype(BF16)
    for e, (c1, w0) in enumerate(spans):
        xs_ref[e, pl.ds(w0, win), :] += picked[e * win:(e + 1) * win, :]

    @pl.when(overflow)
    def _():
        for e, (c1, w0) in enumerate(spans):
            def more(k, carry, e=e, w0=w0):
                ws = w0 + k * win
                wc = pl.multiple_of(jnp.minimum(ws, cap - win), SLOT_ALIGN)
                hit = jnp.logical_and(pos_ref[0][e:e + 1, :] - wc == sub, sub >= ws - wc)
                xs_ref[e, pl.ds(wc, win), :] += _dot(jnp.where(hit, 1.0, 0.0).astype(BF16), h_ref[0]).astype(BF16)
                return carry

            lax.fori_loop(1, (c1 - w0 + win - 1) // win, more, 0)


def _gather_tokens(h2, pos, before, off, n, cap):
    b, t, d = h2.shape
    ne = pos.shape[1]
    ot = off // TOK_TILE
    win = min(cap, SLOT_WINDOW)
    grid_spec = pltpu.PrefetchScalarGridSpec(
        num_scalar_prefetch=1,
        grid=(b, n // TOK_TILE),
        in_specs=[pl.BlockSpec((1, TOK_TILE, d), lambda bi, tt, bf: (bi, tt + ot, 0)),
                  pl.BlockSpec((1, ne, TOK_TILE), lambda bi, tt, bf: (bi, 0, tt))],
        out_specs=pl.BlockSpec((ne, cap, d), lambda bi, tt, bf: (0, bi, 0)),
    )
    return pl.pallas_call(
        functools.partial(_gather_kernel, ne=ne, cap=cap, win=win),
        grid_spec=grid_spec,
        out_shape=jax.ShapeDtypeStruct((ne, b * cap, d), BF16),
        compiler_params=_params(("parallel", "arbitrary")),
        name="expert_gather",
    )(before.reshape(-1), h2, pos)


def _ffn_kernel(xs_ref, wg_ref, wu_ref, wd_ref, ys_ref, wgb_ref, wub_ref, wdb_ref):
    @pl.when(pl.program_id(1) == 0)
    def _():
        wgb_ref[...] = wg_ref[0, 0].astype(BF16)
        wub_ref[...] = wu_ref[0, 0].astype(BF16)
        wdb_ref[...] = wd_ref[0, 0].astype(BF16)

    x = xs_ref[0]
    g = _dot(x, wgb_ref[...])
    hid = (g * _sigmoid(g)) * _dot(x, wub_ref[...])
    ys_ref[0] = _dot(hid.astype(BF16), wdb_ref[...]).astype(BF16)


def _expert_ffn(xs, wg, wu, wd, layer):
    ne, m, d = xs.shape
    ff = wg.shape[3]
    tm = min(m, 512)
    return pl.pallas_call(
        _ffn_kernel,
        grid=(ne, m // tm),
        in_specs=[pl.BlockSpec((1, tm, d), lambda e, j: (e, j, 0)),
                  pl.BlockSpec((1, 1, d, ff), lambda e, j: (layer, e, 0, 0)),
                  pl.BlockSpec((1, 1, d, ff), lambda e, j: (layer, e, 0, 0)),
                  pl.BlockSpec((1, 1, ff, d), lambda e, j: (layer, e, 0, 0))],
        out_specs=pl.BlockSpec((1, tm, d), lambda e, j: (e, j, 0)),
        out_shape=jax.ShapeDtypeStruct((ne, m, d), BF16),
        scratch_shapes=[pltpu.VMEM((d, ff), BF16), pltpu.VMEM((d, ff), BF16), pltpu.VMEM((ff, d), BF16)],
        compiler_params=_params(("parallel", "arbitrary")),
        name="expert_ffn",
    )(xs, wg, wu, wd)


def _scatter_kernel(before_ref, x_ref, mod_ref, ys_ref, pos_ref, aff_ref, fg_ref, o_ref, acc_ref, *,
                    d, ne, cap, win, final):
    bi = pl.program_id(0)
    i = pl.program_id(1)
    m = mod_ref[0]
    pos = pos_ref[0]
    aff = aff_ref[0]
    rows = pos.shape[0]
    lane = lax.broadcasted_iota(jnp.int32, (rows, win), 1)
    acc = jnp.zeros((rows, d), F32)
    spans = []
    overflow = False
    for e in range(ne):
        base = (bi * ne + e) * LANES
        c0 = before_ref[base + i]
        c1 = before_ref[base + i + 1]
        w0 = pl.multiple_of(jnp.minimum((c0 // SLOT_ALIGN) * SLOT_ALIGN, cap - win), SLOT_ALIGN)
        onehot = jnp.where(pos[:, e:e + 1] - w0 == lane, 1.0, 0.0).astype(BF16)
        acc = acc + aff[:, e:e + 1] * _dot(onehot, ys_ref[e, pl.ds(w0, win), :])
        spans.append((c1, w0))
        overflow = jnp.logical_or(overflow, c1 > w0 + win)
    acc_ref[...] = acc

    @pl.when(overflow)
    def _():
        for e, (c1, w0) in enumerate(spans):
            def more(k, carry, e=e, w0=w0):
                ws = w0 + k * win
                wc = pl.multiple_of(jnp.minimum(ws, cap - win), SLOT_ALIGN)
                hit = jnp.logical_and(pos_ref[0][:, e:e + 1] - wc == lane, lane >= ws - wc)
                acc_ref[...] += aff_ref[0][:, e:e + 1] * _dot(jnp.where(hit, 1.0, 0.0).astype(BF16),
                                                              ys_ref[e, pl.ds(wc, win), :])
                return carry

            lax.fori_loop(1, (c1 - w0 + win - 1) // win, more, 0)

    x2 = x_ref[0] + m[:, 5 * d:6 * d] * acc_ref[...]
    if final:
        ms = jnp.mean(x2 * x2, axis=-1, keepdims=True)
        x2 = x2 * lax.rsqrt(ms + EPS) * fg_ref[...]
    o_ref[0] = x2


def _scatter_residual(x1, mod_l, ys, pos_n, aff_n, before, final_g, off, n, cap, mod_row_ctx, final):
    b, t, d = x1.shape
    ne = ys.shape[0]
    nb = mod_l.shape[0] - 1
    ot = off // TOK_TILE
    win = min(cap, SLOT_WINDOW)
    if final:
        out_shape = jax.ShapeDtypeStruct((b, n, d), F32)
        out_spec = pl.BlockSpec((1, TOK_TILE, d), lambda bi, i, bf: (bi, i, 0))
        aliases = {}
    else:
        out_shape = jax.ShapeDtypeStruct((b, t, d), F32)
        out_spec = pl.BlockSpec((1, TOK_TILE, d), lambda bi, i, bf: (bi, i + ot, 0))
        aliases = {1: 0}
    grid_spec = pltpu.PrefetchScalarGridSpec(
        num_scalar_prefetch=1,
        grid=(b, n // TOK_TILE),
        in_specs=[
            pl.BlockSpec((1, TOK_TILE, d), lambda bi, i, bf: (bi, i + ot, 0)),
            pl.BlockSpec((1, 1, N_MOD * d), lambda bi, i, bf: (nb if mod_row_ctx else bi, 0, 0)),
            pl.BlockSpec((ne, cap, d), lambda bi, i, bf: (0, bi, 0)),
            pl.BlockSpec((1, TOK_TILE, ne), lambda bi, i, bf: (bi, i, 0)),
            pl.BlockSpec((1, TOK_TILE, ne), lambda bi, i, bf: (bi, i + ot, 0)),
            pl.BlockSpec((1, d), lambda bi, i, bf: (0, 0)),
        ],
        out_specs=out_spec,
        scratch_shapes=[pltpu.VMEM((TOK_TILE, d), F32)],
    )
    return pl.pallas_call(
        functools.partial(_scatter_kernel, d=d, ne=ne, cap=cap, win=win, final=final),
        grid_spec=grid_spec,
        out_shape=out_shape,
        input_output_aliases=aliases,
        compiler_params=_params(("parallel", "parallel")),
        name="expert_scatter_residual",
    )(before.reshape(-1), x1, mod_l, ys, pos_n, aff_n, final_g)


def _rope_tables(ctx_len, seq, hd):
    rows = seq // GRID_W
    row = jnp.repeat(jnp.arange(rows), GRID_W).astype(F32)
    col = jnp.tile(jnp.arange(GRID_W), rows).astype(F32)
    inv = ROPE_THETA ** (-jnp.arange(0, hd // 2, 2, dtype=F32) / (hd // 2))
    ang = jnp.concatenate([row[:, None] * inv, col[:, None] * inv], axis=-1)
    cos = jnp.repeat(jnp.cos(ang), 2, axis=-1)
    sin = jnp.repeat(jnp.sin(ang), 2, axis=-1) * jnp.tile(jnp.array([-1.0, 1.0], F32), hd // 2)
    cos = jnp.concatenate([jnp.ones((ctx_len, hd), F32), cos], axis=0)
    sin = jnp.concatenate([jnp.zeros((ctx_len, hd), F32), sin], axis=0)
    rep = LANES // hd
    return jnp.tile(cos, (1, rep)), jnp.tile(sin, (1, rep))


def kernel(x, c, ctx, c_ctx, ada_w, ada_b, norm1_g, norm2_g, w_in, hg_lb_logits, hg_norm_g, conv_w, q_norm_g, k_norm_g, w_proj_a, w_proj_b, w_proj_c, w_out, router_w, w_gate, w_up, w_down, final_norm_g):
    b, s, d = x.shape
    ctx_len = ctx.shape[1]
    t = ctx_len + s
    depth = w_in.shape[0]
    hgw = hg_norm_g.shape[1]
    cw = conv_w.shape[2]
    aw = w_proj_c.shape[1]
    hd = q_norm_g.shape[1]
    kvw = (w_in.shape[2] - 5 * hgw - 3 * cw - aw - 3 * d) // 2
    ne = router_w.shape[2]
    nct = ctx_len // TOK_TILE
    assert ctx_len % TOK_TILE == 0 and s % TOK_TILE == 0 and LANES % hd == 0 and cw == d
    dims = dict(hgw=hgw, cw=cw, aw=aw, kvw=kvw, hd=hd, nct=nct, t=t)

    stream = (ctx, x, nct)
    n_rows = -(-(b + 1) // 8) * 8
    cvec = jnp.concatenate([c, c_ctx[None, :], jnp.zeros((n_rows - b - 1, d), F32)], axis=0)
    mod = _modulation(cvec, ada_w, ada_b)
    mod = mod[:, :b + 1].reshape(depth, b + 1, 1, N_MOD * d)

    cos_t, sin_t = _rope_tables(ctx_len, s, hd)
    lane = jnp.arange(LANES)
    gmat = (lane[:, None] // hd == lane[None, :] // hd).astype(BF16)
    tri = (lane[:, None] <= lane[None, :]).astype(BF16)
    rep = LANES // hd
    lb_logits = hg_lb_logits.astype(F32)

    out = None
    for l in range(depth):
        last = l == depth - 1
        mod_l = mod[l]
        (hq, zf, zb, hi, hog, cb, cu, aq, ak, av, ga, gb, gc) = _in_projection(
            stream, mod_l, norm1_g[l][None, :], w_in[l].astype(BF16), cos_t, sin_t,
            jnp.tile(q_norm_g[l], rep)[None, :], jnp.tile(k_norm_g[l], rep)[None, :], gmat, dims)
        a = _hgrn(hq, zf, zb, hi, hog, lb_logits, hg_norm_g[l][None, :], l, ctx_len)
        att = _attention(aq, ak, av, ctx_len)
        x1, h2, aff_t = _merge(stream, mod_l, a, cb, cu, att, ga, gb, gc, conv_w[l],
                               w_proj_a[l].astype(BF16), w_proj_b[l].astype(BF16), w_proj_c[l].astype(BF16),
                               w_out[l].astype(BF16), norm2_g[l][None, :], router_w[l].T, nct)
        aff_n = jnp.swapaxes(aff_t, 1, 2)

        def moe(xs_in, off, n, mod_row_ctx, final):
            cap = CAPACITY_FACTOR * n // ne
            pos, before = _topk_positions(aff_t, tri, off, n, cap)
            xs = _gather_tokens(h2, pos, before, off, n, cap)
            ys = _expert_ffn(xs, w_gate, w_up, w_down, l)
            return _scatter_residual(xs_in, mod_l, ys, jnp.swapaxes(pos, 1, 2), aff_n, before,
                                     final_norm_g[None, :], off, n, cap, mod_row_ctx, final)

        if last:
            out = moe(x1, ctx_len, s, False, True)
        else:
            xc = moe(x1, ctx_len, s, False, False)
            xc = moe(xc, 0, ctx_len, True, False)
            stream = (xc, xc, 0)
    return out
```

```python
import functools

import jax
import jax.numpy as jnp
from jax import lax
from jax.experimental import pallas as pl
from jax.experimental.pallas import tpu as pltpu

F32 = jnp.float32
BF16 = jnp.bfloat16

EPS = 1e-6
N_MOD = 6
HG_DK = 128
GRID_W = 64
ROPE_THETA = 10000.0
CAPACITY_FACTOR = 2

LANES = 128
TOK_TILE = 256
HG_CHUNK = 64
HG_SUB = 32
HG_HEADS_PER_STEP = 2
HG_UNROLL = 4
HALO = 16
ATTN_GROUPS_PER_STEP = 4
SLOT_WINDOW = 64
SLOT_ALIGN = 16
MERGE_SAMPLES = 2
LOG2E = 1.4426950408889634
VMEM_LIMIT = 56 * 1024 * 1024

HIGHEST = lax.Precision.HIGHEST


def _dot(a, b, precision=None):
    return jnp.dot(a, b, preferred_element_type=F32, precision=precision)


def _dot_nt(a, b, precision=None):
    return lax.dot_general(a, b, (((1,), (1,)), ((), ())), preferred_element_type=F32, precision=precision)


def _dot_tn(a, b):
    return lax.dot_general(a, b, (((0,), (0,)), ((), ())), preferred_element_type=F32)


def _sigmoid(x):
    return 1.0 / (1.0 + jnp.exp(-x))


def _params(sem, vmem=VMEM_LIMIT):
    return pltpu.CompilerParams(dimension_semantics=sem, vmem_limit_bytes=vmem)


def _mod_kernel(c_ref, w_ref, b_ref, o_ref):
    c = c_ref[...]
    sc = c * _sigmoid(c)
    o_ref[0] = _dot(sc, w_ref[0], precision=HIGHEST) + b_ref[0]


def _modulation(cvec, ada_w, ada_b):
    depth, d, n = ada_w.shape
    r = cvec.shape[0]
    tn = d
    return pl.pallas_call(
        _mod_kernel,
        grid=(depth, n // tn),
        in_specs=[
            pl.BlockSpec((r, d), lambda l, j: (0, 0)),
            pl.BlockSpec((1, d, tn), lambda l, j: (l, 0, j)),
            pl.BlockSpec((1, 1, tn), lambda l, j: (l, 0, j)),
        ],
        out_specs=pl.BlockSpec((1, r, tn), lambda l, j: (l, 0, j)),
        out_shape=jax.ShapeDtypeStruct((depth, r, n), F32),
        compiler_params=_params(("parallel", "parallel")),
        name="adaln_modulation",
    )(cvec, ada_w, ada_b.reshape(depth, 1, n))


def _modulated_norm(x, g, shift, scale):
    ms = jnp.mean(x * x, axis=-1, keepdims=True)
    return (x * lax.rsqrt(ms + EPS) * g) * (1.0 + scale) + shift


def _headnorm_rope(p, gain, gmat, cos, sin, hd, post_scale):
    rows, width = p.shape
    lane = lax.broadcasted_iota(jnp.int32, (rows, LANES), 1)
    even = (lane % 2) == 0
    outs = []
    for cb in range(width // LANES):
        xb = p[:, cb * LANES:(cb + 1) * LANES]
        ss = _dot((xb * xb).astype(BF16), gmat)
        y = xb * lax.rsqrt(ss * (1.0 / hd) + EPS) * gain
        y_next = pltpu.roll(y, LANES - 1, axis=1)
        y_prev = pltpu.roll(y, 1, axis=1)
        ysw = jnp.where(even, y_next, y_prev)
        outs.append((y * cos + ysw * sin) * post_scale)
    return jnp.concatenate(outs, axis=1) if len(outs) > 1 else outs[0]


def _lower_bounds(lg, layer):
    e = jnp.exp(lg - jnp.max(lg, axis=0, keepdims=True))
    sm = e / jnp.sum(e, axis=0, keepdims=True)
    lb = jnp.zeros(lg.shape[1:], F32)
    for j in range(1, layer + 1):
        lb = lb + sm[j]
    return lb


def _forget_gate(z, lbd, lb_is_zero):
    zs = z * LOG2E
    sp = jnp.log2(1.0 + jnp.exp2(-jnp.abs(zs)))
    ls = jnp.minimum(zs, 0.0) - sp
    sneg = jnp.exp2(jnp.minimum(-zs, 0.0) - sp)
    if lb_is_zero:
        return ls, sneg
    a = jnp.log2(lbd)
    t = jnp.log2(1.0 - lbd) + ls
    mx = jnp.maximum(a, t)
    mn = jnp.minimum(a, t)
    return mx + jnp.log2(1.0 + jnp.exp2(mn - mx)), (1.0 - lbd) * sneg


def _stream_specs(stream, nct, block_rows, ns=1):
    _, _, lat_off = stream
    d = stream[0].shape[2]
    return [pl.BlockSpec((ns, block_rows, d), lambda bi, i: (bi, jnp.minimum(i, nct - 1), 0)),
            pl.BlockSpec((ns, block_rows, d), lambda bi, i: (bi, jnp.maximum(i, nct) - lat_off, 0))]


def _inproj_kernel(xa_ref, xb_ref, mod_ref, g_ref, w_ref, cos_ref, sin_ref, qg_ref, kg_ref, gmat_ref,
                   hq_ref, zf_ref, zb_ref, hi_ref, hog_ref, cb_ref, cu_ref, aq_ref, ak_ref,
                   av_ref, ga_ref, gb_ref, gc_ref, *, d, hgw, cw, aw, kvw, hd, nct):
    x = jnp.where(pl.program_id(1) < nct, xa_ref[0], xb_ref[0])
    m = mod_ref[0]
    h = _modulated_norm(x, g_ref[...], m[:, 0:d], m[:, d:2 * d]).astype(BF16)

    def proj(lo, width):
        return _dot(h, w_ref[:, lo:lo + width])

    o_hq, o_zf, o_zb, o_hi, o_hog = (j * hgw for j in range(5))
    o_cb, o_cc, o_cx = (5 * hgw + j * cw for j in range(3))
    o_q = 5 * hgw + 3 * cw
    o_k, o_v = o_q + aw, o_q + aw + kvw
    o_ga, o_gb, o_gc = (o_v + kvw + j * d for j in range(3))
    cos = cos_ref[...]
    sin = sin_ref[...]
    gmat = gmat_ref[...]

    def gate(o, ref):
        ref[0] = _sigmoid(proj(o, d)).astype(BF16)

    hq_ref[0] = proj(o_hq, hgw).astype(BF16)
    zf_ref[0] = proj(o_zf, hgw)
    zb_ref[0] = proj(o_zb, hgw)
    hi_ref[0] = proj(o_hi, hgw).astype(BF16)
    g = proj(o_hog, hgw)
    hog_ref[0] = (g * _sigmoid(g)).astype(BF16)
    cb_ref[0] = proj(o_cb, cw).astype(BF16)
    cu_ref[0] = (proj(o_cc, cw) * proj(o_cx, cw)).astype(BF16)
    aq_ref[0] = _headnorm_rope(proj(o_q, aw), qg_ref[...], gmat, cos, sin, hd, hd ** -0.5 * LOG2E).astype(BF16)
    kn_t = _headnorm_rope(proj(o_k, kvw), kg_ref[...], gmat, cos, sin, hd, 1.0).T.astype(BF16)
    vv = proj(o_v, kvw).astype(BF16)
    ones_col = jnp.where(lax.broadcasted_iota(jnp.int32, (vv.shape[0], hd), 1) == 0, 1.0, 0.0).astype(BF16)
    for g in range(kvw // hd):
        ak_ref[0, g] = kn_t[g * hd:(g + 1) * hd, :]
        av_ref[0, g] = jnp.concatenate([vv[:, g * hd:(g + 1) * hd], ones_col], axis=1)
    gate(o_ga, ga_ref)
    gate(o_gb, gb_ref)
    gate(o_gc, gc_ref)


def _in_projection(stream, mod_l, norm_g, w_bf, cos_t, sin_t, qg, kg, gmat, dims):
    b, _, d = stream[0].shape
    t = dims["t"]
    hgw, cw, aw, kvw, hd, nct = dims["hgw"], dims["cw"], dims["aw"], dims["kvw"], dims["hd"], dims["nct"]
    nt = t // TOK_TILE
    in_w = w_bf.shape[1]
    nb = mod_l.shape[0] - 1

    def tok(width):
        return pl.BlockSpec((1, TOK_TILE, width), lambda bi, i: (bi, i, 0))

    def const(shape):
        return pl.BlockSpec(shape, lambda bi, i: (0,) * len(shape))

    widths = [(hgw, BF16), (hgw, F32), (hgw, F32), (hgw, BF16), (hgw, BF16),
              (cw, BF16), (cw, BF16), (aw, BF16), (kvw, BF16), (kvw, BF16), (d, BF16), (d, BF16), (d, BF16)]
    ng = kvw // hd
    out_specs = [tok(w) for w, _ in widths]
    out_shape = [jax.ShapeDtypeStruct((b, t, w), dt) for w, dt in widths]
    out_specs[8] = pl.BlockSpec((1, ng, hd, TOK_TILE), lambda bi, i: (bi, 0, 0, i))
    out_shape[8] = jax.ShapeDtypeStruct((b, ng, hd, t), BF16)
    out_specs[9] = pl.BlockSpec((1, ng, TOK_TILE, 2 * hd), lambda bi, i: (bi, 0, i, 0))
    out_shape[9] = jax.ShapeDtypeStruct((b, ng, t, 2 * hd), BF16)
    return pl.pallas_call(
        functools.partial(_inproj_kernel, d=d, hgw=hgw, cw=cw, aw=aw, kvw=kvw, hd=hd, nct=nct),
        grid=(b, nt),
        in_specs=_stream_specs(stream, nct, TOK_TILE) + [
            pl.BlockSpec((1, 1, N_MOD * d), lambda bi, i: (jnp.where(i < nct, nb, bi), 0, 0)),
            const((1, d)),
            pl.BlockSpec((d, in_w), lambda bi, i: (0, 0), pipeline_mode=pl.Buffered(1)),
            pl.BlockSpec((TOK_TILE, LANES), lambda bi, i: (i, 0)),
            pl.BlockSpec((TOK_TILE, LANES), lambda bi, i: (i, 0)),
            const((1, LANES)),
            const((1, LANES)),
            const((LANES, LANES)),
        ],
        out_specs=out_specs,
        out_shape=out_shape,
        compiler_params=_params(("parallel", "parallel")),
        name="in_projection",
    )(stream[0], stream[1], mod_l, norm_g, w_bf, cos_t, sin_t, qg, kg, gmat)


def _chunk_cumsum(x, reverse):
    n = x.shape[0]
    pos = lax.broadcasted_iota(jnp.int32, x.shape, 0) % HG_CHUNK
    s = 1
    while s < HG_CHUNK:
        if reverse:
            x = x + jnp.where(pos < HG_CHUNK - s, pltpu.roll(x, n - s, axis=0), 0.0)
        else:
            x = x + jnp.where(pos >= s, pltpu.roll(x, s, axis=0), 0.0)
        s *= 2
    return x


def _hgrn_chunk(q, z, lbd, lb_is_zero, v, st_ref, reverse):
    c = q.shape[0]
    lf, kk = _forget_gate(z, lbd, lb_is_zero)
    cs = _chunk_cumsum(lf, reverse)
    nsb = c // HG_SUB
    anchors = []
    for i in range(nsb):
        r = i * HG_SUB + (HG_SUB // 2 if reverse else HG_SUB // 2 - 1)
        anchors.append(cs[r:r + 1, :])
    c_anchor = jnp.concatenate([jnp.broadcast_to(a, (HG_SUB, a.shape[1])) for a in anchors], axis=0)
    c_end = cs[0:1, :] if reverse else cs[c - 1:c, :]
    qh = q * jnp.exp2(cs - c_anchor)

    def seg_rows(i, rows_of):
        return jnp.concatenate([rows_of(j) if keep_j else jnp.zeros((HG_SUB, q.shape[1]), F32)
                                for j, keep_j in enumerate(i)], axis=0)

    q_ext = jnp.concatenate(
        [seg_rows([j == i for j in range(nsb)], lambda j: qh[j * HG_SUB:(j + 1) * HG_SUB]) for i in range(nsb)],
        axis=1)
    k_ext = jnp.concatenate(
        [seg_rows([(j >= i) if reverse else (j <= i) for j in range(nsb)],
                  lambda j, i=i: kk[j * HG_SUB:(j + 1) * HG_SUB]
                  * jnp.exp2(anchors[i] - cs[j * HG_SUB:(j + 1) * HG_SUB])) for i in range(nsb)],
        axis=1)
    a = _dot_nt(q_ext.astype(BF16), k_ext.astype(BF16))
    row = lax.broadcasted_iota(jnp.int32, (c, c), 0)
    col = lax.broadcasted_iota(jnp.int32, (c, c), 1)
    keep = (col >= row) if reverse else (col <= row)
    st = st_ref[...]
    o = (_dot(jnp.where(keep, a, 0.0).astype(BF16), v)
         + _dot_nt((qh * jnp.exp2(c_anchor)).astype(BF16), st.astype(BF16)))
    kh = (kk * jnp.exp2(c_end - cs)).astype(BF16)
    st_ref[...] = st * jnp.exp2(c_end) + _dot_tn(v, kh)
    return o


def _hgrn_kernel(lbl_ref, hq_ref, zf_ref, zb_ref, hi_ref, hog_ref, ng_ref, a_ref,
                 of_ref, ob_ref, stf_ref, stb_ref, *, layer, n_chunks, n_ctx_chunks, n_heads):
    stf_ref[...] = jnp.zeros_like(stf_ref)
    stb_ref[...] = jnp.zeros_like(stb_ref)
    c = HG_CHUNK
    lb = _lower_bounds(lbl_ref[...], layer)

    def step(s, carry):
        rf = pl.multiple_of(s * c, c)
        cbk = jnp.where(s < n_ctx_chunks, n_ctx_chunks - 1 - s, n_chunks - 1 - (s - n_ctx_chunks))
        rb = pl.multiple_of(cbk * c, c)
        for h in range(n_heads):
            ln = slice(h * HG_DK, (h + 1) * HG_DK)
            of_ref[pl.ds(rf, c), ln] = _hgrn_chunk(
                hq_ref[0, pl.ds(rf, c), ln].astype(F32), zf_ref[0, pl.ds(rf, c), ln], lb[0:1, ln], layer == 0,
                hi_ref[0, pl.ds(rf, c), ln], stf_ref.at[h], False)
            ob_ref[pl.ds(rb, c), ln] = _hgrn_chunk(
                hq_ref[0, pl.ds(rb, c), ln].astype(F32), zb_ref[0, pl.ds(rb, c), ln], lb[1:2, ln], layer == 0,
                hi_ref[0, pl.ds(rb, c), ln], stb_ref.at[h], True)
        return carry

    lax.fori_loop(0, n_chunks, step, 0, unroll=HG_UNROLL)

    def readout(i, carry):
        r0 = pl.multiple_of(i * TOK_TILE, TOK_TILE)
        for h in range(n_heads):
            ln = slice(h * HG_DK, (h + 1) * HG_DK)
            o = of_ref[pl.ds(r0, TOK_TILE), ln] + ob_ref[pl.ds(r0, TOK_TILE), ln]
            ms = jnp.mean(o * o, axis=-1, keepdims=True)
            y = o * lax.rsqrt(ms + EPS) * ng_ref[:, ln]
            a_ref[0, pl.ds(r0, TOK_TILE), ln] = (y * hog_ref[0, pl.ds(r0, TOK_TILE), ln].astype(F32)).astype(BF16)
        return carry

    lax.fori_loop(0, (n_chunks * c) // TOK_TILE, readout, 0)


def _hgrn(hq, zf, zb, hi, hog, lb_logits, norm_g, layer, ctx_len):
    b, t, hgw = hq.shape
    nh = HG_HEADS_PER_STEP
    wid = nh * HG_DK
    depth = lb_logits.shape[0]

    def seq():
        return pl.BlockSpec((1, t, wid), lambda bi, h: (bi, 0, h))

    return pl.pallas_call(
        functools.partial(_hgrn_kernel, layer=layer, n_chunks=t // HG_CHUNK, n_ctx_chunks=ctx_len // HG_CHUNK,
                          n_heads=nh),
        grid=(b, hgw // wid),
        in_specs=[pl.BlockSpec((depth, 2, wid), lambda bi, h: (0, 0, h)),
                  seq(), seq(), seq(), seq(), seq(),
                  pl.BlockSpec((1, wid), lambda bi, h: (0, h))],
        out_specs=seq(),
        out_shape=jax.ShapeDtypeStruct((b, t, hgw), BF16),
        scratch_shapes=[pltpu.VMEM((t, wid), F32), pltpu.VMEM((t, wid), F32),
                        pltpu.VMEM((nh, HG_DK, HG_DK), F32), pltpu.VMEM((nh, HG_DK, HG_DK), F32)],
        compiler_params=_params(("parallel", "parallel")),
        name="hgrn2_bidirectional",
    )(lb_logits, hq, zf, zb, hi, hog, norm_g)


def _attn_kernel(q_ref, k_ref, v_ref, o_ref, *, n_g, group, hd, nct, ctx_len, t_all):
    i = pl.program_id(2)

    def run(n_keys):
        outs = []
        for g in range(n_g):
            kt = k_ref[0, g, :, 0:n_keys]
            v = v_ref[0, g, 0:n_keys, :]
            for h in range(group):
                c0 = (g * group + h) * hd
                s = _dot(q_ref[0, :, c0:c0 + hd], kt)
                p = jnp.exp2(s - jnp.max(s, axis=1, keepdims=True))
                ov = _dot(p.astype(BF16), v)
                outs.append((ov[:, 0:hd] / ov[:, hd:hd + 1]).astype(BF16))
        o_ref[0] = jnp.concatenate(outs, axis=1)

    @pl.when(i < nct)
    def _():
        run(ctx_len)

    @pl.when(i >= nct)
    def _():
        run(t_all)


def _attention(aq, ak, av, ctx_len):
    b, t, aw = aq.shape
    n_groups, hd = ak.shape[1], ak.shape[2]
    group = aw // (n_groups * hd)
    nt = t // TOK_TILE
    n_g = ATTN_GROUPS_PER_STEP if n_groups % ATTN_GROUPS_PER_STEP == 0 else 1
    return pl.pallas_call(
        functools.partial(_attn_kernel, n_g=n_g, group=group, hd=hd, nct=ctx_len // TOK_TILE, ctx_len=ctx_len,
                          t_all=t),
        grid=(b, n_groups // n_g, nt),
        in_specs=[
            pl.BlockSpec((1, TOK_TILE, n_g * group * hd), lambda bi, g, i: (bi, i, g)),
            pl.BlockSpec((1, n_g, hd, t), lambda bi, g, i: (bi, g, 0, 0)),
            pl.BlockSpec((1, n_g, t, 2 * hd), lambda bi, g, i: (bi, g, 0, 0)),
        ],
        out_specs=pl.BlockSpec((1, TOK_TILE, n_g * group * hd), lambda bi, g, i: (bi, i, g)),
        out_shape=jax.ShapeDtypeStruct((b, t, aw), BF16),
        compiler_params=_params(("parallel", "parallel", "parallel")),
        name="gqa_attention",
    )(aq, ak, av)


def _merge_kernel(xa_ref, xb_ref, mod_ref, a_ref, cb_ref, cu_ref, cup_ref, cun_ref, att_ref, ga_ref, gb_ref, gc_ref,
                  cw_ref, wa_ref, wb_ref, wc_ref, wo_ref, g2_ref, rwt_ref,
                  x1_ref, h2_ref, aff_ref, *, d, nct, nt):
    i = pl.program_id(1)
    ns, rows = xa_ref.shape[0], xa_ref.shape[1]
    n = ns * rows

    def stacked(ref):
        return ref[...].reshape(n, ref.shape[2])

    row = lax.broadcasted_iota(jnp.int32, (n, 1), 0)

    def per_sample(lo):
        out = mod_ref[ns - 1][:, lo:lo + d]
        for s in range(ns - 2, -1, -1):
            out = jnp.where(row < (s + 1) * rows, mod_ref[s][:, lo:lo + d], out)
        return out

    u = stacked(cu_ref).astype(F32)
    has_prev = jnp.logical_and(i != 0, i != nct)
    has_next = jnp.logical_and(i != nct - 1, i != nt - 1)
    u_prev = pltpu.roll(u, 1, axis=0)
    u_next = pltpu.roll(u, n - 1, axis=0)
    for s in range(ns):
        prev_row = jnp.where(has_prev, cup_ref[s, HALO - 1:HALO, :].astype(F32), 0.0)
        next_row = jnp.where(has_next, cun_ref[s, 0:1, :].astype(F32), 0.0)
        u_prev = jnp.where(row == s * rows, prev_row, u_prev)
        u_next = jnp.where(row == (s + 1) * rows - 1, next_row, u_next)
    cw = cw_ref[...]
    conv = cw[0:1, :] * u_prev + cw[1:2, :] * u + cw[2:3, :] * u_next
    bb = (stacked(cb_ref).astype(F32) * conv).astype(BF16)
    y = (stacked(ga_ref).astype(F32) * _dot(stacked(a_ref), wa_ref[...])
         + stacked(gb_ref).astype(F32) * _dot(bb, wb_ref[...])
         + stacked(gc_ref).astype(F32) * _dot(stacked(att_ref), wc_ref[...]))
    x = jnp.where(i < nct, stacked(xa_ref), stacked(xb_ref))
    x1 = x + per_sample(2 * d) * _dot(y.astype(BF16), wo_ref[...])
    x1_ref[...] = x1.reshape(ns, rows, d)
    h2 = _modulated_norm(x1, g2_ref[...], per_sample(3 * d), per_sample(4 * d))
    h2_ref[...] = h2.astype(BF16).reshape(ns, rows, d)
    logits = _dot_nt(rwt_ref[...], h2, precision=HIGHEST)
    ex = jnp.exp(logits - jnp.max(logits, axis=0, keepdims=True))
    aff = ex / jnp.sum(ex, axis=0, keepdims=True)
    for s in range(ns):
        aff_ref[s] = aff[:, s * rows:(s + 1) * rows]


def _merge(stream, mod_l, a, cb, cu, att, ga, gb, gc, conv_w, wa, wb, wc, wo, norm2_g, rwt, nct):
    b, t, d = a.shape[0], a.shape[1], stream[0].shape[2]
    nt = t // TOK_TILE
    ne = rwt.shape[0]
    ns = MERGE_SAMPLES if b % MERGE_SAMPLES == 0 else 1
    mod_m = jnp.concatenate([mod_l[:b]] + [mod_l[b:b + 1]] * ns, axis=0)
    ctx_blk = b // ns
    sub = TOK_TILE // HALO
    n8 = t // HALO

    def tok(width):
        return pl.BlockSpec((ns, TOK_TILE, width), lambda bi, i: (bi, i, 0))

    def const(shape):
        return pl.BlockSpec(shape, lambda bi, i: (0,) * len(shape))

    cwid = cu.shape[2]
    return pl.pallas_call(
        functools.partial(_merge_kernel, d=d, nct=nct, nt=nt),
        grid=(b // ns, nt),
        in_specs=_stream_specs(stream, nct, TOK_TILE, ns) + [
            pl.BlockSpec((ns, 1, N_MOD * d), lambda bi, i: (jnp.where(i < nct, ctx_blk, bi), 0, 0)),
            tok(a.shape[2]), tok(cwid), tok(cwid),
            pl.BlockSpec((ns, HALO, cwid), lambda bi, i: (bi, jnp.maximum(i * sub - 1, 0), 0)),
            pl.BlockSpec((ns, HALO, cwid), lambda bi, i: (bi, jnp.minimum((i + 1) * sub, n8 - 1), 0)),
            tok(att.shape[2]), tok(d), tok(d), tok(d),
            const(conv_w.shape), const(wa.shape), const(wb.shape), const(wc.shape), const(wo.shape),
            const((1, d)), const(rwt.shape),
        ],
        out_specs=[tok(d), tok(d), pl.BlockSpec((ns, ne, TOK_TILE), lambda bi, i: (bi, 0, i))],
        out_shape=[jax.ShapeDtypeStruct((b, t, d), F32), jax.ShapeDtypeStruct((b, t, d), BF16),
                   jax.ShapeDtypeStruct((b, ne, t), F32)],
        compiler_params=_params(("parallel", "parallel")),
        name="merge_residual_router",
    )(stream[0], stream[1], mod_m, a, cb, cu, cu, cu, att, ga, gb, gc, conv_w, wa, wb, wc, wo, norm2_g, rwt)


def _prefix_count(mask, tri):
    e, n = mask.shape
    carry = jnp.zeros((e, 1), F32)
    outs = []
    for blk in range(n // LANES):
        xb = jnp.where(mask[:, blk * LANES:(blk + 1) * LANES], 1.0, 0.0).astype(BF16)
        pre = _dot(xb, tri) + carry
        carry = pre[:, LANES - 1:LANES]
        outs.append(pre)
    return jnp.concatenate(outs, axis=1) if len(outs) > 1 else outs[0]


def _topk_kernel(aff_ref, tri_ref, tind_ref, sut_ref, pos_ref, before_ref, *, off, n, cap):
    a = aff_ref[0][:, off:off + n]
    bits = pltpu.bitcast(a, jnp.int32)
    thr = jnp.zeros((a.shape[0], 1), jnp.int32)
    for bit in range(30, -1, -1):
        cand = thr | (1 << bit)
        cnt = jnp.sum(jnp.where(bits >= cand, 1.0, 0.0), axis=1, keepdims=True)
        thr = jnp.where(cnt >= cap, cand, thr)
    gt = bits > thr
    eq = bits == thr
    need = cap - jnp.sum(jnp.where(gt, 1.0, 0.0), axis=1, keepdims=True)
    tri = tri_ref[...]
    eq_rank = _prefix_count(eq, tri)
    sel = jnp.logical_or(gt, jnp.logical_and(eq, eq_rank <= need))
    slot = _prefix_count(sel, tri) - 1.0
    pos_ref[0] = jnp.where(sel, slot, -1.0).astype(jnp.int32)
    tile_cnt = _dot(jnp.where(sel, 1.0, 0.0).astype(BF16), tind_ref[...])
    before_ref[0] = _dot(tile_cnt.astype(BF16), sut_ref[...]).astype(jnp.int32)


def _topk_positions(aff_t, tri, off, n, cap):
    b, ne, t = aff_t.shape
    assert n // TOK_TILE < LANES and TOK_TILE <= 256
    lane = jnp.arange(LANES)
    tind = (jnp.arange(n)[:, None] // TOK_TILE == lane[None, :]).astype(BF16)
    sut = (lane[:, None] < lane[None, :]).astype(BF16)

    def const(shape):
        return pl.BlockSpec(shape, lambda bi: (0,) * len(shape))

    return pl.pallas_call(
        functools.partial(_topk_kernel, off=off, n=n, cap=cap),
        grid=(b,),
        in_specs=[pl.BlockSpec((1, ne, t), lambda bi: (bi, 0, 0)), const((LANES, LANES)), const(tind.shape),
                  const(sut.shape)],
        out_specs=[pl.BlockSpec((1, ne, n), lambda bi: (bi, 0, 0)), pl.BlockSpec((1, ne, LANES), lambda bi: (bi, 0, 0))],
        out_shape=[jax.ShapeDtypeStruct((b, ne, n), jnp.int32), jax.ShapeDtypeStruct((b, ne, LANES), jnp.int32)],
        compiler_params=_params(("parallel",)),
        name="expert_choice_topk",
    )(aff_t, tri, tind, sut)


def _gather_kernel(before_ref, h_ref, pos_ref, xs_ref, *, ne, cap, win):
    bi = pl.program_id(0)
    tt = pl.program_id(1)

    @pl.when(tt == 0)
    def _():
        xs_ref[...] = jnp.zeros_like(xs_ref)

    h = h_ref[0]
    pos = pos_ref[0]
    rows = h.shape[0]
    sub = lax.broadcasted_iota(jnp.int32, (win, rows), 0)
    hots, spans = [], []
    overflow = False
    for e in range(ne):
        base = (bi * ne + e) * LANES
        c0 = before_ref[base + tt]
        c1 = before_ref[base + tt + 1]
        w0 = pl.multiple_of(jnp.minimum((c0 // SLOT_ALIGN) * SLOT_ALIGN, cap - win), SLOT_ALIGN)
        hots.append(jnp.where(pos[e:e + 1, :] - w0 == sub, 1.0, 0.0).astype(BF16))
        spans.append((c1, w0))
        overflow = jnp.logical_or(overflow, c1 > w0 + win)
    picked = _dot(jnp.concatenate(hots, axis=0), h).astype(BF16)
    for e, (c1, w0) in enumerate(spans):
        xs_ref[e, pl.ds(w0, win), :] += picked[e * win:(e + 1) * win, :]

    @pl.when(overflow)
    def _():
        for e, (c1, w0) in enumerate(spans):
            def more(k, carry, e=e, w0=w0):
                ws = w0 + k * win
                wc = pl.multiple_of(jnp.minimum(ws, cap - win), SLOT_ALIGN)
                hit = jnp.logical_and(pos_ref[0][e:e + 1, :] - wc == sub, sub >= ws - wc)
                xs_ref[e, pl.ds(wc, win), :] += _dot(jnp.where(hit, 1.0, 0.0).astype(BF16), h_ref[0]).astype(BF16)
                return carry

            lax.fori_loop(1, (c1 - w0 + win - 1) // win, more, 0)


def _gather_tokens(h2, pos, before, off, n, cap):
    b, t, d = h2.shape
    ne = pos.shape[1]
    ot = off // TOK_TILE
    win = min(cap, SLOT_WINDOW)
    grid_spec = pltpu.PrefetchScalarGridSpec(
        num_scalar_prefetch=1,
        grid=(b, n // TOK_TILE),
        in_specs=[pl.BlockSpec((1, TOK_TILE, d), lambda bi, tt, bf: (bi, tt + ot, 0)),
                  pl.BlockSpec((1, ne, TOK_TILE), lambda bi, tt, bf: (bi, 0, tt))],
        out_specs=pl.BlockSpec((ne, cap, d), lambda bi, tt, bf: (0, bi, 0)),
    )
    return pl.pallas_call(
        functools.partial(_gather_kernel, ne=ne, cap=cap, win=win),
        grid_spec=grid_spec,
        out_shape=jax.ShapeDtypeStruct((ne, b * cap, d), BF16),
        compiler_params=_params(("parallel", "arbitrary")),
        name="expert_gather",
    )(before.reshape(-1), h2, pos)


def _ffn_kernel(xs_ref, wg_ref, wu_ref, wd_ref, ys_ref, wgb_ref, wub_ref, wdb_ref):
    @pl.when(pl.program_id(1) == 0)
    def _():
        wgb_ref[...] = wg_ref[0, 0].astype(BF16)
        wub_ref[...] = wu_ref[0, 0].astype(BF16)
        wdb_ref[...] = wd_ref[0, 0].astype(BF16)

    x = xs_ref[0]
    g = _dot(x, wgb_ref[...])
    hid = (g * _sigmoid(g)) * _dot(x, wub_ref[...])
    ys_ref[0] = _dot(hid.astype(BF16), wdb_ref[...]).astype(BF16)


def _expert_ffn(xs, wg, wu, wd, layer):
    ne, m, d = xs.shape
    ff = wg.shape[3]
    tm = min(m, 512)
    return pl.pallas_call(
        _ffn_kernel,
        grid=(ne, m // tm),
        in_specs=[pl.BlockSpec((1, tm, d), lambda e, j: (e, j, 0)),
                  pl.BlockSpec((1, 1, d, ff), lambda e, j: (layer, e, 0, 0)),
                  pl.BlockSpec((1, 1, d, ff), lambda e, j: (layer, e, 0, 0)),
                  pl.BlockSpec((1, 1, ff, d), lambda e, j: (layer, e, 0, 0))],
        out_specs=pl.BlockSpec((1, tm, d), lambda e, j: (e, j, 0)),
        out_shape=jax.ShapeDtypeStruct((ne, m, d), BF16),
        scratch_shapes=[pltpu.VMEM((d, ff), BF16), pltpu.VMEM((d, ff), BF16), pltpu.VMEM((ff, d), BF16)],
        compiler_params=_params(("parallel", "arbitrary")),
        name="expert_ffn",
    )(xs, wg, wu, wd)


def _scatter_kernel(before_ref, x_ref, mod_ref, ys_ref, pos_ref, aff_ref, fg_ref, o_ref, acc_ref, *,
                    d, ne, cap, win, final):
    bi = pl.program_id(0)
    i = pl.program_id(1)
    m = mod_ref[0]
    pos = pos_ref[0]
    aff = aff_ref[0]
    rows = pos.shape[0]
    lane = lax.broadcasted_iota(jnp.int32, (rows, win), 1)
    acc = jnp.zeros((rows, d), F32)
    spans = []
    overflow = False
    for e in range(ne):
        base = (bi * ne + e) * LANES
        c0 = before_ref[base + i]
        c1 = before_ref[base + i + 1]
        w0 = pl.multiple_of(jnp.minimum((c0 // SLOT_ALIGN) * SLOT_ALIGN, cap - win), SLOT_ALIGN)
        onehot = jnp.where(pos[:, e:e + 1] - w0 == lane, 1.0, 0.0).astype(BF16)
        acc = acc + aff[:, e:e + 1] * _dot(onehot, ys_ref[e, pl.ds(w0, win), :])
        spans.append((c1, w0))
        overflow = jnp.logical_or(overflow, c1 > w0 + win)
    acc_ref[...] = acc

    @pl.when(overflow)
    def _():
        for e, (c1, w0) in enumerate(spans):
            def more(k, carry, e=e, w0=w0):
                ws = w0 + k * win
                wc = pl.multiple_of(jnp.minimum(ws, cap - win), SLOT_ALIGN)
                hit = jnp.logical_and(pos_ref[0][:, e:e + 1] - wc == lane, lane >= ws - wc)
                acc_ref[...] += aff_ref[0][:, e:e + 1] * _dot(jnp.where(hit, 1.0, 0.0).astype(BF16),
                                                              ys_ref[e, pl.ds(wc, win), :])
                return carry

            lax.fori_loop(1, (c1 - w0 + win - 1) // win, more, 0)

    x2 = x_ref[0] + m[:, 5 * d:6 * d] * acc_ref[...]
    if final:
        ms = jnp.mean(x2 * x2, axis=-1, keepdims=True)
        x2 = x2 * lax.rsqrt(ms + EPS) * fg_ref[...]
    o_ref[0] = x2


def _scatter_residual(x1, mod_l, ys, pos_n, aff_n, before, final_g, off, n, cap, mod_row_ctx, final):
    b, t, d = x1.shape
    ne = ys.shape[0]
    nb = mod_l.shape[0] - 1
    ot = off // TOK_TILE
    win = min(cap, SLOT_WINDOW)
    if final:
        out_shape = jax.ShapeDtypeStruct((b, n, d), F32)
        out_spec = pl.BlockSpec((1, TOK_TILE, d), lambda bi, i, bf: (bi, i, 0))
        aliases = {}
    else:
        out_shape = jax.ShapeDtypeStruct((b, t, d), F32)
        out_spec = pl.BlockSpec((1, TOK_TILE, d), lambda bi, i, bf: (bi, i + ot, 0))
        aliases = {1: 0}
    grid_spec = pltpu.PrefetchScalarGridSpec(
        num_scalar_prefetch=1,
        grid=(b, n // TOK_TILE),
        in_specs=[
            pl.BlockSpec((1, TOK_TILE, d), lambda bi, i, bf: (bi, i + ot, 0)),
            pl.BlockSpec((1, 1, N_MOD * d), lambda bi, i, bf: (nb if mod_row_ctx else bi, 0, 0)),
            pl.BlockSpec((ne, cap, d), lambda bi, i, bf: (0, bi, 0)),
            pl.BlockSpec((1, TOK_TILE, ne), lambda bi, i, bf: (bi, i, 0)),
            pl.BlockSpec((1, TOK_TILE, ne), lambda bi, i, bf: (bi, i + ot, 0)),
            pl.BlockSpec((1, d), lambda bi, i, bf: (0, 0)),
        ],
        out_specs=out_spec,
        scratch_shapes=[pltpu.VMEM((TOK_TILE, d), F32)],
    )
    return pl.pallas_call(
        functools.partial(_scatter_kernel, d=d, ne=ne, cap=cap, win=win, final=final),
        grid_spec=grid_spec,
        out_shape=out_shape,
        input_output_aliases=aliases,
        compiler_params=_params(("parallel", "parallel")),
        name="expert_scatter_residual",
    )(before.reshape(-1), x1, mod_l, ys, pos_n, aff_n, final_g)


def _rope_tables(ctx_len, seq, hd):
    rows = seq // GRID_W
    row = jnp.repeat(jnp.arange(rows), GRID_W).astype(F32)
    col = jnp.tile(jnp.arange(GRID_W), rows).astype(F32)
    inv = ROPE_THETA ** (-jnp.arange(0, hd // 2, 2, dtype=F32) / (hd // 2))
    ang = jnp.concatenate([row[:, None] * inv, col[:, None] * inv], axis=-1)
    cos = jnp.repeat(jnp.cos(ang), 2, axis=-1)
    sin = jnp.repeat(jnp.sin(ang), 2, axis=-1) * jnp.tile(jnp.array([-1.0, 1.0], F32), hd // 2)
    cos = jnp.concatenate([jnp.ones((ctx_len, hd), F32), cos], axis=0)
    sin = jnp.concatenate([jnp.zeros((ctx_len, hd), F32), sin], axis=0)
    rep = LANES // hd
    return jnp.tile(cos, (1, rep)), jnp.tile(sin, (1, rep))


def kernel(x, c, ctx, c_ctx, ada_w, ada_b, norm1_g, norm2_g, w_in, hg_lb_logits, hg_norm_g, conv_w, q_norm_g, k_norm_g, w_proj_a, w_proj_b, w_proj_c, w_out, router_w, w_gate, w_up, w_down, final_norm_g):
    b, s, d = x.shape
    ctx_len = ctx.shape[1]
    t = ctx_len + s
    depth = w_in.shape[0]
    hgw = hg_norm_g.shape[1]
    cw = conv_w.shape[2]
    aw = w_proj_c.shape[1]
    hd = q_norm_g.shape[1]
    kvw = (w_in.shape[2] - 5 * hgw - 3 * cw - aw - 3 * d) // 2
    ne = router_w.shape[2]
    nct = ctx_len // TOK_TILE
    assert ctx_len % TOK_TILE == 0 and s % TOK_TILE == 0 and LANES % hd == 0 and cw == d
    dims = dict(hgw=hgw, cw=cw, aw=aw, kvw=kvw, hd=hd, nct=nct, t=t)

    stream = (ctx, x, nct)
    n_rows = -(-(b + 1) // 8) * 8
    cvec = jnp.concatenate([c, c_ctx[None, :], jnp.zeros((n_rows - b - 1, d), F32)], axis=0)
    mod = _modulation(cvec, ada_w, ada_b)
    mod = mod[:, :b + 1].reshape(depth, b + 1, 1, N_MOD * d)

    cos_t, sin_t = _rope_tables(ctx_len, s, hd)
    lane = jnp.arange(LANES)
    gmat = (lane[:, None] // hd == lane[None, :] // hd).astype(BF16)
    tri = (lane[:, None] <= lane[None, :]).astype(BF16)
    rep = LANES // hd
    lb_logits = hg_lb_logits.astype(F32)

    out = None
    for l in range(depth):
        last = l == depth - 1
        mod_l = mod[l]
        (hq, zf, zb, hi, hog, cb, cu, aq, ak, av, ga, gb, gc) = _in_projection(
            stream, mod_l, norm1_g[l][None, :], w_in[l].astype(BF16), cos_t, sin_t,
            jnp.tile(q_norm_g[l], rep)[None, :], jnp.tile(k_norm_g[l], rep)[None, :], gmat, dims)
        a = _hgrn(hq, zf, zb, hi, hog, lb_logits, hg_norm_g[l][None, :], l, ctx_len)
        att = _attention(aq, ak, av, ctx_len)
        x1, h2, aff_t = _merge(stream, mod_l, a, cb, cu, att, ga, gb, gc, conv_w[l],
                               w_proj_a[l].astype(BF16), w_proj_b[l].astype(BF16), w_proj_c[l].astype(BF16),
                               w_out[l].astype(BF16), norm2_g[l][None, :], router_w[l].T, nct)
        aff_n = jnp.swapaxes(aff_t, 1, 2)

        def moe(xs_in, off, n, mod_row_ctx, final):
            cap = CAPACITY_FACTOR * n // ne
            pos, before = _topk_positions(aff_t, tri, off, n, cap)
            xs = _gather_tokens(h2, pos, before, off, n, cap)
            ys = _expert_ffn(xs, w_gate, w_up, w_down, l)
            return _scatter_residual(xs_in, mod_l, ys, jnp.swapaxes(pos, 1, 2), aff_n, before,
                                     final_norm_g[None, :], off, n, cap, mod_row_ctx, final)

        if last:
            out = moe(x1, ctx_len, s, False, True)
        else:
            xc = moe(x1, ctx_len, s, False, False)
            xc = moe(xc, 0, ctx_len, True, False)
            stream = (xc, xc, 0)
    return out
```

```python
import functools

import jax
import jax.numpy as jnp
from jax import lax
from jax.experimental import pallas as pl
from jax.experimental.pallas import tpu as pltpu

F32 = jnp.float32
BF16 = jnp.bfloat16

EPS = 1e-6
N_MOD = 6
HG_DK = 128
GRID_W = 64
ROPE_THETA = 10000.0
CAPACITY_FACTOR = 2

LANES = 128
TOK_TILE = 256
HG_CHUNK = 64
HG_SUB = 32
HG_HEADS_PER_STEP = 2
HG_UNROLL = 4
HALO = 16
ATTN_GROUPS_PER_STEP = 4
SLOT_WINDOW = 64
SLOT_ALIGN = 16
FFN_SPLIT = 2
MERGE_SAMPLES = 2
LOG2E = 1.4426950408889634
VMEM_LIMIT = 56 * 1024 * 1024

HIGHEST = lax.Precision.HIGHEST


def _dot(a, b, precision=None):
    return jnp.dot(a, b, preferred_element_type=F32, precision=precision)


def _dot_nt(a, b, precision=None):
    return lax.dot_general(a, b, (((1,), (1,)), ((), ())), preferred_element_type=F32, precision=precision)


def _dot_tn(a, b):
    return lax.dot_general(a, b, (((0,), (0,)), ((), ())), preferred_element_type=F32)


def _sigmoid(x):
    return 1.0 / (1.0 + jnp.exp(-x))


def _params(sem, vmem=VMEM_LIMIT):
    return pltpu.CompilerParams(dimension_semantics=sem, vmem_limit_bytes=vmem)


def _mod_kernel(c_ref, w_ref, b_ref, o_ref):
    c = c_ref[...]
    sc = c * _sigmoid(c)
    o_ref[0] = _dot(sc, w_ref[0], precision=HIGHEST) + b_ref[0]


def _modulation(cvec, ada_w, ada_b):
    depth, d, n = ada_w.shape
    r = cvec.shape[0]
    tn = d
    return pl.pallas_call(
        _mod_kernel,
        grid=(depth, n // tn),
        in_specs=[
            pl.BlockSpec((r, d), lambda l, j: (0, 0)),
            pl.BlockSpec((1, d, tn), lambda l, j: (l, 0, j)),
            pl.BlockSpec((1, 1, tn), lambda l, j: (l, 0, j)),
        ],
        out_specs=pl.BlockSpec((1, r, tn), lambda l, j: (l, 0, j)),
        out_shape=jax.ShapeDtypeStruct((depth, r, n), F32),
        compiler_params=_params(("parallel", "parallel")),
        name="adaln_modulation",
    )(cvec, ada_w, ada_b.reshape(depth, 1, n))


def _modulated_norm(x, g, shift, scale):
    ms = jnp.mean(x * x, axis=-1, keepdims=True)
    return (x * lax.rsqrt(ms + EPS) * g) * (1.0 + scale) + shift


def _headnorm_rope(p, gain, gmat, cos, sin, hd, post_scale):
    rows, width = p.shape
    lane = lax.broadcasted_iota(jnp.int32, (rows, LANES), 1)
    even = (lane % 2) == 0
    outs = []
    for cb in range(width // LANES):
        xb = p[:, cb * LANES:(cb + 1) * LANES]
        ss = _dot((xb * xb).astype(BF16), gmat)
        y = xb * lax.rsqrt(ss * (1.0 / hd) + EPS) * gain
        y_next = pltpu.roll(y, LANES - 1, axis=1)
        y_prev = pltpu.roll(y, 1, axis=1)
        ysw = jnp.where(even, y_next, y_prev)
        outs.append((y * cos + ysw * sin) * post_scale)
    return jnp.concatenate(outs, axis=1) if len(outs) > 1 else outs[0]


def _lower_bounds(lg, layer):
    e = jnp.exp(lg - jnp.max(lg, axis=0, keepdims=True))
    sm = e / jnp.sum(e, axis=0, keepdims=True)
    lb = jnp.zeros(lg.shape[1:], F32)
    for j in range(1, layer + 1):
        lb = lb + sm[j]
    return lb


def _forget_gate(z, lbd, lb_is_zero):
    zs = z * LOG2E
    sp = jnp.log2(1.0 + jnp.exp2(-jnp.abs(zs)))
    ls = jnp.minimum(zs, 0.0) - sp
    sneg = jnp.exp2(jnp.minimum(-zs, 0.0) - sp)
    if lb_is_zero:
        return ls, sneg
    a = jnp.log2(lbd)
    t = jnp.log2(1.0 - lbd) + ls
    mx = jnp.maximum(a, t)
    mn = jnp.minimum(a, t)
    return mx + jnp.log2(1.0 + jnp.exp2(mn - mx)), (1.0 - lbd) * sneg


def _stream_specs(stream, nct, block_rows, ns=1):
    _, _, lat_off = stream
    d = stream[0].shape[2]
    return [pl.BlockSpec((ns, block_rows, d), lambda bi, i: (bi, jnp.minimum(i, nct - 1), 0)),
            pl.BlockSpec((ns, block_rows, d), lambda bi, i: (bi, jnp.maximum(i, nct) - lat_off, 0))]


def _inproj_kernel(xa_ref, xb_ref, mod_ref, g_ref, w_ref, cos_ref, sin_ref, qg_ref, kg_ref, gmat_ref,
                   hq_ref, zf_ref, zb_ref, hi_ref, hog_ref, cb_ref, cu_ref, aq_ref, ak_ref,
                   av_ref, ga_ref, gb_ref, gc_ref, *, d, hgw, cw, aw, kvw, hd, nct):
    x = jnp.where(pl.program_id(1) < nct, xa_ref[0], xb_ref[0])
    m = mod_ref[0]
    h = _modulated_norm(x, g_ref[...], m[:, 0:d], m[:, d:2 * d]).astype(BF16)

    def proj(lo, width):
        return _dot(h, w_ref[:, lo:lo + width])

    o_hq, o_zf, o_zb, o_hi, o_hog = (j * hgw for j in range(5))
    o_cb, o_cc, o_cx = (5 * hgw + j * cw for j in range(3))
    o_q = 5 * hgw + 3 * cw
    o_k, o_v = o_q + aw, o_q + aw + kvw
    o_ga, o_gb, o_gc = (o_v + kvw + j * d for j in range(3))
    cos = cos_ref[...]
    sin = sin_ref[...]
    gmat = gmat_ref[...]

    def gate(o, ref):
        ref[0] = _sigmoid(proj(o, d)).astype(BF16)

    hq_ref[0] = proj(o_hq, hgw).astype(BF16)
    zf_ref[0] = proj(o_zf, hgw)
    zb_ref[0] = proj(o_zb, hgw)
    hi_ref[0] = proj(o_hi, hgw).astype(BF16)
    g = proj(o_hog, hgw)
    hog_ref[0] = (g * _sigmoid(g)).astype(BF16)
    cb_ref[0] = proj(o_cb, cw).astype(BF16)
    cu_ref[0] = (proj(o_cc, cw) * proj(o_cx, cw)).astype(BF16)
    aq_ref[0] = _headnorm_rope(proj(o_q, aw), qg_ref[...], gmat, cos, sin, hd, hd ** -0.5 * LOG2E).astype(BF16)
    kn_t = _headnorm_rope(proj(o_k, kvw), kg_ref[...], gmat, cos, sin, hd, 1.0).T.astype(BF16)
    vv = proj(o_v, kvw).astype(BF16)
    ones_col = jnp.where(lax.broadcasted_iota(jnp.int32, (vv.shape[0], hd), 1) == 0, 1.0, 0.0).astype(BF16)
    for g in range(kvw // hd):
        ak_ref[0, g] = kn_t[g * hd:(g + 1) * hd, :]
        av_ref[0, g] = jnp.concatenate([vv[:, g * hd:(g + 1) * hd], ones_col], axis=1)
    gate(o_ga, ga_ref)
    gate(o_gb, gb_ref)
    gate(o_gc, gc_ref)


def _in_projection(stream, mod_l, norm_g, w_bf, cos_t, sin_t, qg, kg, gmat, dims):
    b, _, d = stream[0].shape
    t = dims["t"]
    hgw, cw, aw, kvw, hd, nct = dims["hgw"], dims["cw"], dims["aw"], dims["kvw"], dims["hd"], dims["nct"]
    nt = t // TOK_TILE
    in_w = w_bf.shape[1]
    nb = mod_l.shape[0] - 1

    def tok(width):
        return pl.BlockSpec((1, TOK_TILE, width), lambda bi, i: (bi, i, 0))

    def const(shape):
        return pl.BlockSpec(shape, lambda bi, i: (0,) * len(shape))

    widths = [(hgw, BF16), (hgw, F32), (hgw, F32), (hgw, BF16), (hgw, BF16),
              (cw, BF16), (cw, BF16), (aw, BF16), (kvw, BF16), (kvw, BF16), (d, BF16), (d, BF16), (d, BF16)]
    ng = kvw // hd
    out_specs = [tok(w) for w, _ in widths]
    out_shape = [jax.ShapeDtypeStruct((b, t, w), dt) for w, dt in widths]
    out_specs[8] = pl.BlockSpec((1, ng, hd, TOK_TILE), lambda bi, i: (bi, 0, 0, i))
    out_shape[8] = jax.ShapeDtypeStruct((b, ng, hd, t), BF16)
    out_specs[9] = pl.BlockSpec((1, ng, TOK_TILE, 2 * hd), lambda bi, i: (bi, 0, i, 0))
    out_shape[9] = jax.ShapeDtypeStruct((b, ng, t, 2 * hd), BF16)
    return pl.pallas_call(
        functools.partial(_inproj_kernel, d=d, hgw=hgw, cw=cw, aw=aw, kvw=kvw, hd=hd, nct=nct),
        grid=(b, nt),
        in_specs=_stream_specs(stream, nct, TOK_TILE) + [
            pl.BlockSpec((1, 1, N_MOD * d), lambda bi, i: (jnp.where(i < nct, nb, bi), 0, 0)),
            const((1, d)),
            pl.BlockSpec((d, in_w), lambda bi, i: (0, 0), pipeline_mode=pl.Buffered(1)),
            pl.BlockSpec((TOK_TILE, LANES), lambda bi, i: (i, 0)),
            pl.BlockSpec((TOK_TILE, LANES), lambda bi, i: (i, 0)),
            const((1, LANES)),
            const((1, LANES)),
            const((LANES, LANES)),
        ],
        out_specs=out_specs,
        out_shape=out_shape,
        compiler_params=_params(("parallel", "parallel")),
        name="in_projection",
    )(stream[0], stream[1], mod_l, norm_g, w_bf, cos_t, sin_t, qg, kg, gmat)


def _chunk_cumsum(x, reverse):
    n = x.shape[0]
    pos = lax.broadcasted_iota(jnp.int32, x.shape, 0) % HG_CHUNK
    s = 1
    while s < HG_CHUNK:
        if reverse:
            x = x + jnp.where(pos < HG_CHUNK - s, pltpu.roll(x, n - s, axis=0), 0.0)
        else:
            x = x + jnp.where(pos >= s, pltpu.roll(x, s, axis=0), 0.0)
        s *= 2
    return x


def _hgrn_chunk(q, z, lbd, lb_is_zero, v, st_ref, reverse):
    c = q.shape[0]
    lf, kk = _forget_gate(z, lbd, lb_is_zero)
    cs = _chunk_cumsum(lf, reverse)
    nsb = c // HG_SUB
    anchors = []
    for i in range(nsb):
        r = i * HG_SUB + (HG_SUB // 2 if reverse else HG_SUB // 2 - 1)
        anchors.append(cs[r:r + 1, :])
    c_anchor = jnp.concatenate([jnp.broadcast_to(a, (HG_SUB, a.shape[1])) for a in anchors], axis=0)
    c_end = cs[0:1, :] if reverse else cs[c - 1:c, :]
    qh = q * jnp.exp2(cs - c_anchor)

    def seg_rows(i, rows_of):
        return jnp.concatenate([rows_of(j) if keep_j else jnp.zeros((HG_SUB, q.shape[1]), F32)
                                for j, keep_j in enumerate(i)], axis=0)

    q_ext = jnp.concatenate(
        [seg_rows([j == i for j in range(nsb)], lambda j: qh[j * HG_SUB:(j + 1) * HG_SUB]) for i in range(nsb)],
        axis=1)
    k_ext = jnp.concatenate(
        [seg_rows([(j >= i) if reverse else (j <= i) for j in range(nsb)],
                  lambda j, i=i: kk[j * HG_SUB:(j + 1) * HG_SUB]
                  * jnp.exp2(anchors[i] - cs[j * HG_SUB:(j + 1) * HG_SUB])) for i in range(nsb)],
        axis=1)
    a = _dot_nt(q_ext.astype(BF16), k_ext.astype(BF16))
    row = lax.broadcasted_iota(jnp.int32, (c, c), 0)
    col = lax.broadcasted_iota(jnp.int32, (c, c), 1)
    keep = (col >= row) if reverse else (col <= row)
    st = st_ref[...]
    o = (_dot(jnp.where(keep, a, 0.0).astype(BF16), v)
         + _dot_nt((qh * jnp.exp2(c_anchor)).astype(BF16), st.astype(BF16)))
    kh = (kk * jnp.exp2(c_end - cs)).astype(BF16)
    st_ref[...] = st * jnp.exp2(c_end) + _dot_tn(v, kh)
    return o


def _hgrn_kernel(lbl_ref, hq_ref, zf_ref, zb_ref, hi_ref, hog_ref, ng_ref, a_ref,
                 of_ref, ob_ref, stf_ref, stb_ref, *, layer, n_chunks, n_ctx_chunks, n_heads):
    stf_ref[...] = jnp.zeros_like(stf_ref)
    stb_ref[...] = jnp.zeros_like(stb_ref)
    c = HG_CHUNK
    lb = _lower_bounds(lbl_ref[...], layer)

    def step(s, carry):
        rf = pl.multiple_of(s * c, c)
        cbk = jnp.where(s < n_ctx_chunks, n_ctx_chunks - 1 - s, n_chunks - 1 - (s - n_ctx_chunks))
        rb = pl.multiple_of(cbk * c, c)
        for h in range(n_heads):
            ln = slice(h * HG_DK, (h + 1) * HG_DK)
            of_ref[pl.ds(rf, c), ln] = _hgrn_chunk(
                hq_ref[0, pl.ds(rf, c), ln].astype(F32), zf_ref[0, pl.ds(rf, c), ln], lb[0:1, ln], layer == 0,
                hi_ref[0, pl.ds(rf, c), ln], stf_ref.at[h], False)
            ob_ref[pl.ds(rb, c), ln] = _hgrn_chunk(
                hq_ref[0, pl.ds(rb, c), ln].astype(F32), zb_ref[0, pl.ds(rb, c), ln], lb[1:2, ln], layer == 0,
                hi_ref[0, pl.ds(rb, c), ln], stb_ref.at[h], True)
        return carry

    lax.fori_loop(0, n_chunks, step, 0, unroll=HG_UNROLL)

    def readout(i, carry):
        r0 = pl.multiple_of(i * TOK_TILE, TOK_TILE)
        for h in range(n_heads):
            ln = slice(h * HG_DK, (h + 1) * HG_DK)
            o = of_ref[pl.ds(r0, TOK_TILE), ln] + ob_ref[pl.ds(r0, TOK_TILE), ln]
            ms = jnp.mean(o * o, axis=-1, keepdims=True)
            y = o * lax.rsqrt(ms + EPS) * ng_ref[:, ln]
            a_ref[0, pl.ds(r0, TOK_TILE), ln] = (y * hog_ref[0, pl.ds(r0, TOK_TILE), ln].astype(F32)).astype(BF16)
        return carry

    lax.fori_loop(0, (n_chunks * c) // TOK_TILE, readout, 0)


def _hgrn(hq, zf, zb, hi, hog, lb_logits, norm_g, layer, ctx_len):
    b, t, hgw = hq.shape
    nh = HG_HEADS_PER_STEP
    wid = nh * HG_DK
    depth = lb_logits.shape[0]

    def seq():
        return pl.BlockSpec((1, t, wid), lambda bi, h: (bi, 0, h))

    return pl.pallas_call(
        functools.partial(_hgrn_kernel, layer=layer, n_chunks=t // HG_CHUNK, n_ctx_chunks=ctx_len // HG_CHUNK,
                          n_heads=nh),
        grid=(b, hgw // wid),
        in_specs=[pl.BlockSpec((depth, 2, wid), lambda bi, h: (0, 0, h)),
                  seq(), seq(), seq(), seq(), seq(),
                  pl.BlockSpec((1, wid), lambda bi, h: (0, h))],
        out_specs=seq(),
        out_shape=jax.ShapeDtypeStruct((b, t, hgw), BF16),
        scratch_shapes=[pltpu.VMEM((t, wid), F32), pltpu.VMEM((t, wid), F32),
                        pltpu.VMEM((nh, HG_DK, HG_DK), F32), pltpu.VMEM((nh, HG_DK, HG_DK), F32)],
        compiler_params=_params(("parallel", "parallel")),
        name="hgrn2_bidirectional",
    )(lb_logits, hq, zf, zb, hi, hog, norm_g)


def _attn_kernel(q_ref, k_ref, v_ref, o_ref, *, n_g, group, hd, nct, ctx_len, t_all):
    i = pl.program_id(2)

    def run(n_keys):
        outs = []
        for g in range(n_g):
            kt = k_ref[0, g, :, 0:n_keys]
            v = v_ref[0, g, 0:n_keys, :]
            for h in range(group):
                c0 = (g * group + h) * hd
                s = _dot(q_ref[0, :, c0:c0 + hd], kt)
                p = jnp.exp2(s - jnp.max(s, axis=1, keepdims=True))
                ov = _dot(p.astype(BF16), v)
                outs.append((ov[:, 0:hd] / ov[:, hd:hd + 1]).astype(BF16))
        o_ref[0] = jnp.concatenate(outs, axis=1)

    @pl.when(i < nct)
    def _():
        run(ctx_len)

    @pl.when(i >= nct)
    def _():
        run(t_all)


def _attention(aq, ak, av, ctx_len):
    b, t, aw = aq.shape
    n_groups, hd = ak.shape[1], ak.shape[2]
    group = aw // (n_groups * hd)
    nt = t // TOK_TILE
    n_g = ATTN_GROUPS_PER_STEP if n_groups % ATTN_GROUPS_PER_STEP == 0 else 1
    return pl.pallas_call(
        functools.partial(_attn_kernel, n_g=n_g, group=group, hd=hd, nct=ctx_len // TOK_TILE, ctx_len=ctx_len,
                          t_all=t),
        grid=(b, n_groups // n_g, nt),
        in_specs=[
            pl.BlockSpec((1, TOK_TILE, n_g * group * hd), lambda bi, g, i: (bi, i, g)),
            pl.BlockSpec((1, n_g, hd, t), lambda bi, g, i: (bi, g, 0, 0)),
            pl.BlockSpec((1, n_g, t, 2 * hd), lambda bi, g, i: (bi, g, 0, 0)),
        ],
        out_specs=pl.BlockSpec((1, TOK_TILE, n_g * group * hd), lambda bi, g, i: (bi, i, g)),
        out_shape=jax.ShapeDtypeStruct((b, t, aw), BF16),
        compiler_params=_params(("parallel", "parallel", "parallel")),
        name="gqa_attention",
    )(aq, ak, av)


def _merge_kernel(xa_ref, xb_ref, mod_ref, a_ref, cb_ref, cu_ref, cup_ref, cun_ref, att_ref, ga_ref, gb_ref, gc_ref,
                  cw_ref, wa_ref, wb_ref, wc_ref, wo_ref, g2_ref, rwt_ref,
                  x1_ref, h2_ref, aff_ref, *, d, nct, nt):
    i = pl.program_id(1)
    ns, rows = xa_ref.shape[0], xa_ref.shape[1]
    n = ns * rows

    def stacked(ref):
        return ref[...].reshape(n, ref.shape[2])

    row = lax.broadcasted_iota(jnp.int32, (n, 1), 0)

    def per_sample(lo):
        out = mod_ref[ns - 1][:, lo:lo + d]
        for s in range(ns - 2, -1, -1):
            out = jnp.where(row < (s + 1) * rows, mod_ref[s][:, lo:lo + d], out)
        return out

    u = stacked(cu_ref).astype(F32)
    has_prev = jnp.logical_and(i != 0, i != nct)
    has_next = jnp.logical_and(i != nct - 1, i != nt - 1)
    u_prev = pltpu.roll(u, 1, axis=0)
    u_next = pltpu.roll(u, n - 1, axis=0)
    for s in range(ns):
        prev_row = jnp.where(has_prev, cup_ref[s, HALO - 1:HALO, :].astype(F32), 0.0)
        next_row = jnp.where(has_next, cun_ref[s, 0:1, :].astype(F32), 0.0)
        u_prev = jnp.where(row == s * rows, prev_row, u_prev)
        u_next = jnp.where(row == (s + 1) * rows - 1, next_row, u_next)
    cw = cw_ref[...]
    conv = cw[0:1, :] * u_prev + cw[1:2, :] * u + cw[2:3, :] * u_next
    bb = (stacked(cb_ref).astype(F32) * conv).astype(BF16)
    y = (stacked(ga_ref).astype(F32) * _dot(stacked(a_ref), wa_ref[...])
         + stacked(gb_ref).astype(F32) * _dot(bb, wb_ref[...])
         + stacked(gc_ref).astype(F32) * _dot(stacked(att_ref), wc_ref[...]))
    x = jnp.where(i < nct, stacked(xa_ref), stacked(xb_ref))
    x1 = x + per_sample(2 * d) * _dot(y.astype(BF16), wo_ref[...])
    x1_ref[...] = x1.reshape(ns, rows, d)
    h2 = _modulated_norm(x1, g2_ref[...], per_sample(3 * d), per_sample(4 * d))
    h2_ref[...] = h2.astype(BF16).reshape(ns, rows, d)
    logits = _dot_nt(rwt_ref[...], h2, precision=HIGHEST)
    ex = jnp.exp(logits - jnp.max(logits, axis=0, keepdims=True))
    aff = ex / jnp.sum(ex, axis=0, keepdims=True)
    for s in range(ns):
        aff_ref[s] = aff[:, s * rows:(s + 1) * rows]


def _merge(stream, mod_l, a, cb, cu, att, ga, gb, gc, conv_w, wa, wb, wc, wo, norm2_g, rwt, nct):
    b, t, d = a.shape[0], a.shape[1], stream[0].shape[2]
    nt = t // TOK_TILE
    ne = rwt.shape[0]
    ns = MERGE_SAMPLES if b % MERGE_SAMPLES == 0 else 1
    mod_m = jnp.concatenate([mod_l[:b]] + [mod_l[b:b + 1]] * ns, axis=0)
    ctx_blk = b // ns
    sub = TOK_TILE // HALO
    n8 = t // HALO

    def tok(width):
        return pl.BlockSpec((ns, TOK_TILE, width), lambda bi, i: (bi, i, 0))

    def const(shape):
        return pl.BlockSpec(shape, lambda bi, i: (0,) * len(shape))

    cwid = cu.shape[2]
    return pl.pallas_call(
        functools.partial(_merge_kernel, d=d, nct=nct, nt=nt),
        grid=(b // ns, nt),
        in_specs=_stream_specs(stream, nct, TOK_TILE, ns) + [
            pl.BlockSpec((ns, 1, N_MOD * d), lambda bi, i: (jnp.where(i < nct, ctx_blk, bi), 0, 0)),
            tok(a.shape[2]), tok(cwid), tok(cwid),
            pl.BlockSpec((ns, HALO, cwid), lambda bi, i: (bi, jnp.maximum(i * sub - 1, 0), 0)),
            pl.BlockSpec((ns, HALO, cwid), lambda bi, i: (bi, jnp.minimum((i + 1) * sub, n8 - 1), 0)),
            tok(att.shape[2]), tok(d), tok(d), tok(d),
            const(conv_w.shape), const(wa.shape), const(wb.shape), const(wc.shape), const(wo.shape),
            const((1, d)), const(rwt.shape),
        ],
        out_specs=[tok(d), tok(d), pl.BlockSpec((ns, ne, TOK_TILE), lambda bi, i: (bi, 0, i))],
        out_shape=[jax.ShapeDtypeStruct((b, t, d), F32), jax.ShapeDtypeStruct((b, t, d), BF16),
                   jax.ShapeDtypeStruct((b, ne, t), F32)],
        compiler_params=_params(("parallel", "parallel")),
        name="merge_residual_router",
    )(stream[0], stream[1], mod_m, a, cb, cu, cu, cu, att, ga, gb, gc, conv_w, wa, wb, wc, wo, norm2_g, rwt)


def _prefix_count(mask, tri):
    e, n = mask.shape
    carry = jnp.zeros((e, 1), F32)
    outs = []
    for blk in range(n // LANES):
        xb = jnp.where(mask[:, blk * LANES:(blk + 1) * LANES], 1.0, 0.0).astype(BF16)
        pre = _dot(xb, tri) + carry
        carry = pre[:, LANES - 1:LANES]
        outs.append(pre)
    return jnp.concatenate(outs, axis=1) if len(outs) > 1 else outs[0]


def _topk_kernel(aff_ref, tri_ref, tind_ref, sut_ref, pos_ref, before_ref, *, off, n, cap):
    a = aff_ref[0][:, off:off + n]
    bits = pltpu.bitcast(a, jnp.int32)
    thr = jnp.zeros((a.shape[0], 1), jnp.int32)
    for bit in range(30, -1, -1):
        cand = thr | (1 << bit)
        cnt = jnp.sum(jnp.where(bits >= cand, 1.0, 0.0), axis=1, keepdims=True)
        thr = jnp.where(cnt >= cap, cand, thr)
    gt = bits > thr
    eq = bits == thr
    need = cap - jnp.sum(jnp.where(gt, 1.0, 0.0), axis=1, keepdims=True)
    tri = tri_ref[...]
    eq_rank = _prefix_count(eq, tri)
    sel = jnp.logical_or(gt, jnp.logical_and(eq, eq_rank <= need))
    slot = _prefix_count(sel, tri) - 1.0
    pos_ref[0] = jnp.where(sel, slot, -1.0).astype(jnp.int32)
    tile_cnt = _dot(jnp.where(sel, 1.0, 0.0).astype(BF16), tind_ref[...])
    before_ref[0] = _dot(tile_cnt.astype(BF16), sut_ref[...]).astype(jnp.int32)


def _topk_positions(aff_t, tri, off, n, cap):
    b, ne, t = aff_t.shape
    assert n // TOK_TILE < LANES and TOK_TILE <= 256
    lane = jnp.arange(LANES)
    tind = (jnp.arange(n)[:, None] // TOK_TILE == lane[None, :]).astype(BF16)
    sut = (lane[:, None] < lane[None, :]).astype(BF16)

    def const(shape):
        return pl.BlockSpec(shape, lambda bi: (0,) * len(shape))

    return pl.pallas_call(
        functools.partial(_topk_kernel, off=off, n=n, cap=cap),
        grid=(b,),
        in_specs=[pl.BlockSpec((1, ne, t), lambda bi: (bi, 0, 0)), const((LANES, LANES)), const(tind.shape),
                  const(sut.shape)],
        out_specs=[pl.BlockSpec((1, ne, n), lambda bi: (bi, 0, 0)), pl.BlockSpec((1, ne, LANES), lambda bi: (bi, 0, 0))],
        out_shape=[jax.ShapeDtypeStruct((b, ne, n), jnp.int32), jax.ShapeDtypeStruct((b, ne, LANES), jnp.int32)],
        compiler_params=_params(("parallel",)),
        name="expert_choice_topk",
    )(aff_t, tri, tind, sut)


def _gather_kernel(before_ref, h_ref, pos_ref, xs_ref, *, ne, cap, win):
    bi = pl.program_id(0)
    tt = pl.program_id(1)

    @pl.when(tt == 0)
    def _():
        xs_ref[...] = jnp.zeros_like(xs_ref)

    h = h_ref[0]
    pos = pos_ref[0]
    rows = h.shape[0]
    sub = lax.broadcasted_iota(jnp.int32, (win, rows), 0)
    hots, spans = [], []
    overflow = False
    for e in range(ne):
        base = (bi * ne + e) * LANES
        c0 = before_ref[base + tt]
        c1 = before_ref[base + tt + 1]
        w0 = pl.multiple_of(jnp.minimum((c0 // SLOT_ALIGN) * SLOT_ALIGN, cap - win), SLOT_ALIGN)
        hots.append(jnp.where(pos[e:e + 1, :] - w0 == sub, 1.0, 0.0).astype(BF16))
        spans.append((c1, w0))
        overflow = jnp.logical_or(overflow, c1 > w0 + win)
    picked = _dot(jnp.concatenate(hots, axis=0), h).astype(BF16)
    for e, (c1, w0) in enumerate(spans):
        xs_ref[e, pl.ds(w0, win), :] += picked[e * win:(e + 1) * win, :]

    @pl.when(overflow)
    def _():
        for e, (c1, w0) in enumerate(spans):
            def more(k, carry, e=e, w0=w0):
                ws = w0 + k * win
                wc = pl.multiple_of(jnp.minimum(ws, cap - win), SLOT_ALIGN)
                hit = jnp.logical_and(pos_ref[0][e:e + 1, :] - wc == sub, sub >= ws - wc)
                xs_ref[e, pl.ds(wc, win), :] += _dot(jnp.where(hit, 1.0, 0.0).astype(BF16), h_ref[0]).astype(BF16)
                return carry

            lax.fori_loop(1, (c1 - w0 + win - 1) // win, more, 0)


def _gather_tokens(h2, pos, before, off, n, cap):
    b, t, d = h2.shape
    ne = pos.shape[1]
    ot = off // TOK_TILE
    win = min(cap, SLOT_WINDOW)
    grid_spec = pltpu.PrefetchScalarGridSpec(
        num_scalar_prefetch=1,
        grid=(b, n // TOK_TILE),
        in_specs=[pl.BlockSpec((1, TOK_TILE, d), lambda bi, tt, bf: (bi, tt + ot, 0)),
                  pl.BlockSpec((1, ne, TOK_TILE), lambda bi, tt, bf: (bi, 0, tt))],
        out_specs=pl.BlockSpec((ne, cap, d), lambda bi, tt, bf: (0, bi, 0)),
    )
    return pl.pallas_call(
        functools.partial(_gather_kernel, ne=ne, cap=cap, win=win),
        grid_spec=grid_spec,
        out_shape=jax.ShapeDtypeStruct((ne, b * cap, d), BF16),
        compiler_params=_params(("parallel", "arbitrary")),
        name="expert_gather",
    )(before.reshape(-1), h2, pos)


def _ffn_kernel(xs_ref, wg_ref, wu_ref, wd_ref, ys_ref, wgb_ref, wub_ref, wdb_ref):
    @pl.when(pl.program_id(1) == 0)
    def _():
        wgb_ref[...] = wg_ref[0, 0].astype(BF16)
        wub_ref[...] = wu_ref[0, 0].astype(BF16)
        wdb_ref[...] = wd_ref[0, 0].astype(BF16)

    x = xs_ref[0]
    ff = wgb_ref.shape[1]
    y = None
    for f0 in range(0, ff, ff // FFN_SPLIT):
        f1 = f0 + ff // FFN_SPLIT
        g = _dot(x, wgb_ref[:, f0:f1])
        hid = (g * _sigmoid(g)) * _dot(x, wub_ref[:, f0:f1])
        part = _dot(hid.astype(BF16), wdb_ref[f0:f1, :])
        y = part if y is None else y + part
    ys_ref[0] = y.astype(BF16)


def _expert_ffn(xs, wg, wu, wd, layer):
    ne, m, d = xs.shape
    ff = wg.shape[3]
    tm = min(m, 512)
    return pl.pallas_call(
        _ffn_kernel,
        grid=(ne, m // tm),
        in_specs=[pl.BlockSpec((1, tm, d), lambda e, j: (e, j, 0)),
                  pl.BlockSpec((1, 1, d, ff), lambda e, j: (layer, e, 0, 0)),
                  pl.BlockSpec((1, 1, d, ff), lambda e, j: (layer, e, 0, 0)),
                  pl.BlockSpec((1, 1, ff, d), lambda e, j: (layer, e, 0, 0))],
        out_specs=pl.BlockSpec((1, tm, d), lambda e, j: (e, j, 0)),
        out_shape=jax.ShapeDtypeStruct((ne, m, d), BF16),
        scratch_shapes=[pltpu.VMEM((d, ff), BF16), pltpu.VMEM((d, ff), BF16), pltpu.VMEM((ff, d), BF16)],
        compiler_params=_params(("parallel", "arbitrary")),
        name="expert_ffn",
    )(xs, wg, wu, wd)


def _scatter_kernel(before_ref, x_ref, mod_ref, ys_ref, pos_ref, aff_ref, fg_ref, o_ref, acc_ref, *,
                    d, ne, cap, win, final):
    bi = pl.program_id(0)
    i = pl.program_id(1)
    m = mod_ref[0]
    pos = pos_ref[0]
    aff = aff_ref[0]
    rows = pos.shape[0]
    lane = lax.broadcasted_iota(jnp.int32, (rows, win), 1)
    acc = jnp.zeros((rows, d), F32)
    spans = []
    overflow = False
    for e in range(ne):
        base = (bi * ne + e) * LANES
        c0 = before_ref[base + i]
        c1 = before_ref[base + i + 1]
        w0 = pl.multiple_of(jnp.minimum((c0 // SLOT_ALIGN) * SLOT_ALIGN, cap - win), SLOT_ALIGN)
        onehot = jnp.where(pos[:, e:e + 1] - w0 == lane, 1.0, 0.0).astype(BF16)
        acc = acc + aff[:, e:e + 1] * _dot(onehot, ys_ref[e, pl.ds(w0, win), :])
        spans.append((c1, w0))
        overflow = jnp.logical_or(overflow, c1 > w0 + win)
    acc_ref[...] = acc

    @pl.when(overflow)
    def _():
        for e, (c1, w0) in enumerate(spans):
            def more(k, carry, e=e, w0=w0):
                ws = w0 + k * win
                wc = pl.multiple_of(jnp.minimum(ws, cap - win), SLOT_ALIGN)
                hit = jnp.logical_and(pos_ref[0][:, e:e + 1] - wc == lane, lane >= ws - wc)
                acc_ref[...] += aff_ref[0][:, e:e + 1] * _dot(jnp.where(hit, 1.0, 0.0).astype(BF16),
                                                              ys_ref[e, pl.ds(wc, win), :])
                return carry

            lax.fori_loop(1, (c1 - w0 + win - 1) // win, more, 0)

    x2 = x_ref[0] + m[:, 5 * d:6 * d] * acc_ref[...]
    if final:
        ms = jnp.mean(x2 * x2, axis=-1, keepdims=True)
        x2 = x2 * lax.rsqrt(ms + EPS) * fg_ref[...]
    o_ref[0] = x2


def _scatter_residual(x1, mod_l, ys, pos_n, aff_n, before, final_g, off, n, cap, mod_row_ctx, final):
    b, t, d = x1.shape
    ne = ys.shape[0]
    nb = mod_l.shape[0] - 1
    ot = off // TOK_TILE
    win = min(cap, SLOT_WINDOW)
    if final:
        out_shape = jax.ShapeDtypeStruct((b, n, d), F32)
        out_spec = pl.BlockSpec((1, TOK_TILE, d), lambda bi, i, bf: (bi, i, 0))
        aliases = {}
    else:
        out_shape = jax.ShapeDtypeStruct((b, t, d), F32)
        out_spec = pl.BlockSpec((1, TOK_TILE, d), lambda bi, i, bf: (bi, i + ot, 0))
        aliases = {1: 0}
    grid_spec = pltpu.PrefetchScalarGridSpec(
        num_scalar_prefetch=1,
        grid=(b, n // TOK_TILE),
        in_specs=[
            pl.BlockSpec((1, TOK_TILE, d), lambda bi, i, bf: (bi, i + ot, 0)),
            pl.BlockSpec((1, 1, N_MOD * d), lambda bi, i, bf: (nb if mod_row_ctx else bi, 0, 0)),
            pl.BlockSpec((ne, cap, d), lambda bi, i, bf: (0, bi, 0)),
            pl.BlockSpec((1, TOK_TILE, ne), lambda bi, i, bf: (bi, i, 0)),
            pl.BlockSpec((1, TOK_TILE, ne), lambda bi, i, bf: (bi, i + ot, 0)),
            pl.BlockSpec((1, d), lambda bi, i, bf: (0, 0)),
        ],
        out_specs=out_spec,
        scratch_shapes=[pltpu.VMEM((TOK_TILE, d), F32)],
    )
    return pl.pallas_call(
        functools.partial(_scatter_kernel, d=d, ne=ne, cap=cap, win=win, final=final),
        grid_spec=grid_spec,
        out_shape=out_shape,
        input_output_aliases=aliases,
        compiler_params=_params(("parallel", "parallel")),
        name="expert_scatter_residual",
    )(before.reshape(-1), x1, mod_l, ys, pos_n, aff_n, final_g)


def _rope_tables(ctx_len, seq, hd):
    rows = seq // GRID_W
    row = jnp.repeat(jnp.arange(rows), GRID_W).astype(F32)
    col = jnp.tile(jnp.arange(GRID_W), rows).astype(F32)
    inv = ROPE_THETA ** (-jnp.arange(0, hd // 2, 2, dtype=F32) / (hd // 2))
    ang = jnp.concatenate([row[:, None] * inv, col[:, None] * inv], axis=-1)
    cos = jnp.repeat(jnp.cos(ang), 2, axis=-1)
    sin = jnp.repeat(jnp.sin(ang), 2, axis=-1) * jnp.tile(jnp.array([-1.0, 1.0], F32), hd // 2)
    cos = jnp.concatenate([jnp.ones((ctx_len, hd), F32), cos], axis=0)
    sin = jnp.concatenate([jnp.zeros((ctx_len, hd), F32), sin], axis=0)
    rep = LANES // hd
    return jnp.tile(cos, (1, rep)), jnp.tile(sin, (1, rep))


def kernel(x, c, ctx, c_ctx, ada_w, ada_b, norm1_g, norm2_g, w_in, hg_lb_logits, hg_norm_g, conv_w, q_norm_g, k_norm_g, w_proj_a, w_proj_b, w_proj_c, w_out, router_w, w_gate, w_up, w_down, final_norm_g):
    b, s, d = x.shape
    ctx_len = ctx.shape[1]
    t = ctx_len + s
    depth = w_in.shape[0]
    hgw = hg_norm_g.shape[1]
    cw = conv_w.shape[2]
    aw = w_proj_c.shape[1]
    hd = q_norm_g.shape[1]
    kvw = (w_in.shape[2] - 5 * hgw - 3 * cw - aw - 3 * d) // 2
    ne = router_w.shape[2]
    nct = ctx_len // TOK_TILE
    assert ctx_len % TOK_TILE == 0 and s % TOK_TILE == 0 and LANES % hd == 0 and cw == d
    dims = dict(hgw=hgw, cw=cw, aw=aw, kvw=kvw, hd=hd, nct=nct, t=t)

    stream = (ctx, x, nct)
    n_rows = -(-(b + 1) // 8) * 8
    cvec = jnp.concatenate([c, c_ctx[None, :], jnp.zeros((n_rows - b - 1, d), F32)], axis=0)
    mod = _modulation(cvec, ada_w, ada_b)
    mod = mod[:, :b + 1].reshape(depth, b + 1, 1, N_MOD * d)

    cos_t, sin_t = _rope_tables(ctx_len, s, hd)
    lane = jnp.arange(LANES)
    gmat = (lane[:, None] // hd == lane[None, :] // hd).astype(BF16)
    tri = (lane[:, None] <= lane[None, :]).astype(BF16)
    rep = LANES // hd
    lb_logits = hg_lb_logits.astype(F32)

    out = None
    for l in range(depth):
        last = l == depth - 1
        mod_l = mod[l]
        (hq, zf, zb, hi, hog, cb, cu, aq, ak, av, ga, gb, gc) = _in_projection(
            stream, mod_l, norm1_g[l][None, :], w_in[l].astype(BF16), cos_t, sin_t,
            jnp.tile(q_norm_g[l], rep)[None, :], jnp.tile(k_norm_g[l], rep)[None, :], gmat, dims)
        a = _hgrn(hq, zf, zb, hi, hog, lb_logits, hg_norm_g[l][None, :], l, ctx_len)
        att = _attention(aq, ak, av, ctx_len)
        x1, h2, aff_t = _merge(stream, mod_l, a, cb, cu, att, ga, gb, gc, conv_w[l],
                               w_proj_a[l].astype(BF16), w_proj_b[l].astype(BF16), w_proj_c[l].astype(BF16),
                               w_out[l].astype(BF16), norm2_g[l][None, :], router_w[l].T, nct)
        aff_n = jnp.swapaxes(aff_t, 1, 2)

        def moe(xs_in, off, n, mod_row_ctx, final):
            cap = CAPACITY_FACTOR * n // ne
            pos, before = _topk_positions(aff_t, tri, off, n, cap)
            xs = _gather_tokens(h2, pos, before, off, n, cap)
            ys = _expert_ffn(xs, w_gate, w_up, w_down, l)
            return _scatter_residual(xs_in, mod_l, ys, jnp.swapaxes(pos, 1, 2), aff_n, before,
                                     final_norm_g[None, :], off, n, cap, mod_row_ctx, final)

        if last:
            out = moe(x1, ctx_len, s, False, True)
        else:
            xc = moe(x1, ctx_len, s, False, False)
            xc = moe(xc, 0, ctx_len, True, False)
            stream = (xc, xc, 0)
    return out
```
